```python
import math
import jax, jax.numpy as jnp
from jax import lax
import numpy as np

D_MODEL = 1024
BATCH = 2
SEQ = 16384
DEPTH = 2
DEC_BATCH = 16
DEC_SEQ = 2048
PAST_LEN = 128

N_MIXERS = 2
N_ATTN_LAYERS = (DEPTH + 1) // 2
N_SGU_LAYERS = DEPTH // 2
DIFF_HEAD_DIM = 64
DIFF_HEADS = D_MODEL // (2 * DIFF_HEAD_DIM)
Q_BLOCK = 128
ROPE_THETA = 10000.0
CHUNK = 128
SGU_WIDTH = D_MODEL
SGU_GROUP_DIM = 128
SGU_GROUPS = SGU_WIDTH // SGU_GROUP_DIM
D_FF = 2816
CONV_WIDTH = 3
PLE_DIM = 256
EPS = 1e-6

kernel_name = "hybrid_diffattn_sgu_convffn_encoder"


def rms_norm(x, g):
    xf = x.astype(jnp.float32)
    y = xf * lax.rsqrt(jnp.mean(xf * xf, axis=-1, keepdims=True) + EPS)
    return (y * g.astype(jnp.float32)).astype(x.dtype)


def layer_norm(x, g, b):
    xf = x.astype(jnp.float32)
    mu = jnp.mean(xf, axis=-1, keepdims=True)
    xc = xf - mu
    y = xc * lax.rsqrt(jnp.mean(xc * xc, axis=-1, keepdims=True) + EPS)
    return (y * g.astype(jnp.float32) + b.astype(jnp.float32)).astype(x.dtype)


def rope_tables(seq, dtype):
    inv = 1.0 / (ROPE_THETA ** (jnp.arange(0, DIFF_HEAD_DIM, 2, dtype=jnp.float32) / DIFF_HEAD_DIM))
    ang = jnp.arange(seq, dtype=jnp.float32)[:, None] * inv[None, :]
    ang = jnp.concatenate([ang, ang], axis=-1)
    return jnp.cos(ang).astype(dtype), jnp.sin(ang).astype(dtype)


def apply_rope(x, cos, sin):
    x1, x2 = jnp.split(x, 2, axis=-1)
    rot = jnp.concatenate([-x2, x1], axis=-1)
    return x * cos[None, :, None, :] + rot * sin[None, :, None, :]


def diff_attention(h, w_qkv, lq1, lk1, lq2, lk2, subln_g, w_o, lambda_init):
    B, S, _ = h.shape
    q, k, v = jnp.split(h @ w_qkv, 3, axis=-1)
    q = q.reshape(B, S, 2 * DIFF_HEADS, DIFF_HEAD_DIM)
    k = k.reshape(B, S, 2 * DIFF_HEADS, DIFF_HEAD_DIM)
    v = v.reshape(B, S, DIFF_HEADS, 2 * DIFF_HEAD_DIM)
    cos, sin = rope_tables(S, h.dtype)
    q = apply_rope(q, cos, sin) * (DIFF_HEAD_DIM ** -0.5)
    k = apply_rope(k, cos, sin)
    f32 = jnp.float32
    lam = (jnp.exp(jnp.sum(lq1.astype(f32) * lk1.astype(f32)))
           - jnp.exp(jnp.sum(lq2.astype(f32) * lk2.astype(f32))) + lambda_init)
    nb = S // Q_BLOCK
    q_blocks = q.reshape(B, nb, Q_BLOCK, 2 * DIFF_HEADS, DIFF_HEAD_DIM).transpose(1, 0, 2, 3, 4)

    def block(qb):
        s = jnp.einsum('bqhd,bkhd->bhqk', qb, k, preferred_element_type=jnp.float32)
        p = jax.nn.softmax(s, axis=-1).reshape(B, DIFF_HEADS, 2, Q_BLOCK, S)
        a = p[:, :, 0] - lam * p[:, :, 1]
        return jnp.einsum('bhqk,bkhd->bqhd', a.astype(v.dtype), v)

    o = lax.map(block, q_blocks)
    o = o.transpose(1, 0, 2, 3, 4).reshape(B, S, DIFF_HEADS, 2 * DIFF_HEAD_DIM)
    o = rms_norm(o, subln_g) * (1.0 - lambda_init)
    return o.reshape(B, S, D_MODEL) @ w_o


def chunked_sgu(h, w_uv, ln_g, ln_b, w_s, b_s, w_out):
    B, S, _ = h.shape
    z = jax.nn.gelu(h @ w_uv, approximate=False)
    u, v = jnp.split(z, 2, axis=-1)
    v = layer_norm(v, ln_g, ln_b)
    nc = S // CHUNK
    v = v.reshape(B, nc, CHUNK, SGU_GROUPS, SGU_GROUP_DIM)
    mixed = jnp.einsum('gpq,bnqgc->bnpgc', w_s, v) + b_s.T[None, None, :, :, None]
    return (u * mixed.reshape(B, S, SGU_WIDTH)) @ w_out


def conv_ffn(h, w_up, conv_w, conv_b, w_down):
    a = h @ w_up
    S = a.shape[1]
    half = CONV_WIDTH // 2
    pad = jnp.pad(a, ((0, 0), (half, half), (0, 0)))
    c = conv_b
    for t in range(CONV_WIDTH):
        c = c + pad[:, t:t + S] * conv_w[t]
    val, gate = jnp.split(c, 2, axis=-1)
    return (jax.nn.silu(gate) * val) @ w_down


def trunk(x, p, norm_mix_g, attn_w_qkv, attn_lq1, attn_lk1, attn_lq2, attn_lk2, attn_subln_g, attn_w_o,
          sgu_w_uv, sgu_ln_g, sgu_ln_b, sgu_w_s, sgu_b_s, sgu_w_out,
          norm_ffn_g, ffn_w_up, ffn_conv_w, ffn_conv_b, ffn_w_down,
          norm_ple_g, ple_w_gate, ple_w_proj, final_norm_g):
    for i in range(DEPTH):
        h = rms_norm(x, norm_mix_g[i])
        j = i // N_MIXERS
        if i % N_MIXERS == 0:
            lambda_init = 0.8 - 0.6 * math.exp(-0.3 * i)
            x = x + diff_attention(h, attn_w_qkv[j], attn_lq1[j], attn_lk1[j], attn_lq2[j], attn_lk2[j],
                                   attn_subln_g[j], attn_w_o[j], lambda_init)
        else:
            x = x + chunked_sgu(h, sgu_w_uv[j], sgu_ln_g[j], sgu_ln_b[j], sgu_w_s[j], sgu_b_s[j], sgu_w_out[j])
        x = x + conv_ffn(rms_norm(x, norm_ffn_g[i]), ffn_w_up[i], ffn_conv_w[i], ffn_conv_b[i], ffn_w_down[i])
        gate = jax.nn.sigmoid(rms_norm(x, norm_ple_g[i]) @ ple_w_gate[i])
        x = x + gate * (p[i] @ ple_w_proj[i])
    return rms_norm(x, final_norm_g)


def setup_inputs(seed: int = 0) -> dict:
    key = jax.random.key(seed)
    ks = jax.random.split(key, 32)
    f32 = jnp.float32

    def nrm(k, shape, scale):
        return jax.random.normal(k, shape, f32) * scale

    D, W, F = D_MODEL, SGU_WIDTH, D_FF
    NA, NB = N_ATTN_LAYERS, N_SGU_LAYERS
    return {
        "x_prompt": nrm(ks[0], (BATCH, SEQ, D), 1.0),
        "x_sample": nrm(ks[1], (DEC_BATCH, DEC_SEQ, D), 1.0),
        "p_prompt": nrm(ks[2], (DEPTH, BATCH, SEQ, PLE_DIM), 1.0),
        "p_sample": nrm(ks[3], (DEPTH, DEC_BATCH, DEC_SEQ, PLE_DIM), 1.0),
        "norm_mix_g": 1.0 + nrm(ks[4], (DEPTH, D), 0.02),
        "attn_w_qkv": nrm(ks[5], (NA, D, 3 * D), D ** -0.5),
        "attn_lq1": nrm(ks[6], (NA, DIFF_HEAD_DIM), 0.1),
        "attn_lk1": nrm(ks[7], (NA, DIFF_HEAD_DIM), 0.1),
        "attn_lq2": nrm(ks[8], (NA, DIFF_HEAD_DIM), 0.1),
        "attn_lk2": nrm(ks[9], (NA, DIFF_HEAD_DIM), 0.1),
        "attn_subln_g": 1.0 + nrm(ks[10], (NA, 2 * DIFF_HEAD_DIM), 0.02),
        "attn_w_o": nrm(ks[11], (NA, D, D), 0.5 * D ** -0.5),
        "sgu_w_uv": nrm(ks[12], (NB, D, 2 * W), D ** -0.5),
        "sgu_ln_g": 1.0 + nrm(ks[13], (NB, W), 0.02),
        "sgu_ln_b": nrm(ks[14], (NB, W), 0.02),
        "sgu_w_s": nrm(ks[15], (NB, SGU_GROUPS, CHUNK, CHUNK), CHUNK ** -0.5),
        "sgu_b_s": 1.0 + nrm(ks[16], (NB, SGU_GROUPS, CHUNK), 0.02),
        "sgu_w_out": nrm(ks[17], (NB, W, D), 0.5 * W ** -0.5),
        "norm_ffn_g": 1.0 + nrm(ks[18], (DEPTH, D), 0.02),
        "ffn_w_up": nrm(ks[19], (DEPTH, D, 2 * F), D ** -0.5),
        "ffn_conv_w": nrm(ks[20], (DEPTH, CONV_WIDTH, 2 * F), CONV_WIDTH ** -0.5),
        "ffn_conv_b": nrm(ks[21], (DEPTH, 2 * F), 0.02),
        "ffn_w_down": nrm(ks[22], (DEPTH, F, D), 0.5 * F ** -0.5),
        "norm_ple_g": 1.0 + nrm(ks[23], (DEPTH, D), 0.02),
        "ple_w_gate": nrm(ks[24], (DEPTH, D, D), D ** -0.5),
        "ple_w_proj": nrm(ks[25], (DEPTH, PLE_DIM, D), 0.5 * PLE_DIM ** -0.5),
        "final_norm_g": 1.0 + nrm(ks[26], (D,), 0.02),
    }


def reference(x_prompt, x_sample, p_prompt, p_sample, norm_mix_g, attn_w_qkv, attn_lq1, attn_lk1, attn_lq2,
              attn_lk2, attn_subln_g, attn_w_o, sgu_w_uv, sgu_ln_g, sgu_ln_b, sgu_w_s, sgu_b_s, sgu_w_out,
              norm_ffn_g, ffn_w_up, ffn_conv_w, ffn_conv_b, ffn_w_down, norm_ple_g, ple_w_gate, ple_w_proj,
              final_norm_g):
    y_prompt = trunk(x_prompt, p_prompt, norm_mix_g, attn_w_qkv, attn_lq1, attn_lk1, attn_lq2, attn_lk2,
                     attn_subln_g, attn_w_o, sgu_w_uv, sgu_ln_g, sgu_ln_b, sgu_w_s, sgu_b_s, sgu_w_out,
                     norm_ffn_g, ffn_w_up, ffn_conv_w, ffn_conv_b, ffn_w_down, norm_ple_g, ple_w_gate,
                     ple_w_proj, final_norm_g)
    y_sample = trunk(x_sample, p_sample, norm_mix_g, attn_w_qkv, attn_lq1, attn_lk1, attn_lq2, attn_lk2,
                     attn_subln_g, attn_w_o, sgu_w_uv, sgu_ln_g, sgu_ln_b, sgu_w_s, sgu_b_s, sgu_w_out,
                     norm_ffn_g, ffn_w_up, ffn_conv_w, ffn_conv_b, ffn_w_down, norm_ple_g, ple_w_gate,
                     ple_w_proj, final_norm_g)
    return (y_prompt, y_sample)
```

```python
import functools
import math

import jax
import jax.numpy as jnp
from jax import lax
from jax.experimental import pallas as pl
from jax.experimental.pallas import tpu as pltpu

F32 = jnp.float32
BF16 = jnp.bfloat16

EPS = 1e-6
ROPE_THETA = 10000.0
HEAD_DIM = 64
CHUNK = 128
CONV_WIDTH = 3
HALO = 8
LANES = 128

VMEM_LIMIT = 56 * 1024 * 1024


def _rms(x, g):
    return x * lax.rsqrt(jnp.mean(x * x, axis=-1, keepdims=True) + EPS) * g


def _const_spec(shape):
    nd = len(shape)
    return pl.BlockSpec(shape, lambda *_: (0,) * nd, pipeline_mode=pl.Buffered(1))


def _params(n_axes):
    return pltpu.CompilerParams(dimension_semantics=("arbitrary",) * n_axes,
                                vmem_limit_bytes=VMEM_LIMIT)


def _attn_pre_kernel(x_ref, g_ref, w_ref, cq_ref, sq_ref, ck_ref, sk_ref, q_ref, k_ref, v_ref):
    tm, d = x_ref.shape
    h = _rms(x_ref[...], g_ref[...]).astype(BF16)
    qkv = jnp.dot(h, w_ref[...], preferred_element_type=F32)
    lane = lax.broadcasted_iota(jnp.int32, (tm, LANES), 1)
    first_half = (lane % HEAD_DIM) < (HEAD_DIM // 2)

    def rope(t, cos, sin):
        rot = jnp.where(first_half, pltpu.roll(t, LANES - HEAD_DIM // 2, 1), pltpu.roll(t, HEAD_DIM // 2, 1))
        return t * cos + rot * sin

    cq, sq, ck, sk = cq_ref[...], sq_ref[...], ck_ref[...], sk_ref[...]
    for j in range(d // LANES):
        cols = slice(j * LANES, (j + 1) * LANES)
        q_ref[:, cols] = rope(qkv[:, j * LANES:(j + 1) * LANES], cq, sq).astype(BF16)
        k_ref[:, cols] = rope(qkv[:, d + j * LANES:d + (j + 1) * LANES], ck, sk).astype(BF16)
    v_ref[...] = qkv[:, 2 * d:].astype(BF16)


def _attn_pre(x, g, w_qkv, tables, seq, tm):
    n, d = x.shape
    tps = seq // tm
    tab_spec = pl.BlockSpec((tm, LANES), lambda i: (i % tps, 0))
    out = jax.ShapeDtypeStruct((n, d), BF16)
    return pl.pallas_call(
        _attn_pre_kernel,
        grid=(n // tm,),
        in_specs=[pl.BlockSpec((tm, d), lambda i: (i, 0)), _const_spec((1, d)), _const_spec((d, 3 * d)),
                  tab_spec, tab_spec, tab_spec, tab_spec],
        out_specs=[pl.BlockSpec((tm, d), lambda i: (i, 0))] * 3,
        out_shape=[out, out, out],
        compiler_params=_params(1),
        name="attn_pre",
    )(x, g, w_qkv, *tables)


def _flash_kernel(q_ref, k_ref, v_ref, lq1_ref, lk1_ref, lq2_ref, lk2_ref, g_ref, o_ref,
                  q2_s, m_s, l_s, acc_s, *, tk, lambda_init):
    tq = q_ref.shape[1]
    seq = k_ref.shape[1]
    q = q_ref[0]
    lane = lax.broadcasted_iota(jnp.int32, q.shape, 1)
    zero = jnp.zeros_like(q)
    q2_s[0:tq, :] = jnp.where(lane < HEAD_DIM, q, zero)
    q2_s[tq:2 * tq, :] = jnp.where(lane >= HEAD_DIM, q, zero)
    m_s[...] = jnp.full(m_s.shape, -jnp.inf, F32)
    l_s[...] = jnp.zeros(l_s.shape, F32)
    acc_s[...] = jnp.zeros(acc_s.shape, F32)

    def body(j, carry):
        start = pl.multiple_of(j * tk, tk)
        kc = k_ref[0, pl.ds(start, tk), :]
        vc = v_ref[0, pl.ds(start, tk), :]
        s = lax.dot_general(q2_s[...], kc, (((1,), (1,)), ((), ())), preferred_element_type=F32)
        m_old = m_s[...]
        m_new = jnp.maximum(m_old, jnp.max(s, axis=-1, keepdims=True))
        alpha = jnp.exp2(m_old - m_new)
        p = jnp.exp2(s - m_new)
        l_s[...] = alpha * l_s[...] + jnp.sum(p, axis=-1, keepdims=True)
        acc_s[...] = alpha * acc_s[...] + jnp.dot(p.astype(BF16), vc, preferred_element_type=F32)
        m_s[...] = m_new
        return carry

    lax.fori_loop(0, seq // tk, body, 0)

    o_maps = acc_s[...] / l_s[...]
    lam = (jnp.exp(jnp.sum(lq1_ref[...] * lk1_ref[...], axis=-1, keepdims=True))
           - jnp.exp(jnp.sum(lq2_ref[...] * lk2_ref[...], axis=-1, keepdims=True)) + lambda_init)
    o = o_maps[0:tq] - lam * o_maps[tq:2 * tq]
    o_ref[0] = (_rms(o, g_ref[...]) * (1.0 - lambda_init)).astype(o_ref.dtype)


def _flash(q, k, v, lq1, lk1, lq2, lk2, subln_g, lambda_init, tq, tk):
    b, seq, d = q.shape
    hd = 2 * HEAD_DIM
    small = _const_spec((1, HEAD_DIM))
    return pl.pallas_call(
        functools.partial(_flash_kernel, tk=tk, lambda_init=lambda_init),
        grid=(b, d // hd, seq // tq),
        in_specs=[pl.BlockSpec((1, tq, hd), lambda bi, h, qi: (bi, qi, h)),
                  pl.BlockSpec((1, seq, hd), lambda bi, h, qi: (bi, 0, h)),
                  pl.BlockSpec((1, seq, hd), lambda bi, h, qi: (bi, 0, h)),
                  small, small, small, small, _const_spec((1, hd))],
        out_specs=pl.BlockSpec((1, tq, hd), lambda bi, h, qi: (bi, qi, h)),
        out_shape=jax.ShapeDtypeStruct((b, seq, d), BF16),
        scratch_shapes=[pltpu.VMEM((2 * tq, hd), BF16), pltpu.VMEM((2 * tq, 1), F32),
                        pltpu.VMEM((2 * tq, 1), F32), pltpu.VMEM((2 * tq, hd), F32)],
        compiler_params=_params(3),
        name="flash_diff_attn",
    )(q, k, v, lq1, lk1, lq2, lk2, subln_g)


def _attn_post_kernel(x_ref, o_ref, w_ref, y_ref):
    y_ref[...] = x_ref[...] + jnp.dot(o_ref[...], w_ref[...], preferred_element_type=F32)


def _attn_post(x, o, w_o, tm):
    n, d = x.shape
    row = pl.BlockSpec((tm, d), lambda i: (i, 0))
    return pl.pallas_call(
        _attn_post_kernel,
        grid=(n // tm,),
        in_specs=[row, row, _const_spec((d, d))],
        out_specs=row,
        out_shape=jax.ShapeDtypeStruct((n, d), F32),
        compiler_params=_params(1),
        name="attn_post",
    )(x, o, w_o)


def _ffn_ple_kernel(x_ref, xp_ref, xn_ref, p_ref, gf_ref, wup_ref, cw_ref, cb_ref, wdn_ref,
                    gp_ref, wg_ref, wp_ref, gfin_ref, y_ref, a_s, act_s, *, tiles_per_seq, fc, final):
    tm, d = x_ref.shape
    f = wdn_ref.shape[0]
    i = pl.program_id(0)
    keep_prev = jnp.where(i % tiles_per_seq == 0, 0.0, 1.0).astype(F32)
    keep_next = jnp.where(i % tiles_per_seq == tiles_per_seq - 1, 0.0, 1.0).astype(F32)
    x = x_ref[...]
    gf = gf_ref[...]
    h = jnp.concatenate([_rms(xp_ref[...], gf) * keep_prev, _rms(x, gf), _rms(xn_ref[...], gf) * keep_next], axis=0)
    a_s[...] = jnp.dot(h.astype(BF16), wup_ref[...], preferred_element_type=F32)

    def conv(col0):
        cols = pl.ds(col0, fc)
        c = cb_ref[:, cols]
        for t in range(CONV_WIDTH):
            c = c + a_s[pl.ds(HALO - 1 + t, tm), cols] * cw_ref[pl.ds(t, 1), cols]
        return c

    for j in range(f // fc):
        val = conv(j * fc)
        gate = conv(f + j * fc)
        act_s[:, pl.ds(j * fc, fc)] = (gate * jax.nn.sigmoid(gate) * val).astype(BF16)

    x = x + jnp.dot(act_s[...], wdn_ref[...], preferred_element_type=F32)
    gate = jax.nn.sigmoid(jnp.dot(_rms(x, gp_ref[...]).astype(BF16), wg_ref[...], preferred_element_type=F32))
    x = x + gate * jnp.dot(p_ref[...].astype(BF16), wp_ref[...], preferred_element_type=F32)
    if final:
        x = _rms(x, gfin_ref[...])
    y_ref[...] = x


def _ffn_ple(x, p, g_ffn, w_up, conv_w, conv_b, w_down, g_ple, w_gate, w_proj, g_final, seq, tm, final):
    n, d = x.shape
    f = w_down.shape[0]
    pd = p.shape[1]
    fc = 2 * LANES
    tiles_per_seq = seq // tm
    hb = tm // HALO
    last_blk = n // HALO - 1
    return pl.pallas_call(
        functools.partial(_ffn_ple_kernel, tiles_per_seq=tiles_per_seq, fc=fc, final=final),
        grid=(n // tm,),
        in_specs=[pl.BlockSpec((tm, d), lambda i: (i, 0)),
                  pl.BlockSpec((HALO, d), lambda i: (jnp.maximum(i * hb - 1, 0), 0)),
                  pl.BlockSpec((HALO, d), lambda i: (jnp.minimum((i + 1) * hb, last_blk), 0)),
                  pl.BlockSpec((tm, pd), lambda i: (i, 0)),
                  _const_spec((1, d)), _const_spec((d, 2 * f)), _const_spec((CONV_WIDTH, 2 * f)),
                  _const_spec((1, 2 * f)), _const_spec((f, d)), _const_spec((1, d)), _const_spec((d, d)),
                  _const_spec((pd, d)), _const_spec((1, d))],
        out_specs=pl.BlockSpec((tm, d), lambda i: (i, 0)),
        out_shape=jax.ShapeDtypeStruct((n, d), F32),
        scratch_shapes=[pltpu.VMEM((tm + 2 * HALO, 2 * f), F32), pltpu.VMEM((tm, f), BF16)],
        compiler_params=_params(1),
        name="ffn_ple_final" if final else "ffn_ple",
    )(x, x, x, p, g_ffn, w_up, conv_w, conv_b, w_down, g_ple, w_gate, w_proj, g_final)


def _sgu_kernel(x_ref, g_ref, wuv_ref, lng_ref, lnb_ref, ws_ref, bs_ref, wout_ref, y_ref, um_s):
    tm, d = x_ref.shape
    w = wout_ref.shape[0]
    x = x_ref[...]
    z = jnp.dot(_rms(x, g_ref[...]).astype(BF16), wuv_ref[...], preferred_element_type=F32)
    z = 0.5 * z * (1.0 + lax.erf(z * (1.0 / math.sqrt(2.0))))
    v = z[:, w:]
    vc = v - jnp.mean(v, axis=-1, keepdims=True)
    v = vc * lax.rsqrt(jnp.mean(vc * vc, axis=-1, keepdims=True) + EPS) * lng_ref[...] + lnb_ref[...]
    vb = v.astype(BF16)
    for c in range(tm // CHUNK):
        rows = slice(c * CHUNK, (c + 1) * CHUNK)
        for grp in range(w // CHUNK):
            cols = slice(grp * CHUNK, (grp + 1) * CHUNK)
            mixed = jnp.dot(ws_ref[grp], vb[rows, cols], preferred_element_type=F32) + bs_ref[grp]
            um_s[rows, cols] = (z[rows, cols] * mixed).astype(BF16)
    y_ref[...] = x + jnp.dot(um_s[...], wout_ref[...], preferred_element_type=F32)


def _sgu(x, g, w_uv, ln_g, ln_b, w_s, b_s, w_out, tm):
    n, d = x.shape
    w = w_out.shape[0]
    ng = w // CHUNK
    row = pl.BlockSpec((tm, d), lambda i: (i, 0))
    return pl.pallas_call(
        _sgu_kernel,
        grid=(n // tm,),
        in_specs=[row, _const_spec((1, d)), _const_spec((d, 2 * w)), _const_spec((1, w)), _const_spec((1, w)),
                  _const_spec((ng, CHUNK, CHUNK)), _const_spec((ng, CHUNK, 1)), _const_spec((w, d))],
        out_specs=row,
        out_shape=jax.ShapeDtypeStruct((n, d), F32),
        scratch_shapes=[pltpu.VMEM((tm, w), BF16)],
        compiler_params=_params(1),
        name="sgu",
    )(x, g, w_uv, ln_g, ln_b, w_s, b_s, w_out)


def _rope_tables(seq):
    inv = 1.0 / (ROPE_THETA ** (jnp.arange(0, HEAD_DIM, 2, dtype=F32) / HEAD_DIM))
    ang = jnp.arange(seq, dtype=F32)[:, None] * inv[None, :]
    ang = jnp.concatenate([ang, ang, ang, ang], axis=-1)
    half = HEAD_DIM // 2
    sign = jnp.where((jnp.arange(LANES) % HEAD_DIM) < half, -1.0, 1.0).astype(F32)
    cos, sin = jnp.cos(ang), jnp.sin(ang) * sign
    q_scale = (HEAD_DIM ** -0.5) * math.log2(math.e)
    return cos * q_scale, sin * q_scale, cos, sin


def _tile(n, want):
    t = min(n, want)
    assert n % t == 0, (n, t)
    return t


def _trunk(x, p, w):
    b, seq, d = x.shape
    n = b * seq
    x = x.reshape(n, d)
    tm = _tile(seq, 512)
    q, k, v = _attn_pre(x, w["norm_mix_g"][0], w["attn_w_qkv"], _rope_tables(seq), seq, tm)
    lambda_init = 0.8 - 0.6 * math.exp(-0.3 * 0)
    o = _flash(q.reshape(b, seq, d), k.reshape(b, seq, d), v.reshape(b, seq, d),
               w["attn_lq1"], w["attn_lk1"], w["attn_lq2"], w["attn_lk2"], w["attn_subln_g"],
               lambda_init, _tile(seq, 256), _tile(seq, 512))
    x = _attn_post(x, o.reshape(n, d), w["attn_w_o"], tm)
    x = _ffn_ple(x, p[0].reshape(n, -1), w["norm_ffn_g"][0], w["ffn_w_up"][0], w["ffn_conv_w"][0],
                 w["ffn_conv_b"][0], w["ffn_w_down"][0], w["norm_ple_g"][0], w["ple_w_gate"][0],
                 w["ple_w_proj"][0], w["final_norm_g"], seq, tm, final=False)
    x = _sgu(x, w["norm_mix_g"][1], w["sgu_w_uv"], w["sgu_ln_g"], w["sgu_ln_b"], w["sgu_w_s"], w["sgu_b_s"],
             w["sgu_w_out"], tm)
    x = _ffn_ple(x, p[1].reshape(n, -1), w["norm_ffn_g"][1], w["ffn_w_up"][1], w["ffn_conv_w"][1],
                 w["ffn_conv_b"][1], w["ffn_w_down"][1], w["norm_ple_g"][1], w["ple_w_gate"][1],
                 w["ple_w_proj"][1], w["final_norm_g"], seq, tm, final=True)
    return x.reshape(b, seq, d)


def kernel(x_prompt, x_sample, p_prompt, p_sample, norm_mix_g, attn_w_qkv, attn_lq1, attn_lk1, attn_lq2, attn_lk2, attn_subln_g, attn_w_o, sgu_w_uv, sgu_ln_g, sgu_ln_b, sgu_w_s, sgu_b_s, sgu_w_out, norm_ffn_g, ffn_w_up, ffn_conv_w, ffn_conv_b, ffn_w_down, norm_ple_g, ple_w_gate, ple_w_proj, final_norm_g):
    depth = norm_mix_g.shape[0]
    assert depth == 2 and attn_w_qkv.shape[0] == 1 and sgu_w_uv.shape[0] == 1
    d = x_prompt.shape[-1]
    row = lambda a: a.reshape(a.shape[:-1] + (1, a.shape[-1]))
    w = dict(
        norm_mix_g=row(norm_mix_g), norm_ffn_g=row(norm_ffn_g), norm_ple_g=row(norm_ple_g),
        final_norm_g=final_norm_g.reshape(1, d),
        attn_w_qkv=attn_w_qkv[0].astype(BF16), attn_w_o=attn_w_o[0].astype(BF16),
        attn_lq1=attn_lq1, attn_lk1=attn_lk1, attn_lq2=attn_lq2, attn_lk2=attn_lk2, attn_subln_g=attn_subln_g,
        sgu_w_uv=sgu_w_uv[0].astype(BF16), sgu_ln_g=sgu_ln_g, sgu_ln_b=sgu_ln_b,
        sgu_w_s=sgu_w_s[0].astype(BF16), sgu_b_s=sgu_b_s[0][:, :, None], sgu_w_out=sgu_w_out[0].astype(BF16),
        ffn_w_up=ffn_w_up.astype(BF16), ffn_conv_w=ffn_conv_w, ffn_conv_b=row(ffn_conv_b),
        ffn_w_down=ffn_w_down.astype(BF16), ple_w_gate=ple_w_gate.astype(BF16), ple_w_proj=ple_w_proj.astype(BF16),
    )
    return _trunk(x_prompt, p_prompt, w), _trunk(x_sample, p_sample, w)
```

```python
import functools
import math

import jax
import jax.numpy as jnp
from jax import lax
from jax.experimental import pallas as pl
from jax.experimental.pallas import tpu as pltpu

F32 = jnp.float32
BF16 = jnp.bfloat16

EPS = 1e-6
ROPE_THETA = 10000.0
HEAD_DIM = 64
CHUNK = 128
CONV_WIDTH = 3
HALO = 8
LANES = 128
ONES_ROWS = 16

VMEM_LIMIT = 56 * 1024 * 1024


def _rms(x, g):
    return x * lax.rsqrt(jnp.mean(x * x, axis=-1, keepdims=True) + EPS) * g


def _const_spec(shape):
    nd = len(shape)
    return pl.BlockSpec(shape, lambda *_: (0,) * nd, pipeline_mode=pl.Buffered(1))


def _params(n_axes):
    return pltpu.CompilerParams(dimension_semantics=("arbitrary",) * n_axes,
                                vmem_limit_bytes=VMEM_LIMIT)


def _attn_pre_kernel(x_ref, g_ref, w_ref, cq_ref, sq_ref, ck_ref, sk_ref, qt_ref, k_ref, vt_ref):
    tm, d = x_ref.shape
    hd = 2 * HEAD_DIM
    h = _rms(x_ref[...], g_ref[...]).astype(BF16)
    qkv = jnp.dot(h, w_ref[...], preferred_element_type=F32)
    lane = lax.broadcasted_iota(jnp.int32, (tm, LANES), 1)
    first_half = (lane % HEAD_DIM) < (HEAD_DIM // 2)

    def rope(t, cos, sin):
        rot = jnp.where(first_half, pltpu.roll(t, LANES - HEAD_DIM // 2, 1), pltpu.roll(t, HEAD_DIM // 2, 1))
        return t * cos + rot * sin

    cq, sq, ck, sk = cq_ref[...], sq_ref[...], ck_ref[...], sk_ref[...]
    for j in range(d // hd):
        cols = slice(j * hd, (j + 1) * hd)
        qt_ref[j] = rope(qkv[:, j * hd:(j + 1) * hd], cq, sq).T.astype(BF16)
        k_ref[:, cols] = rope(qkv[:, d + j * hd:d + (j + 1) * hd], ck, sk).astype(BF16)
        vt_ref[j, 0:hd, :] = qkv[:, 2 * d + j * hd:2 * d + (j + 1) * hd].T.astype(BF16)
        vt_ref[j, hd:hd + ONES_ROWS, :] = jnp.ones((ONES_ROWS, tm), BF16)


def _attn_pre(x, g, w_qkv, tables, seq, tm):
    n, d = x.shape
    hd = 2 * HEAD_DIM
    heads = d // hd
    tps = seq // tm
    tab_spec = pl.BlockSpec((tm, LANES), lambda i: (i % tps, 0))
    return pl.pallas_call(
        _attn_pre_kernel,
        grid=(n // tm,),
        in_specs=[pl.BlockSpec((tm, d), lambda i: (i, 0)), _const_spec((1, d)), _const_spec((d, 3 * d)),
                  tab_spec, tab_spec, tab_spec, tab_spec],
        out_specs=[pl.BlockSpec((heads, hd, tm), lambda i: (0, 0, i)),
                   pl.BlockSpec((tm, d), lambda i: (i, 0)),
                   pl.BlockSpec((heads, hd + ONES_ROWS, tm), lambda i: (0, 0, i))],
        out_shape=[jax.ShapeDtypeStruct((heads, hd, n), BF16), jax.ShapeDtypeStruct((n, d), BF16),
                   jax.ShapeDtypeStruct((heads, hd + ONES_ROWS, n), BF16)],
        compiler_params=_params(1),
        name="attn_pre",
    )(x, g, w_qkv, *tables)


def _flash_kernel(qt_ref, k_ref, vt_ref, lq1_ref, lk1_ref, lq2_ref, lk2_ref, g_ref, o_ref,
                  q2t_s, m_s, acc_s, s0_s, s1_s, p0_s, p1_s, a0_s, a1_s, *, tk, lambda_init):
    hd, tq = qt_ref.shape[1], qt_ref.shape[2]
    seq = k_ref.shape[1]
    qt = qt_ref[0]
    row = lax.broadcasted_iota(jnp.int32, qt.shape, 0)
    zero = jnp.zeros_like(qt)
    q2t_s[:, 0:tq] = jnp.where(row < HEAD_DIM, qt, zero)
    q2t_s[:, tq:2 * tq] = jnp.where(row >= HEAD_DIM, qt, zero)
    m_s[...] = jnp.full(m_s.shape, -jnp.inf, F32)
    acc_s[...] = jnp.zeros(acc_s.shape, F32)
    s_bufs, p_bufs, a_bufs = (s0_s, s1_s), (p0_s, p1_s), (a0_s, a1_s)
    n = seq // tk

    def scores(t, slot):
        start = pl.multiple_of(t * tk, tk)
        s_bufs[slot][...] = jnp.dot(k_ref[0, pl.ds(start, tk), :], q2t_s[...], preferred_element_type=F32)

    def softmax(slot):
        s = s_bufs[slot][...]
        m_old = m_s[...]
        m_new = jnp.maximum(m_old, jnp.max(s, axis=0, keepdims=True))
        a_bufs[slot][...] = jnp.exp2(m_old - m_new)
        p_bufs[slot][...] = jnp.exp2(s - m_new).astype(BF16)
        m_s[...] = m_new

    def values(t, slot):
        start = pl.multiple_of(t * tk, tk)
        pv = jnp.dot(vt_ref[0, :, pl.ds(start, tk)], p_bufs[slot][...], preferred_element_type=F32)
        acc_s[...] = a_bufs[slot][...] * acc_s[...] + pv

    def step(t, slot, with_scores=True, with_softmax=True):
        values(t, slot)
        if with_scores:
            scores(t + 2, slot)
        if with_softmax:
            softmax(1 - slot)

    scores(0, 0)
    scores(1, 1)
    softmax(0)

    def body(i, carry):
        step(2 * i, 0)
        step(2 * i + 1, 1)
        return carry

    lax.fori_loop(0, (n - 2) // 2, body, 0)
    step(n - 2, 0, with_scores=False)
    step(n - 1, 1, with_scores=False, with_softmax=False)

    acc = acc_s[...]
    o_maps = acc[0:hd] / acc[hd:hd + 1]
    lam = (jnp.exp(jnp.sum(lq1_ref[...] * lk1_ref[...], axis=-1, keepdims=True))
           - jnp.exp(jnp.sum(lq2_ref[...] * lk2_ref[...], axis=-1, keepdims=True)) + lambda_init)
    o = (o_maps[:, 0:tq] - lam * o_maps[:, tq:2 * tq]).T
    o_ref[0] = (_rms(o, g_ref[...]) * (1.0 - lambda_init)).astype(o_ref.dtype)


def _flash(qt, k, vt, lq1, lk1, lq2, lk2, subln_g, lambda_init, tq, tk):
    b, seq, d = k.shape
    heads, hd, _ = qt.shape
    vrows = vt.shape[1]
    qtiles = seq // tq
    n_chunks = seq // tk
    assert n_chunks >= 2 and n_chunks % 2 == 0, (seq, tk)
    small = _const_spec((1, HEAD_DIM))
    return pl.pallas_call(
        functools.partial(_flash_kernel, tk=tk, lambda_init=lambda_init),
        grid=(b, heads, qtiles),
        in_specs=[pl.BlockSpec((1, hd, tq), lambda bi, h, qi: (h, 0, bi * qtiles + qi)),
                  pl.BlockSpec((1, seq, hd), lambda bi, h, qi: (bi, 0, h)),
                  pl.BlockSpec((1, vrows, seq), lambda bi, h, qi: (h, 0, bi)),
                  small, small, small, small, _const_spec((1, hd))],
        out_specs=pl.BlockSpec((1, tq, hd), lambda bi, h, qi: (bi, qi, h)),
        out_shape=jax.ShapeDtypeStruct((b, seq, d), BF16),
        scratch_shapes=[pltpu.VMEM((hd, 2 * tq), BF16), pltpu.VMEM((1, 2 * tq), F32),
                        pltpu.VMEM((vrows, 2 * tq), F32),
                        pltpu.VMEM((tk, 2 * tq), F32), pltpu.VMEM((tk, 2 * tq), F32),
                        pltpu.VMEM((tk, 2 * tq), BF16), pltpu.VMEM((tk, 2 * tq), BF16),
                        pltpu.VMEM((1, 2 * tq), F32), pltpu.VMEM((1, 2 * tq), F32)],
        compiler_params=_params(3),
        name="flash_diff_attn",
    )(qt, k, vt, lq1, lk1, lq2, lk2, subln_g)


def _attn_post_kernel(x_ref, o_ref, w_ref, y_ref):
    y_ref[...] = x_ref[...] + jnp.dot(o_ref[...], w_ref[...], preferred_element_type=F32)


def _attn_post(x, o, w_o, tm):
    n, d = x.shape
    row = pl.BlockSpec((tm, d), lambda i: (i, 0))
    return pl.pallas_call(
        _attn_post_kernel,
        grid=(n // tm,),
        in_specs=[row, row, _const_spec((d, d))],
        out_specs=row,
        out_shape=jax.ShapeDtypeStruct((n, d), F32),
        compiler_params=_params(1),
        name="attn_post",
    )(x, o, w_o)


def _ffn_ple_kernel(x_ref, xp_ref, xn_ref, p_ref, gf_ref, wup_ref, cw_ref, cb_ref, wdn_ref,
                    gp_ref, wg_ref, wp_ref, gfin_ref, y_ref, a_s, act_s, *, tiles_per_seq, fc, final):
    tm, d = x_ref.shape
    f = wdn_ref.shape[0]
    i = pl.program_id(0)
    keep_prev = jnp.where(i % tiles_per_seq == 0, 0.0, 1.0).astype(F32)
    keep_next = jnp.where(i % tiles_per_seq == tiles_per_seq - 1, 0.0, 1.0).astype(F32)
    x = x_ref[...]
    gf = gf_ref[...]
    h = jnp.concatenate([_rms(xp_ref[...], gf) * keep_prev, _rms(x, gf), _rms(xn_ref[...], gf) * keep_next], axis=0)
    a_s[...] = jnp.dot(h.astype(BF16), wup_ref[...], preferred_element_type=F32)

    def conv(col0):
        cols = pl.ds(col0, fc)
        c = cb_ref[:, cols]
        for t in range(CONV_WIDTH):
            c = c + a_s[pl.ds(HALO - 1 + t, tm), cols] * cw_ref[pl.ds(t, 1), cols]
        return c

    for j in range(f // fc):
        val = conv(j * fc)
        gate = conv(f + j * fc)
        act_s[:, pl.ds(j * fc, fc)] = (gate * jax.nn.sigmoid(gate) * val).astype(BF16)

    x = x + jnp.dot(act_s[...], wdn_ref[...], preferred_element_type=F32)
    gate = jax.nn.sigmoid(jnp.dot(_rms(x, gp_ref[...]).astype(BF16), wg_ref[...], preferred_element_type=F32))
    x = x + gate * jnp.dot(p_ref[...].astype(BF16), wp_ref[...], preferred_element_type=F32)
    if final:
        x = _rms(x, gfin_ref[...])
    y_ref[...] = x


def _ffn_ple(x, p, g_ffn, w_up, conv_w, conv_b, w_down, g_ple, w_gate, w_proj, g_final, seq, tm, final):
    n, d = x.shape
    f = w_down.shape[0]
    pd = p.shape[1]
    fc = 2 * LANES
    tiles_per_seq = seq // tm
    hb = tm // HALO
    last_blk = n // HALO - 1
    return pl.pallas_call(
        functools.partial(_ffn_ple_kernel, tiles_per_seq=tiles_per_seq, fc=fc, final=final),
        grid=(n // tm,),
        in_specs=[pl.BlockSpec((tm, d), lambda i: (i, 0)),
                  pl.BlockSpec((HALO, d), lambda i: (jnp.maximum(i * hb - 1, 0), 0)),
                  pl.BlockSpec((HALO, d), lambda i: (jnp.minimum((i + 1) * hb, last_blk), 0)),
                  pl.BlockSpec((tm, pd), lambda i: (i, 0)),
                  _const_spec((1, d)), _const_spec((d, 2 * f)), _const_spec((CONV_WIDTH, 2 * f)),
                  _const_spec((1, 2 * f)), _const_spec((f, d)), _const_spec((1, d)), _const_spec((d, d)),
                  _const_spec((pd, d)), _const_spec((1, d))],
        out_specs=pl.BlockSpec((tm, d), lambda i: (i, 0)),
        out_shape=jax.ShapeDtypeStruct((n, d), F32),
        scratch_shapes=[pltpu.VMEM((tm + 2 * HALO, 2 * f), F32), pltpu.VMEM((tm, f), BF16)],
        compiler_params=_params(1),
        name="ffn_ple_final" if final else "ffn_ple",
    )(x, x, x, p, g_ffn, w_up, conv_w, conv_b, w_down, g_ple, w_gate, w_proj, g_final)


def _sgu_kernel(x_ref, g_ref, wuv_ref, lng_ref, lnb_ref, ws_ref, bs_ref, wout_ref, y_ref, um_s):
    tm, d = x_ref.shape
    w = wout_ref.shape[0]
    x = x_ref[...]
    z = jnp.dot(_rms(x, g_ref[...]).astype(BF16), wuv_ref[...], preferred_element_type=F32)
    z = 0.5 * z * (1.0 + lax.erf(z * (1.0 / math.sqrt(2.0))))
    v = z[:, w:]
    vc = v - jnp.mean(v, axis=-1, keepdims=True)
    v = vc * lax.rsqrt(jnp.mean(vc * vc, axis=-1, keepdims=True) + EPS) * lng_ref[...] + lnb_ref[...]
    vb = v.astype(BF16)
    for c in range(tm // CHUNK):
        rows = slice(c * CHUNK, (c + 1) * CHUNK)
        for grp in range(w // CHUNK):
            cols = slice(grp * CHUNK, (grp + 1) * CHUNK)
            mixed = jnp.dot(ws_ref[grp], vb[rows, cols], preferred_element_type=F32) + bs_ref[grp]
            um_s[rows, cols] = (z[rows, cols] * mixed).astype(BF16)
    y_ref[...] = x + jnp.dot(um_s[...], wout_ref[...], preferred_element_type=F32)


def _sgu(x, g, w_uv, ln_g, ln_b, w_s, b_s, w_out, tm):
    n, d = x.shape
    w = w_out.shape[0]
    ng = w // CHUNK
    row = pl.BlockSpec((tm, d), lambda i: (i, 0))
    return pl.pallas_call(
        _sgu_kernel,
        grid=(n // tm,),
        in_specs=[row, _const_spec((1, d)), _const_spec((d, 2 * w)), _const_spec((1, w)), _const_spec((1, w)),
                  _const_spec((ng, CHUNK, CHUNK)), _const_spec((ng, CHUNK, 1)), _const_spec((w, d))],
        out_specs=row,
        out_shape=jax.ShapeDtypeStruct((n, d), F32),
        scratch_shapes=[pltpu.VMEM((tm, w), BF16)],
        compiler_params=_params(1),
        name="sgu",
    )(x, g, w_uv, ln_g, ln_b, w_s, b_s, w_out)


def _rope_tables(seq):
    inv = 1.0 / (ROPE_THETA ** (jnp.arange(0, HEAD_DIM, 2, dtype=F32) / HEAD_DIM))
    ang = jnp.arange(seq, dtype=F32)[:, None] * inv[None, :]
    ang = jnp.concatenate([ang, ang, ang, ang], axis=-1)
    half = HEAD_DIM // 2
    sign = jnp.where((jnp.arange(LANES) % HEAD_DIM) < half, -1.0, 1.0).astype(F32)
    cos, sin = jnp.cos(ang), jnp.sin(ang) * sign
    q_scale = (HEAD_DIM ** -0.5) * math.log2(math.e)
    return cos * q_scale, sin * q_scale, cos, sin


def _tile(n, want):
    t = min(n, want)
    assert n % t == 0, (n, t)
    return t


def _trunk(x, p, w):
    b, seq, d = x.shape
    n = b * seq
    x = x.reshape(n, d)
    tm = _tile(seq, 512)
    qt, k, vt = _attn_pre(x, w["norm_mix_g"][0], w["attn_w_qkv"], _rope_tables(seq), seq, tm)
    lambda_init = 0.8 - 0.6 * math.exp(-0.3 * 0)
    o = _flash(qt, k.reshape(b, seq, d), vt,
               w["attn_lq1"], w["attn_lk1"], w["attn_lq2"], w["attn_lk2"], w["attn_subln_g"],
               lambda_init, _tile(seq, 256), _tile(seq // 2, 512))
    x = _attn_post(x, o.reshape(n, d), w["attn_w_o"], tm)
    x = _ffn_ple(x, p[0].reshape(n, -1), w["norm_ffn_g"][0], w["ffn_w_up"][0], w["ffn_conv_w"][0],
                 w["ffn_conv_b"][0], w["ffn_w_down"][0], w["norm_ple_g"][0], w["ple_w_gate"][0],
                 w["ple_w_proj"][0], w["final_norm_g"], seq, tm, final=False)
    x = _sgu(x, w["norm_mix_g"][1], w["sgu_w_uv"], w["sgu_ln_g"], w["sgu_ln_b"], w["sgu_w_s"], w["sgu_b_s"],
             w["sgu_w_out"], tm)
    x = _ffn_ple(x, p[1].reshape(n, -1), w["norm_ffn_g"][1], w["ffn_w_up"][1], w["ffn_conv_w"][1],
                 w["ffn_conv_b"][1], w["ffn_w_down"][1], w["norm_ple_g"][1], w["ple_w_gate"][1],
                 w["ple_w_proj"][1], w["final_norm_g"], seq, tm, final=True)
    return x.reshape(b, seq, d)


def kernel(x_prompt, x_sample, p_prompt, p_sample, norm_mix_g, attn_w_qkv, attn_lq1, attn_lk1, attn_lq2, attn_lk2, attn_subln_g, attn_w_o, sgu_w_uv, sgu_ln_g, sgu_ln_b, sgu_w_s, sgu_b_s, sgu_w_out, norm_ffn_g, ffn_w_up, ffn_conv_w, ffn_conv_b, ffn_w_down, norm_ple_g, ple_w_gate, ple_w_proj, final_norm_g):
    depth = norm_mix_g.shape[0]
    assert depth == 2 and attn_w_qkv.shape[0] == 1 and sgu_w_uv.shape[0] == 1
    d = x_prompt.shape[-1]
    row = lambda a: a.reshape(a.shape[:-1] + (1, a.shape[-1]))
    w = dict(
        norm_mix_g=row(norm_mix_g), norm_ffn_g=row(norm_ffn_g), norm_ple_g=row(norm_ple_g),
        final_norm_g=final_norm_g.reshape(1, d),
        attn_w_qkv=attn_w_qkv[0].astype(BF16), attn_w_o=attn_w_o[0].astype(BF16),
        attn_lq1=attn_lq1, attn_lk1=attn_lk1, attn_lq2=attn_lq2, attn_lk2=attn_lk2, attn_subln_g=attn_subln_g,
        sgu_w_uv=sgu_w_uv[0].astype(BF16), sgu_ln_g=sgu_ln_g, sgu_ln_b=sgu_ln_b,
        sgu_w_s=sgu_w_s[0].astype(BF16), sgu_b_s=sgu_b_s[0][:, :, None], sgu_w_out=sgu_w_out[0].astype(BF16),
        ffn_w_up=ffn_w_up.astype(BF16), ffn_conv_w=ffn_conv_w, ffn_conv_b=row(ffn_conv_b),
        ffn_w_down=ffn_w_down.astype(BF16), ple_w_gate=ple_w_gate.astype(BF16), ple_w_proj=ple_w_proj.astype(BF16),
    )
    return _trunk(x_prompt, p_prompt, w), _trunk(x_sample, p_sample, w)
```

```python
import functools
import math

import jax
import jax.numpy as jnp
from jax import lax
from jax.experimental import pallas as pl
from jax.experimental.pallas import tpu as pltpu

F32 = jnp.float32
BF16 = jnp.bfloat16

EPS = 1e-6
ROPE_THETA = 10000.0
HEAD_DIM = 64
CHUNK = 128
CONV_WIDTH = 3
HALO = 8
LANES = 128
MXU_COLS = 256
ONES_ROWS = 16

VMEM_LIMIT = 56 * 1024 * 1024


def _rms(x, g):
    return x * lax.rsqrt(jnp.mean(x * x, axis=-1, keepdims=True) + EPS) * g


def _const_spec(shape):
    nd = len(shape)
    return pl.BlockSpec(shape, lambda *_: (0,) * nd, pipeline_mode=pl.Buffered(1))


def _params(n_axes, flags=None):
    return pltpu.CompilerParams(dimension_semantics=("arbitrary",) * n_axes,
                                vmem_limit_bytes=VMEM_LIMIT, flags=flags)


def _attn_pre_kernel(x_ref, g_ref, w_ref, cq_ref, sq_ref, ck_ref, sk_ref, qt_ref, k_ref, vt_ref):
    tm, d = x_ref.shape
    hd = 2 * HEAD_DIM
    h = _rms(x_ref[...], g_ref[...]).astype(BF16)
    qkv = jnp.dot(h, w_ref[...], preferred_element_type=F32)
    lane = lax.broadcasted_iota(jnp.int32, (tm, LANES), 1)
    first_half = (lane % HEAD_DIM) < (HEAD_DIM // 2)

    def rope(t, cos, sin):
        rot = jnp.where(first_half, pltpu.roll(t, LANES - HEAD_DIM // 2, 1), pltpu.roll(t, HEAD_DIM // 2, 1))
        return t * cos + rot * sin

    cq, sq, ck, sk = cq_ref[...], sq_ref[...], ck_ref[...], sk_ref[...]
    for j in range(d // hd):
        cols = slice(j * hd, (j + 1) * hd)
        qt_ref[j] = rope(qkv[:, j * hd:(j + 1) * hd], cq, sq).T.astype(BF16)
        k_ref[:, cols] = rope(qkv[:, d + j * hd:d + (j + 1) * hd], ck, sk).astype(BF16)
        vt_ref[j, 0:hd, :] = qkv[:, 2 * d + j * hd:2 * d + (j + 1) * hd].T.astype(BF16)
        vt_ref[j, hd:hd + ONES_ROWS, :] = jnp.ones((ONES_ROWS, tm), BF16)


def _attn_pre(x, g, w_qkv, tables, seq, tm):
    n, d = x.shape
    hd = 2 * HEAD_DIM
    heads = d // hd
    tps = seq // tm
    tab_spec = pl.BlockSpec((tm, LANES), lambda i: (i % tps, 0))
    return pl.pallas_call(
        _attn_pre_kernel,
        grid=(n // tm,),
        in_specs=[pl.BlockSpec((tm, d), lambda i: (i, 0)), _const_spec((1, d)), _const_spec((d, 3 * d)),
                  tab_spec, tab_spec, tab_spec, tab_spec],
        out_specs=[pl.BlockSpec((heads, hd, tm), lambda i: (0, 0, i)),
                   pl.BlockSpec((tm, d), lambda i: (i, 0)),
                   pl.BlockSpec((heads, hd + ONES_ROWS, tm), lambda i: (0, 0, i))],
        out_shape=[jax.ShapeDtypeStruct((heads, hd, n), BF16), jax.ShapeDtypeStruct((n, d), BF16),
                   jax.ShapeDtypeStruct((heads, hd + ONES_ROWS, n), BF16)],
        compiler_params=_params(1),
        name="attn_pre",
    )(x, g, w_qkv, *tables)


def _flash_kernel(qt_ref, k_ref, vt_ref, lq1_ref, lk1_ref, lq2_ref, lk2_ref, g_ref, o_ref,
                  q2t_s, m_s, acc_s, *bufs, tk, group, lambda_init):
    hd, tq = qt_ref.shape[1], qt_ref.shape[2]
    seq = k_ref.shape[1]
    qt = qt_ref[0]
    row = lax.broadcasted_iota(jnp.int32, qt.shape, 0)
    zero = jnp.zeros_like(qt)
    q2t_s[:, 0:tq] = jnp.where(row < HEAD_DIM, qt, zero)
    q2t_s[:, tq:2 * tq] = jnp.where(row >= HEAD_DIM, qt, zero)
    m_s[...] = jnp.full(m_s.shape, -jnp.inf, F32)
    acc_s[...] = jnp.zeros(acc_s.shape, F32)
    slots = 2 * group
    s_bufs, p_bufs, a_bufs, x_bufs = (bufs[i * slots:(i + 1) * slots] for i in range(4))
    n_groups = seq // (tk * group)

    def work(values=None, softmax_half=None, scores=None):
        for g in range(group):
            if softmax_half is not None:
                slot_m = softmax_half * group + g
                m_old = m_s[...]
                m_new = jnp.maximum(m_old, x_bufs[slot_m][...])
                a_bufs[slot_m][...] = jnp.exp2(m_old - m_new)
                m_s[...] = m_new
            if values is not None:
                slot_v = values[1] * group + g
                start_v = pl.multiple_of((values[0] * group + g) * tk, tk)
            if scores is not None:
                slot_s = scores[1] * group + g
                start_s = pl.multiple_of((scores[0] * group + g) * tk, tk)
            for c in range(2 * tq // MXU_COLS):
                cols = slice(c * MXU_COLS, (c + 1) * MXU_COLS)
                if values is not None:
                    pv = jnp.dot(vt_ref[0, :, pl.ds(start_v, tk)], p_bufs[slot_v][:, cols],
                                 preferred_element_type=F32)
                    acc_s[:, cols] = a_bufs[slot_v][:, cols] * acc_s[:, cols] + pv
                if softmax_half is not None:
                    for lc in range(c * MXU_COLS, (c + 1) * MXU_COLS, LANES):
                        lcols = slice(lc, lc + LANES)
                        p_bufs[slot_m][:, lcols] = jnp.exp2(s_bufs[slot_m][:, lcols] - m_new[:, lcols]).astype(BF16)
                if scores is not None:
                    s = jnp.dot(k_ref[0, pl.ds(start_s, tk), :], q2t_s[:, cols], preferred_element_type=F32)
                    s_bufs[slot_s][:, cols] = s
                    x_bufs[slot_s][:, cols] = jnp.max(s, axis=0, keepdims=True)

    def step(b, half):
        work(values=(b, half), softmax_half=1 - half, scores=(b + 2, half))

    work(scores=(0, 0))
    work(softmax_half=0, scores=(1, 1))
    steady = n_groups - 2
    for b in range(steady % 2):
        step(b, b % 2)

    def body(i, carry):
        for r in range(2):
            step(steady % 2 + 2 * i + r, (steady + r) % 2)
        return carry

    lax.fori_loop(0, steady // 2, body, 0)
    work(values=(n_groups - 2, n_groups % 2), softmax_half=(n_groups - 1) % 2)
    work(values=(n_groups - 1, (n_groups - 1) % 2))

    acc = acc_s[...]
    o_maps = acc[0:hd] / acc[hd:hd + 1]
    lam = (jnp.exp(jnp.sum(lq1_ref[...] * lk1_ref[...], axis=-1, keepdims=True))
           - jnp.exp(jnp.sum(lq2_ref[...] * lk2_ref[...], axis=-1, keepdims=True)) + lambda_init)
    o = (o_maps[:, 0:tq] - lam * o_maps[:, tq:2 * tq]).T
    o_ref[0] = (_rms(o, g_ref[...]) * (1.0 - lambda_init)).astype(o_ref.dtype)


def _flash(qt, k, vt, lq1, lk1, lq2, lk2, subln_g, lambda_init, tq, tk, group):
    b, seq, d = k.shape
    heads, hd, _ = qt.shape
    vrows = vt.shape[1]
    qtiles = seq // tq
    slots = 2 * group
    assert seq % (tk * group) == 0 and seq // (tk * group) >= 2, (seq, tk, group)
    small = _const_spec((1, HEAD_DIM))
    return pl.pallas_call(
        functools.partial(_flash_kernel, tk=tk, group=group, lambda_init=lambda_init),
        grid=(b, heads, qtiles),
        in_specs=[pl.BlockSpec((1, hd, tq), lambda bi, h, qi: (h, 0, bi * qtiles + qi)),
                  pl.BlockSpec((1, seq, hd), lambda bi, h, qi: (bi, 0, h)),
                  pl.BlockSpec((1, vrows, seq), lambda bi, h, qi: (h, 0, bi)),
                  small, small, small, small, _const_spec((1, hd))],
        out_specs=pl.BlockSpec((1, tq, hd), lambda bi, h, qi: (bi, qi, h)),
        out_shape=jax.ShapeDtypeStruct((b, seq, d), BF16),
        scratch_shapes=[pltpu.VMEM((hd, 2 * tq), BF16), pltpu.VMEM((1, 2 * tq), F32),
                        pltpu.VMEM((vrows, 2 * tq), F32)]
        + [pltpu.VMEM((tk, 2 * tq), F32)] * slots
        + [pltpu.VMEM((tk, 2 * tq), BF16)] * slots
        + [pltpu.VMEM((1, 2 * tq), F32)] * slots
        + [pltpu.VMEM((1, 2 * tq), F32)] * slots,
        compiler_params=_params(3),
        name="flash_diff_attn",
    )(qt, k, vt, lq1, lk1, lq2, lk2, subln_g)


def _attn_post_kernel(x_ref, o_ref, w_ref, y_ref):
    y_ref[...] = x_ref[...] + jnp.dot(o_ref[...], w_ref[...], preferred_element_type=F32)


def _attn_post(x, o, w_o, tm):
    n, d = x.shape
    row = pl.BlockSpec((tm, d), lambda i: (i, 0))
    return pl.pallas_call(
        _attn_post_kernel,
        grid=(n // tm,),
        in_specs=[row, row, _const_spec((d, d))],
        out_specs=row,
        out_shape=jax.ShapeDtypeStruct((n, d), F32),
        compiler_params=_params(1),
        name="attn_post",
    )(x, o, w_o)


def _ffn_ple_kernel(x_ref, xp_ref, xn_ref, p_ref, gf_ref, wup_ref, cw_ref, cb_ref, wdn_ref,
                    gp_ref, wg_ref, wp_ref, gfin_ref, y_ref, a_s, act_s, *, tiles_per_seq, fc, final):
    tm, d = x_ref.shape
    f = wdn_ref.shape[0]
    i = pl.program_id(0)
    keep_prev = jnp.where(i % tiles_per_seq == 0, 0.0, 1.0).astype(F32)
    keep_next = jnp.where(i % tiles_per_seq == tiles_per_seq - 1, 0.0, 1.0).astype(F32)
    x = x_ref[...]
    gf = gf_ref[...]
    h = jnp.concatenate([_rms(xp_ref[...], gf) * keep_prev, _rms(x, gf), _rms(xn_ref[...], gf) * keep_next], axis=0)
    a_s[...] = jnp.dot(h.astype(BF16), wup_ref[...], preferred_element_type=F32)

    def conv(col0):
        cols = pl.ds(col0, fc)
        c = cb_ref[:, cols]
        for t in range(CONV_WIDTH):
            c = c + a_s[pl.ds(HALO - 1 + t, tm), cols] * cw_ref[pl.ds(t, 1), cols]
        return c

    for j in range(f // fc):
        val = conv(j * fc)
        gate = conv(f + j * fc)
        act_s[:, pl.ds(j * fc, fc)] = (gate * jax.nn.sigmoid(gate) * val).astype(BF16)

    x = x + jnp.dot(act_s[...], wdn_ref[...], preferred_element_type=F32)
    gate = jax.nn.sigmoid(jnp.dot(_rms(x, gp_ref[...]).astype(BF16), wg_ref[...], preferred_element_type=F32))
    x = x + gate * jnp.dot(p_ref[...].astype(BF16), wp_ref[...], preferred_element_type=F32)
    if final:
        x = _rms(x, gfin_ref[...])
    y_ref[...] = x


def _ffn_ple(x, p, g_ffn, w_up, conv_w, conv_b, w_down, g_ple, w_gate, w_proj, g_final, seq, tm, final):
    n, d = x.shape
    f = w_down.shape[0]
    pd = p.shape[1]
    fc = 2 * LANES
    tiles_per_seq = seq // tm
    hb = tm // HALO
    last_blk = n // HALO - 1
    return pl.pallas_call(
        functools.partial(_ffn_ple_kernel, tiles_per_seq=tiles_per_seq, fc=fc, final=final),
        grid=(n // tm,),
        in_specs=[pl.BlockSpec((tm, d), lambda i: (i, 0)),
                  pl.BlockSpec((HALO, d), lambda i: (jnp.maximum(i * hb - 1, 0), 0)),
                  pl.BlockSpec((HALO, d), lambda i: (jnp.minimum((i + 1) * hb, last_blk), 0)),
                  pl.BlockSpec((tm, pd), lambda i: (i, 0)),
                  _const_spec((1, d)), _const_spec((d, 2 * f)), _const_spec((CONV_WIDTH, 2 * f)),
                  _const_spec((1, 2 * f)), _const_spec((f, d)), _const_spec((1, d)), _const_spec((d, d)),
                  _const_spec((pd, d)), _const_spec((1, d))],
        out_specs=pl.BlockSpec((tm, d), lambda i: (i, 0)),
        out_shape=jax.ShapeDtypeStruct((n, d), F32),
        scratch_shapes=[pltpu.VMEM((tm + 2 * HALO, 2 * f), F32), pltpu.VMEM((tm, f), BF16)],
        compiler_params=_params(1),
        name="ffn_ple_final" if final else "ffn_ple",
    )(x, x, x, p, g_ffn, w_up, conv_w, conv_b, w_down, g_ple, w_gate, w_proj, g_final)


def _sgu_kernel(x_ref, g_ref, wuv_ref, lng_ref, lnb_ref, ws_ref, bs_ref, wout_ref, y_ref, um_s):
    tm, d = x_ref.shape
    w = wout_ref.shape[0]
    x = x_ref[...]
    z = jnp.dot(_rms(x, g_ref[...]).astype(BF16), wuv_ref[...], preferred_element_type=F32)
    z = 0.5 * z * (1.0 + lax.erf(z * (1.0 / math.sqrt(2.0))))
    v = z[:, w:]
    vc = v - jnp.mean(v, axis=-1, keepdims=True)
    v = vc * lax.rsqrt(jnp.mean(vc * vc, axis=-1, keepdims=True) + EPS) * lng_ref[...] + lnb_ref[...]
    vb = v.astype(BF16)
    for c in range(tm // CHUNK):
        rows = slice(c * CHUNK, (c + 1) * CHUNK)
        for grp in range(w // CHUNK):
            cols = slice(grp * CHUNK, (grp + 1) * CHUNK)
            mixed = jnp.dot(ws_ref[grp], vb[rows, cols], preferred_element_type=F32) + bs_ref[grp]
            um_s[rows, cols] = (z[rows, cols] * mixed).astype(BF16)
    y_ref[...] = x + jnp.dot(um_s[...], wout_ref[...], preferred_element_type=F32)


def _sgu(x, g, w_uv, ln_g, ln_b, w_s, b_s, w_out, tm):
    n, d = x.shape
    w = w_out.shape[0]
    ng = w // CHUNK
    row = pl.BlockSpec((tm, d), lambda i: (i, 0))
    return pl.pallas_call(
        _sgu_kernel,
        grid=(n // tm,),
        in_specs=[row, _const_spec((1, d)), _const_spec((d, 2 * w)), _const_spec((1, w)), _const_spec((1, w)),
                  _const_spec((ng, CHUNK, CHUNK)), _const_spec((ng, CHUNK, 1)), _const_spec((w, d))],
        out_specs=row,
        out_shape=jax.ShapeDtypeStruct((n, d), F32),
        scratch_shapes=[pltpu.VMEM((tm, w), BF16)],
        compiler_params=_params(1),
        name="sgu",
    )(x, g, w_uv, ln_g, ln_b, w_s, b_s, w_out)


def _rope_tables(seq):
    inv = 1.0 / (ROPE_THETA ** (jnp.arange(0, HEAD_DIM, 2, dtype=F32) / HEAD_DIM))
    ang = jnp.arange(seq, dtype=F32)[:, None] * inv[None, :]
    ang = jnp.concatenate([ang, ang, ang, ang], axis=-1)
    half = HEAD_DIM // 2
    sign = jnp.where((jnp.arange(LANES) % HEAD_DIM) < half, -1.0, 1.0).astype(F32)
    cos, sin = jnp.cos(ang), jnp.sin(ang) * sign
    q_scale = (HEAD_DIM ** -0.5) * math.log2(math.e)
    return cos * q_scale, sin * q_scale, cos, sin


def _flash_tiles(seq):
    group = 2
    tq = _tile(seq, 512)
    tk = _tile(seq, max(LANES, min(512, seq // (4 * group))))
    return tq, tk, group


def _tile(n, want):
    t = min(n, want)
    assert n % t == 0, (n, t)
    return t


def _trunk(x, p, w):
    b, seq, d = x.shape
    n = b * seq
    x = x.reshape(n, d)
    tm = _tile(seq, 512)
    qt, k, vt = _attn_pre(x, w["norm_mix_g"][0], w["attn_w_qkv"], _rope_tables(seq), seq, tm)
    lambda_init = 0.8 - 0.6 * math.exp(-0.3 * 0)
    o = _flash(qt, k.reshape(b, seq, d), vt,
               w["attn_lq1"], w["attn_lk1"], w["attn_lq2"], w["attn_lk2"], w["attn_subln_g"],
               lambda_init, *_flash_tiles(seq))
    x = _attn_post(x, o.reshape(n, d), w["attn_w_o"], tm)
    x = _ffn_ple(x, p[0].reshape(n, -1), w["norm_ffn_g"][0], w["ffn_w_up"][0], w["ffn_conv_w"][0],
                 w["ffn_conv_b"][0], w["ffn_w_down"][0], w["norm_ple_g"][0], w["ple_w_gate"][0],
                 w["ple_w_proj"][0], w["final_norm_g"], seq, tm, final=False)
    x = _sgu(x, w["norm_mix_g"][1], w["sgu_w_uv"], w["sgu_ln_g"], w["sgu_ln_b"], w["sgu_w_s"], w["sgu_b_s"],
             w["sgu_w_out"], tm)
    x = _ffn_ple(x, p[1].reshape(n, -1), w["norm_ffn_g"][1], w["ffn_w_up"][1], w["ffn_conv_w"][1],
                 w["ffn_conv_b"][1], w["ffn_w_down"][1], w["norm_ple_g"][1], w["ple_w_gate"][1],
                 w["ple_w_proj"][1], w["final_norm_g"], seq, tm, final=True)
    return x.reshape(b, seq, d)


def kernel(x_prompt, x_sample, p_prompt, p_sample, norm_mix_g, attn_w_qkv, attn_lq1, attn_lk1, attn_lq2, attn_lk2, attn_subln_g, attn_w_o, sgu_w_uv, sgu_ln_g, sgu_ln_b, sgu_w_s, sgu_b_s, sgu_w_out, norm_ffn_g, ffn_w_up, ffn_conv_w, ffn_conv_b, ffn_w_down, norm_ple_g, ple_w_gate, ple_w_proj, final_norm_g):
    depth = norm_mix_g.shape[0]
    assert depth == 2 and attn_w_qkv.shape[0] == 1 and sgu_w_uv.shape[0] == 1
    d = x_prompt.shape[-1]
    row = lambda a: a.reshape(a.shape[:-1] + (1, a.shape[-1]))
    w = dict(
        norm_mix_g=row(norm_mix_g), norm_ffn_g=row(norm_ffn_g), norm_ple_g=row(norm_ple_g),
        final_norm_g=final_norm_g.reshape(1, d),
        attn_w_qkv=attn_w_qkv[0].astype(BF16), attn_w_o=attn_w_o[0].astype(BF16),
        attn_lq1=attn_lq1, attn_lk1=attn_lk1, attn_lq2=attn_lq2, attn_lk2=attn_lk2, attn_subln_g=attn_subln_g,
        sgu_w_uv=sgu_w_uv[0].astype(BF16), sgu_ln_g=sgu_ln_g, sgu_ln_b=sgu_ln_b,
        sgu_w_s=sgu_w_s[0].astype(BF16), sgu_b_s=sgu_b_s[0][:, :, None], sgu_w_out=sgu_w_out[0].astype(BF16),
        ffn_w_up=ffn_w_up.astype(BF16), ffn_conv_w=ffn_conv_w, ffn_conv_b=row(ffn_conv_b),
        ffn_w_down=ffn_w_down.astype(BF16), ple_w_gate=ple_w_gate.astype(BF16), ple_w_proj=ple_w_proj.astype(BF16),
    )
    return _trunk(x_prompt, p_prompt, w), _trunk(x_sample, p_sample, w)
```

```python
import functools
import math

import jax
import jax.numpy as jnp
from jax import lax
from jax.experimental import pallas as pl
from jax.experimental.pallas import tpu as pltpu

F32 = jnp.float32
BF16 = jnp.bfloat16

EPS = 1e-6
ROPE_THETA = 10000.0
HEAD_DIM = 64
CHUNK = 128
CONV_WIDTH = 3
HALO = 8
LANES = 128
MXU_COLS = 256
ONES_ROWS = 16

VMEM_LIMIT = 56 * 1024 * 1024


def _rms(x, g):
    return x * lax.rsqrt(jnp.mean(x * x, axis=-1, keepdims=True) + EPS) * g


def _const_spec(shape):
    nd = len(shape)
    return pl.BlockSpec(shape, lambda *_: (0,) * nd, pipeline_mode=pl.Buffered(1))


def _params(n_axes, flags=None):
    return pltpu.CompilerParams(dimension_semantics=("arbitrary",) * n_axes,
                                vmem_limit_bytes=VMEM_LIMIT, flags=flags)


def _attn_pre_kernel(x_ref, g_ref, w_ref, cq_ref, sq_ref, ck_ref, sk_ref, qt_ref, k_ref, vt_ref):
    tm, d = x_ref.shape
    hd = 2 * HEAD_DIM
    h = _rms(x_ref[...], g_ref[...]).astype(BF16)
    qkv = jnp.dot(h, w_ref[...], preferred_element_type=F32)
    lane = lax.broadcasted_iota(jnp.int32, (tm, LANES), 1)
    first_half = (lane % HEAD_DIM) < (HEAD_DIM // 2)

    def rope(t, cos, sin):
        rot = jnp.where(first_half, pltpu.roll(t, LANES - HEAD_DIM // 2, 1), pltpu.roll(t, HEAD_DIM // 2, 1))
        return t * cos + rot * sin

    cq, sq, ck, sk = cq_ref[...], sq_ref[...], ck_ref[...], sk_ref[...]
    for j in range(d // hd):
        cols = slice(j * hd, (j + 1) * hd)
        qt_ref[j] = rope(qkv[:, j * hd:(j + 1) * hd], cq, sq).T.astype(BF16)
        k_ref[:, cols] = rope(qkv[:, d + j * hd:d + (j + 1) * hd], ck, sk).astype(BF16)
        vt_ref[j, 0:hd, :] = qkv[:, 2 * d + j * hd:2 * d + (j + 1) * hd].T.astype(BF16)
        vt_ref[j, hd:hd + ONES_ROWS, :] = jnp.ones((ONES_ROWS, tm), BF16)


def _attn_pre(x, g, w_qkv, tables, seq, tm):
    n, d = x.shape
    hd = 2 * HEAD_DIM
    heads = d // hd
    tps = seq // tm
    tab_spec = pl.BlockSpec((tm, LANES), lambda i: (i % tps, 0))
    return pl.pallas_call(
        _attn_pre_kernel,
        grid=(n // tm,),
        in_specs=[pl.BlockSpec((tm, d), lambda i: (i, 0)), _const_spec((1, d)), _const_spec((d, 3 * d)),
                  tab_spec, tab_spec, tab_spec, tab_spec],
        out_specs=[pl.BlockSpec((heads, hd, tm), lambda i: (0, 0, i)),
                   pl.BlockSpec((tm, d), lambda i: (i, 0)),
                   pl.BlockSpec((heads, hd + ONES_ROWS, tm), lambda i: (0, 0, i))],
        out_shape=[jax.ShapeDtypeStruct((heads, hd, n), BF16), jax.ShapeDtypeStruct((n, d), BF16),
                   jax.ShapeDtypeStruct((heads, hd + ONES_ROWS, n), BF16)],
        compiler_params=_params(1),
        name="attn_pre",
    )(x, g, w_qkv, *tables)


def _flash_kernel(qt_ref, k_ref, vt_ref, lq1_ref, lk1_ref, lq2_ref, lk2_ref, g_ref, o_ref,
                  q2t_s, m_s, acc_s, *bufs, tk, group, lambda_init):
    hd, tq = qt_ref.shape[1], qt_ref.shape[2]
    seq = k_ref.shape[1]
    qt = qt_ref[0]
    row = lax.broadcasted_iota(jnp.int32, qt.shape, 0)
    zero = jnp.zeros_like(qt)
    q2t_s[:, 0:tq] = jnp.where(row < HEAD_DIM, qt, zero)
    q2t_s[:, tq:2 * tq] = jnp.where(row >= HEAD_DIM, qt, zero)
    m_s[...] = jnp.full(m_s.shape, -jnp.inf, F32)
    acc_s[...] = jnp.zeros(acc_s.shape, F32)
    slots = 2 * group
    s_bufs, p_bufs, a_bufs, x_bufs = (bufs[i * slots:(i + 1) * slots] for i in range(4))
    n_groups = seq // (tk * group)
    slabs_per_piece = MXU_COLS // LANES

    def work(values=None, softmax_half=None, scores=None):
        for g in range(group):
            if softmax_half is not None:
                slot_m = softmax_half * group + g
                m_old = m_s[...]
                m_new = jnp.maximum(m_old, x_bufs[slot_m][...])
                a_bufs[slot_m][...] = jnp.exp2(m_old - m_new)
                m_s[...] = m_new
            if values is not None:
                slot_v = values[1] * group + g
                start_v = pl.multiple_of((values[0] * group + g) * tk, tk)
            if scores is not None:
                slot_s = scores[1] * group + g
                start_s = pl.multiple_of((scores[0] * group + g) * tk, tk)
            for c in range(2 * tq // MXU_COLS):
                cols = slice(c * MXU_COLS, (c + 1) * MXU_COLS)
                slabs = range(c * slabs_per_piece, (c + 1) * slabs_per_piece)
                if values is not None:
                    p = jnp.concatenate([p_bufs[slot_v][j] for j in slabs], axis=1)
                    pv = jnp.dot(vt_ref[0, :, pl.ds(start_v, tk)], p, preferred_element_type=F32)
                    acc_s[:, cols] = a_bufs[slot_v][:, cols] * acc_s[:, cols] + pv
                if softmax_half is not None:
                    for j in slabs:
                        lcols = slice(j * LANES, (j + 1) * LANES)
                        p_bufs[slot_m][j] = jnp.exp2(s_bufs[slot_m][j] - m_new[:, lcols]).astype(BF16)
                if scores is not None:
                    s = jnp.dot(k_ref[0, pl.ds(start_s, tk), :], q2t_s[:, cols], preferred_element_type=F32)
                    for i, j in enumerate(slabs):
                        s_bufs[slot_s][j] = s[:, i * LANES:(i + 1) * LANES]
                    x_bufs[slot_s][:, cols] = jnp.max(s, axis=0, keepdims=True)

    def step(b, half):
        work(values=(b, half), softmax_half=1 - half, scores=(b + 2, half))

    work(scores=(0, 0))
    work(softmax_half=0, scores=(1, 1))
    steady = n_groups - 2
    for b in range(steady % 2):
        step(b, b % 2)

    def body(i, carry):
        for r in range(2):
            step(steady % 2 + 2 * i + r, (steady + r) % 2)
        return carry

    lax.fori_loop(0, steady // 2, body, 0)
    work(values=(n_groups - 2, n_groups % 2), softmax_half=(n_groups - 1) % 2)
    work(values=(n_groups - 1, (n_groups - 1) % 2))

    acc = acc_s[...]
    o_maps = acc[0:hd] / acc[hd:hd + 1]
    lam = (jnp.exp(jnp.sum(lq1_ref[...] * lk1_ref[...], axis=-1, keepdims=True))
           - jnp.exp(jnp.sum(lq2_ref[...] * lk2_ref[...], axis=-1, keepdims=True)) + lambda_init)
    o = (o_maps[:, 0:tq] - lam * o_maps[:, tq:2 * tq]).T
    o_ref[0] = (_rms(o, g_ref[...]) * (1.0 - lambda_init)).astype(o_ref.dtype)


def _flash(qt, k, vt, lq1, lk1, lq2, lk2, subln_g, lambda_init, tq, tk, group):
    b, seq, d = k.shape
    heads, hd, _ = qt.shape
    vrows = vt.shape[1]
    qtiles = seq // tq
    slots = 2 * group
    assert seq % (tk * group) == 0 and seq // (tk * group) >= 2, (seq, tk, group)
    small = _const_spec((1, HEAD_DIM))
    return pl.pallas_call(
        functools.partial(_flash_kernel, tk=tk, group=group, lambda_init=lambda_init),
        grid=(b, heads, qtiles),
        in_specs=[pl.BlockSpec((1, hd, tq), lambda bi, h, qi: (h, 0, bi * qtiles + qi)),
                  pl.BlockSpec((1, seq, hd), lambda bi, h, qi: (bi, 0, h)),
                  pl.BlockSpec((1, vrows, seq), lambda bi, h, qi: (h, 0, bi)),
                  small, small, small, small, _const_spec((1, hd))],
        out_specs=pl.BlockSpec((1, tq, hd), lambda bi, h, qi: (bi, qi, h)),
        out_shape=jax.ShapeDtypeStruct((b, seq, d), BF16),
        scratch_shapes=[pltpu.VMEM((hd, 2 * tq), BF16), pltpu.VMEM((1, 2 * tq), F32),
                        pltpu.VMEM((vrows, 2 * tq), F32)]
        + [pltpu.VMEM((2 * tq // LANES, tk, LANES), F32)] * slots
        + [pltpu.VMEM((2 * tq // LANES, tk, LANES), BF16)] * slots
        + [pltpu.VMEM((1, 2 * tq), F32)] * slots
        + [pltpu.VMEM((1, 2 * tq), F32)] * slots,
        compiler_params=_params(3),
        name="flash_diff_attn",
    )(qt, k, vt, lq1, lk1, lq2, lk2, subln_g)


def _attn_post_kernel(x_ref, o_ref, w_ref, y_ref):
    y_ref[...] = x_ref[...] + jnp.dot(o_ref[...], w_ref[...], preferred_element_type=F32)


def _attn_post(x, o, w_o, tm):
    n, d = x.shape
    row = pl.BlockSpec((tm, d), lambda i: (i, 0))
    return pl.pallas_call(
        _attn_post_kernel,
        grid=(n // tm,),
        in_specs=[row, row, _const_spec((d, d))],
        out_specs=row,
        out_shape=jax.ShapeDtypeStruct((n, d), F32),
        compiler_params=_params(1),
        name="attn_post",
    )(x, o, w_o)


def _ffn_ple_kernel(x_ref, xp_ref, xn_ref, p_ref, gf_ref, wup_ref, cw_ref, cb_ref, wdn_ref,
                    gp_ref, wg_ref, wp_ref, gfin_ref, y_ref, a_s, act_s, *, tiles_per_seq, fc, final):
    tm, d = x_ref.shape
    f = wdn_ref.shape[0]
    i = pl.program_id(0)
    keep_prev = jnp.where(i % tiles_per_seq == 0, 0.0, 1.0).astype(F32)
    keep_next = jnp.where(i % tiles_per_seq == tiles_per_seq - 1, 0.0, 1.0).astype(F32)
    x = x_ref[...]
    gf = gf_ref[...]
    h = jnp.concatenate([_rms(xp_ref[...], gf) * keep_prev, _rms(x, gf), _rms(xn_ref[...], gf) * keep_next], axis=0)
    a_s[...] = jnp.dot(h.astype(BF16), wup_ref[...], preferred_element_type=F32)

    def conv(col0):
        cols = pl.ds(col0, fc)
        c = cb_ref[:, cols]
        for t in range(CONV_WIDTH):
            c = c + a_s[pl.ds(HALO - 1 + t, tm), cols] * cw_ref[pl.ds(t, 1), cols]
        return c

    for j in range(f // fc):
        val = conv(j * fc)
        gate = conv(f + j * fc)
        act_s[:, pl.ds(j * fc, fc)] = (gate * jax.nn.sigmoid(gate) * val).astype(BF16)

    x = x + jnp.dot(act_s[...], wdn_ref[...], preferred_element_type=F32)
    gate = jax.nn.sigmoid(jnp.dot(_rms(x, gp_ref[...]).astype(BF16), wg_ref[...], preferred_element_type=F32))
    x = x + gate * jnp.dot(p_ref[...].astype(BF16), wp_ref[...], preferred_element_type=F32)
    if final:
        x = _rms(x, gfin_ref[...])
    y_ref[...] = x


def _ffn_ple(x, p, g_ffn, w_up, conv_w, conv_b, w_down, g_ple, w_gate, w_proj, g_final, seq, tm, final):
    n, d = x.shape
    f = w_down.shape[0]
    pd = p.shape[1]
    fc = 2 * LANES
    tiles_per_seq = seq // tm
    hb = tm // HALO
    last_blk = n // HALO - 1
    return pl.pallas_call(
        functools.partial(_ffn_ple_kernel, tiles_per_seq=tiles_per_seq, fc=fc, final=final),
        grid=(n // tm,),
        in_specs=[pl.BlockSpec((tm, d), lambda i: (i, 0)),
                  pl.BlockSpec((HALO, d), lambda i: (jnp.maximum(i * hb - 1, 0), 0)),
                  pl.BlockSpec((HALO, d), lambda i: (jnp.minimum((i + 1) * hb, last_blk), 0)),
                  pl.BlockSpec((tm, pd), lambda i: (i, 0)),
                  _const_spec((1, d)), _const_spec((d, 2 * f)), _const_spec((CONV_WIDTH, 2 * f)),
                  _const_spec((1, 2 * f)), _const_spec((f, d)), _const_spec((1, d)), _const_spec((d, d)),
                  _const_spec((pd, d)), _const_spec((1, d))],
        out_specs=pl.BlockSpec((tm, d), lambda i: (i, 0)),
        out_shape=jax.ShapeDtypeStruct((n, d), F32),
        scratch_shapes=[pltpu.VMEM((tm + 2 * HALO, 2 * f), F32), pltpu.VMEM((tm, f), BF16)],
        compiler_params=_params(1),
        name="ffn_ple_final" if final else "ffn_ple",
    )(x, x, x, p, g_ffn, w_up, conv_w, conv_b, w_down, g_ple, w_gate, w_proj, g_final)


def _sgu_kernel(x_ref, g_ref, wuv_ref, lng_ref, lnb_ref, ws_ref, bs_ref, wout_ref, y_ref, um_s):
    tm, d = x_ref.shape
    w = wout_ref.shape[0]
    x = x_ref[...]
    z = jnp.dot(_rms(x, g_ref[...]).astype(BF16), wuv_ref[...], preferred_element_type=F32)
    z = 0.5 * z * (1.0 + lax.erf(z * (1.0 / math.sqrt(2.0))))
    v = z[:, w:]
    vc = v - jnp.mean(v, axis=-1, keepdims=True)
    v = vc * lax.rsqrt(jnp.mean(vc * vc, axis=-1, keepdims=True) + EPS) * lng_ref[...] + lnb_ref[...]
    vb = v.astype(BF16)
    for c in range(tm // CHUNK):
        rows = slice(c * CHUNK, (c + 1) * CHUNK)
        for grp in range(w // CHUNK):
            cols = slice(grp * CHUNK, (grp + 1) * CHUNK)
            mixed = jnp.dot(ws_ref[grp], vb[rows, cols], preferred_element_type=F32) + bs_ref[grp]
            um_s[rows, cols] = (z[rows, cols] * mixed).astype(BF16)
    y_ref[...] = x + jnp.dot(um_s[...], wout_ref[...], preferred_element_type=F32)


def _sgu(x, g, w_uv, ln_g, ln_b, w_s, b_s, w_out, tm):
    n, d = x.shape
    w = w_out.shape[0]
    ng = w // CHUNK
    row = pl.BlockSpec((tm, d), lambda i: (i, 0))
    return pl.pallas_call(
        _sgu_kernel,
        grid=(n // tm,),
        in_specs=[row, _const_spec((1, d)), _const_spec((d, 2 * w)), _const_spec((1, w)), _const_spec((1, w)),
                  _const_spec((ng, CHUNK, CHUNK)), _const_spec((ng, CHUNK, 1)), _const_spec((w, d))],
        out_specs=row,
        out_shape=jax.ShapeDtypeStruct((n, d), F32),
        scratch_shapes=[pltpu.VMEM((tm, w), BF16)],
        compiler_params=_params(1),
        name="sgu",
    )(x, g, w_uv, ln_g, ln_b, w_s, b_s, w_out)


def _rope_tables(seq):
    inv = 1.0 / (ROPE_THETA ** (jnp.arange(0, HEAD_DIM, 2, dtype=F32) / HEAD_DIM))
    ang = jnp.arange(seq, dtype=F32)[:, None] * inv[None, :]
    ang = jnp.concatenate([ang, ang, ang, ang], axis=-1)
    half = HEAD_DIM // 2
    sign = jnp.where((jnp.arange(LANES) % HEAD_DIM) < half, -1.0, 1.0).astype(F32)
    cos, sin = jnp.cos(ang), jnp.sin(ang) * sign
    q_scale = (HEAD_DIM ** -0.5) * math.log2(math.e)
    return cos * q_scale, sin * q_scale, cos, sin


def _flash_tiles(seq):
    group = 2
    tq = _tile(seq, 512)
    tk = _tile(seq, max(LANES, min(512, seq // (4 * group))))
    return tq, tk, group


def _tile(n, want):
    t = min(n, want)
    assert n % t == 0, (n, t)
    return t


def _trunk(x, p, w):
    b, seq, d = x.shape
    n = b * seq
    x = x.reshape(n, d)
    tm = _tile(seq, 512)
    qt, k, vt = _attn_pre(x, w["norm_mix_g"][0], w["attn_w_qkv"], _rope_tables(seq), seq, tm)
    lambda_init = 0.8 - 0.6 * math.exp(-0.3 * 0)
    o = _flash(qt, k.reshape(b, seq, d), vt,
               w["attn_lq1"], w["attn_lk1"], w["attn_lq2"], w["attn_lk2"], w["attn_subln_g"],
               lambda_init, *_flash_tiles(seq))
    x = _attn_post(x, o.reshape(n, d), w["attn_w_o"], tm)
    x = _ffn_ple(x, p[0].reshape(n, -1), w["norm_ffn_g"][0], w["ffn_w_up"][0], w["ffn_conv_w"][0],
                 w["ffn_conv_b"][0], w["ffn_w_down"][0], w["norm_ple_g"][0], w["ple_w_gate"][0],
                 w["ple_w_proj"][0], w["final_norm_g"], seq, tm, final=False)
    x = _sgu(x, w["norm_mix_g"][1], w["sgu_w_uv"], w["sgu_ln_g"], w["sgu_ln_b"], w["sgu_w_s"], w["sgu_b_s"],
             w["sgu_w_out"], tm)
    x = _ffn_ple(x, p[1].reshape(n, -1), w["norm_ffn_g"][1], w["ffn_w_up"][1], w["ffn_conv_w"][1],
                 w["ffn_conv_b"][1], w["ffn_w_down"][1], w["norm_ple_g"][1], w["ple_w_gate"][1],
                 w["ple_w_proj"][1], w["final_norm_g"], seq, tm, final=True)
    return x.reshape(b, seq, d)


def kernel(x_prompt, x_sample, p_prompt, p_sample, norm_mix_g, attn_w_qkv, attn_lq1, attn_lk1, attn_lq2, attn_lk2, attn_subln_g, attn_w_o, sgu_w_uv, sgu_ln_g, sgu_ln_b, sgu_w_s, sgu_b_s, sgu_w_out, norm_ffn_g, ffn_w_up, ffn_conv_w, ffn_conv_b, ffn_w_down, norm_ple_g, ple_w_gate, ple_w_proj, final_norm_g):
    depth = norm_mix_g.shape[0]
    assert depth == 2 and attn_w_qkv.shape[0] == 1 and sgu_w_uv.shape[0] == 1
    d = x_prompt.shape[-1]
    row = lambda a: a.reshape(a.shape[:-1] + (1, a.shape[-1]))
    w = dict(
        norm_mix_g=row(norm_mix_g), norm_ffn_g=row(norm_ffn_g), norm_ple_g=row(norm_ple_g),
        final_norm_g=final_norm_g.reshape(1, d),
        attn_w_qkv=attn_w_qkv[0].astype(BF16), attn_w_o=attn_w_o[0].astype(BF16),
        attn_lq1=attn_lq1, attn_lk1=attn_lk1, attn_lq2=attn_lq2, attn_lk2=attn_lk2, attn_subln_g=attn_subln_g,
        sgu_w_uv=sgu_w_uv[0].astype(BF16), sgu_ln_g=sgu_ln_g, sgu_ln_b=sgu_ln_b,
        sgu_w_s=sgu_w_s[0].astype(BF16), sgu_b_s=sgu_b_s[0][:, :, None], sgu_w_out=sgu_w_out[0].astype(BF16),
        ffn_w_up=ffn_w_up.astype(BF16), ffn_conv_w=ffn_conv_w, ffn_conv_b=row(ffn_conv_b),
        ffn_w_down=ffn_w_down.astype(BF16), ple_w_gate=ple_w_gate.astype(BF16), ple_w_proj=ple_w_proj.astype(BF16),
    )
    return _trunk(x_prompt, p_prompt, w), _trunk(x_sample, p_sample, w)
```

```python
import functools
import math

import jax
import jax.numpy as jnp
from jax import lax
from jax.experimental import pallas as pl
from jax.experimental.pallas import tpu as pltpu

F32 = jnp.float32
BF16 = jnp.bfloat16

EPS = 1e-6
ROPE_THETA = 10000.0
HEAD_DIM = 64
CHUNK = 128
CONV_WIDTH = 3
F32_ROWS = 8
BF16_ROWS = 16
HALO = F32_ROWS
LANES = 128
MXU_COLS = 256
ONES_ROWS = 16

VMEM_LIMIT = 56 * 1024 * 1024


def _rms(x, g):
    return x * lax.rsqrt(jnp.mean(x * x, axis=-1, keepdims=True) + EPS) * g


def _const_spec(shape):
    nd = len(shape)
    return pl.BlockSpec(shape, lambda *_: (0,) * nd, pipeline_mode=pl.Buffered(1))


def _params(n_axes, flags=None):
    return pltpu.CompilerParams(dimension_semantics=("arbitrary",) * n_axes,
                                vmem_limit_bytes=VMEM_LIMIT, flags=flags)


def _attn_pre_kernel(x_ref, g_ref, w_ref, cq_ref, sq_ref, ck_ref, sk_ref, qt_ref, k_ref, vt_ref):
    tm, d = x_ref.shape
    hd = 2 * HEAD_DIM
    h = _rms(x_ref[...], g_ref[...]).astype(BF16)
    qkv = jnp.dot(h, w_ref[...], preferred_element_type=F32)
    lane = lax.broadcasted_iota(jnp.int32, (tm, LANES), 1)
    first_half = (lane % HEAD_DIM) < (HEAD_DIM // 2)

    def rope(t, cos, sin):
        rot = jnp.where(first_half, pltpu.roll(t, LANES - HEAD_DIM // 2, 1), pltpu.roll(t, HEAD_DIM // 2, 1))
        return t * cos + rot * sin

    cq, sq, ck, sk = cq_ref[...], sq_ref[...], ck_ref[...], sk_ref[...]
    for j in range(d // hd):
        cols = slice(j * hd, (j + 1) * hd)
        qt_ref[j] = rope(qkv[:, j * hd:(j + 1) * hd], cq, sq).T.astype(BF16)
        k_ref[:, cols] = rope(qkv[:, d + j * hd:d + (j + 1) * hd], ck, sk).astype(BF16)
        vt_ref[j, 0:hd, :] = qkv[:, 2 * d + j * hd:2 * d + (j + 1) * hd].T.astype(BF16)
        vt_ref[j, hd:hd + ONES_ROWS, :] = jnp.ones((ONES_ROWS, tm), BF16)


def _attn_pre(x, g, w_qkv, tables, seq, tm):
    n, d = x.shape
    hd = 2 * HEAD_DIM
    heads = d // hd
    tps = seq // tm
    tab_spec = pl.BlockSpec((tm, LANES), lambda i: (i % tps, 0))
    return pl.pallas_call(
        _attn_pre_kernel,
        grid=(n // tm,),
        in_specs=[pl.BlockSpec((tm, d), lambda i: (i, 0)), _const_spec((1, d)), _const_spec((d, 3 * d)),
                  tab_spec, tab_spec, tab_spec, tab_spec],
        out_specs=[pl.BlockSpec((heads, hd, tm), lambda i: (0, 0, i)),
                   pl.BlockSpec((tm, d), lambda i: (i, 0)),
                   pl.BlockSpec((heads, hd + ONES_ROWS, tm), lambda i: (0, 0, i))],
        out_shape=[jax.ShapeDtypeStruct((heads, hd, n), BF16), jax.ShapeDtypeStruct((n, d), BF16),
                   jax.ShapeDtypeStruct((heads, hd + ONES_ROWS, n), BF16)],
        compiler_params=_params(1),
        name="attn_pre",
    )(x, g, w_qkv, *tables)


def _flash_kernel(qt_ref, k_ref, vt_ref, lq1_ref, lk1_ref, lq2_ref, lk2_ref, g_ref, o_ref,
                  q2t_s, m_s, acc_s, *bufs, tk, group, lambda_init):
    hd, tq = qt_ref.shape[1], qt_ref.shape[2]
    seq = k_ref.shape[1]
    qt = qt_ref[0]
    row = lax.broadcasted_iota(jnp.int32, qt.shape, 0)
    zero = jnp.zeros_like(qt)
    q2t_s[:, 0:tq] = jnp.where(row < HEAD_DIM, qt, zero)
    q2t_s[:, tq:2 * tq] = jnp.where(row >= HEAD_DIM, qt, zero)
    m_s[...] = jnp.full(m_s.shape, -jnp.inf, F32)
    acc_s[...] = jnp.zeros(acc_s.shape, F32)
    slots = 2 * group
    s_bufs, p_bufs, a_bufs, x_bufs = (bufs[i * slots:(i + 1) * slots] for i in range(4))
    n_groups = seq // (tk * group)
    slabs_per_piece = MXU_COLS // LANES
    vrows = vt_ref.shape[1]

    def work(values=None, softmax_half=None, scores=None):
        for g in range(group):
            if softmax_half is not None:
                slot_m = softmax_half * group + g
                m_old = m_s[...]
                m_new = jnp.maximum(m_old, x_bufs[slot_m][...])
                a_bufs[slot_m][...] = jnp.exp2(m_old - m_new)
                m_s[...] = m_new
            if values is not None:
                slot_v = values[1] * group + g
                start_v = pl.multiple_of((values[0] * group + g) * tk, tk)
            if scores is not None:
                slot_s = scores[1] * group + g
                start_s = pl.multiple_of((scores[0] * group + g) * tk, tk)
            for c in range(2 * tq // MXU_COLS):
                cols = slice(c * MXU_COLS, (c + 1) * MXU_COLS)
                slabs = range(c * slabs_per_piece, (c + 1) * slabs_per_piece)
                if values is not None:
                    p = jnp.concatenate([p_bufs[slot_v][j, 0:tk] for j in slabs], axis=1)
                    pv = jnp.dot(vt_ref[0, :, pl.ds(start_v, tk)], p, preferred_element_type=F32)
                    for i, j in enumerate(slabs):
                        alpha = a_bufs[slot_v][:, j * LANES:(j + 1) * LANES]
                        acc_s[j, 0:vrows] = alpha * acc_s[j, 0:vrows] + pv[:, i * LANES:(i + 1) * LANES]
                if softmax_half is not None:
                    for j in slabs:
                        lcols = slice(j * LANES, (j + 1) * LANES)
                        p_bufs[slot_m][j, 0:tk] = jnp.exp2(s_bufs[slot_m][j, 0:tk] - m_new[:, lcols]).astype(BF16)
                if scores is not None:
                    s = jnp.dot(k_ref[0, pl.ds(start_s, tk), :], q2t_s[:, cols], preferred_element_type=F32)
                    for i, j in enumerate(slabs):
                        s_bufs[slot_s][j, 0:tk] = s[:, i * LANES:(i + 1) * LANES]
                    x_bufs[slot_s][:, cols] = jnp.max(s, axis=0, keepdims=True)

    def step(b, half):
        work(values=(b, half), softmax_half=1 - half, scores=(b + 2, half))

    work(scores=(0, 0))
    work(softmax_half=0, scores=(1, 1))
    steady = n_groups - 2
    for b in range(steady % 2):
        step(b, b % 2)

    def body(i, carry):
        for r in range(2):
            step(steady % 2 + 2 * i + r, (steady + r) % 2)
        return carry

    lax.fori_loop(0, steady // 2, body, 0)
    work(values=(n_groups - 2, n_groups % 2), softmax_half=(n_groups - 1) % 2)
    work(values=(n_groups - 1, (n_groups - 1) % 2))

    acc = jnp.concatenate([acc_s[j, 0:vrows] for j in range(2 * tq // LANES)], axis=1)
    o_maps = acc[0:hd] / acc[hd:hd + 1]
    lam = (jnp.exp(jnp.sum(lq1_ref[...] * lk1_ref[...], axis=-1, keepdims=True))
           - jnp.exp(jnp.sum(lq2_ref[...] * lk2_ref[...], axis=-1, keepdims=True)) + lambda_init)
    o = (o_maps[:, 0:tq] - lam * o_maps[:, tq:2 * tq]).T
    o_ref[0] = (_rms(o, g_ref[...]) * (1.0 - lambda_init)).astype(o_ref.dtype)


def _flash(qt, k, vt, lq1, lk1, lq2, lk2, subln_g, lambda_init, tq, tk, group):
    b, seq, d = k.shape
    heads, hd, _ = qt.shape
    vrows = vt.shape[1]
    qtiles = seq // tq
    slots = 2 * group
    assert seq % (tk * group) == 0 and seq // (tk * group) >= 2, (seq, tk, group)
    small = _const_spec((1, HEAD_DIM))
    return pl.pallas_call(
        functools.partial(_flash_kernel, tk=tk, group=group, lambda_init=lambda_init),
        grid=(b, heads, qtiles),
        in_specs=[pl.BlockSpec((1, hd, tq), lambda bi, h, qi: (h, 0, bi * qtiles + qi)),
                  pl.BlockSpec((1, seq, hd), lambda bi, h, qi: (bi, 0, h)),
                  pl.BlockSpec((1, vrows, seq), lambda bi, h, qi: (h, 0, bi)),
                  small, small, small, small, _const_spec((1, hd))],
        out_specs=pl.BlockSpec((1, tq, hd), lambda bi, h, qi: (bi, qi, h)),
        out_shape=jax.ShapeDtypeStruct((b, seq, d), BF16),
        scratch_shapes=[pltpu.VMEM((hd, 2 * tq), BF16), pltpu.VMEM((1, 2 * tq), F32),
                        pltpu.VMEM((2 * tq // LANES, vrows + F32_ROWS, LANES), F32)]
        + [pltpu.VMEM((2 * tq // LANES, tk + F32_ROWS, LANES), F32)] * slots
        + [pltpu.VMEM((2 * tq // LANES, tk + BF16_ROWS, LANES), BF16)] * slots
        + [pltpu.VMEM((1, 2 * tq), F32)] * slots
        + [pltpu.VMEM((1, 2 * tq), F32)] * slots,
        compiler_params=_params(3),
        name="flash_diff_attn",
    )(qt, k, vt, lq1, lk1, lq2, lk2, subln_g)


def _attn_post_kernel(x_ref, o_ref, w_ref, y_ref):
    y_ref[...] = x_ref[...] + jnp.dot(o_ref[...], w_ref[...], preferred_element_type=F32)


def _attn_post(x, o, w_o, tm):
    n, d = x.shape
    row = pl.BlockSpec((tm, d), lambda i: (i, 0))
    return pl.pallas_call(
        _attn_post_kernel,
        grid=(n // tm,),
        in_specs=[row, row, _const_spec((d, d))],
        out_specs=row,
        out_shape=jax.ShapeDtypeStruct((n, d), F32),
        compiler_params=_params(1),
        name="attn_post",
    )(x, o, w_o)


def _ffn_ple_kernel(x_ref, xp_ref, xn_ref, p_ref, gf_ref, wup_ref, cw_ref, cb_ref, wdn_ref,
                    gp_ref, wg_ref, wp_ref, gfin_ref, y_ref, a_s, act_s, *, tiles_per_seq, fc, final):
    tm, d = x_ref.shape
    f = wdn_ref.shape[0]
    i = pl.program_id(0)
    keep_prev = jnp.where(i % tiles_per_seq == 0, 0.0, 1.0).astype(F32)
    keep_next = jnp.where(i % tiles_per_seq == tiles_per_seq - 1, 0.0, 1.0).astype(F32)
    x = x_ref[...]
    gf = gf_ref[...]
    h = jnp.concatenate([_rms(xp_ref[...], gf) * keep_prev, _rms(x, gf), _rms(xn_ref[...], gf) * keep_next], axis=0)
    a_s[...] = jnp.dot(h.astype(BF16), wup_ref[...], preferred_element_type=F32)

    def conv(col0):
        cols = pl.ds(col0, fc)
        c = cb_ref[:, cols]
        for t in range(CONV_WIDTH):
            c = c + a_s[pl.ds(HALO - 1 + t, tm), cols] * cw_ref[pl.ds(t, 1), cols]
        return c

    for j in range(f // fc):
        val = conv(j * fc)
        gate = conv(f + j * fc)
        act_s[:, pl.ds(j * fc, fc)] = (gate * jax.nn.sigmoid(gate) * val).astype(BF16)

    x = x + jnp.dot(act_s[...], wdn_ref[...], preferred_element_type=F32)
    gate = jax.nn.sigmoid(jnp.dot(_rms(x, gp_ref[...]).astype(BF16), wg_ref[...], preferred_element_type=F32))
    x = x + gate * jnp.dot(p_ref[...].astype(BF16), wp_ref[...], preferred_element_type=F32)
    if final:
        x = _rms(x, gfin_ref[...])
    y_ref[...] = x


def _ffn_ple(x, p, g_ffn, w_up, conv_w, conv_b, w_down, g_ple, w_gate, w_proj, g_final, seq, tm, final):
    n, d = x.shape
    f = w_down.shape[0]
    pd = p.shape[1]
    fc = 2 * LANES
    tiles_per_seq = seq // tm
    hb = tm // HALO
    last_blk = n // HALO - 1
    return pl.pallas_call(
        functools.partial(_ffn_ple_kernel, tiles_per_seq=tiles_per_seq, fc=fc, final=final),
        grid=(n // tm,),
        in_specs=[pl.BlockSpec((tm, d), lambda i: (i, 0)),
                  pl.BlockSpec((HALO, d), lambda i: (jnp.maximum(i * hb - 1, 0), 0)),
                  pl.BlockSpec((HALO, d), lambda i: (jnp.minimum((i + 1) * hb, last_blk), 0)),
                  pl.BlockSpec((tm, pd), lambda i: (i, 0)),
                  _const_spec((1, d)), _const_spec((d, 2 * f)), _const_spec((CONV_WIDTH, 2 * f)),
                  _const_spec((1, 2 * f)), _const_spec((f, d)), _const_spec((1, d)), _const_spec((d, d)),
                  _const_spec((pd, d)), _const_spec((1, d))],
        out_specs=pl.BlockSpec((tm, d), lambda i: (i, 0)),
        out_shape=jax.ShapeDtypeStruct((n, d), F32),
        scratch_shapes=[pltpu.VMEM((tm + 2 * HALO, 2 * f), F32), pltpu.VMEM((tm, f), BF16)],
        compiler_params=_params(1),
        name="ffn_ple_final" if final else "ffn_ple",
    )(x, x, x, p, g_ffn, w_up, conv_w, conv_b, w_down, g_ple, w_gate, w_proj, g_final)


def _sgu_kernel(x_ref, g_ref, wuv_ref, lng_ref, lnb_ref, ws_ref, bs_ref, wout_ref, y_ref, um_s):
    tm, d = x_ref.shape
    w = wout_ref.shape[0]
    x = x_ref[...]
    z = jnp.dot(_rms(x, g_ref[...]).astype(BF16), wuv_ref[...], preferred_element_type=F32)
    z = 0.5 * z * (1.0 + lax.erf(z * (1.0 / math.sqrt(2.0))))
    v = z[:, w:]
    vc = v - jnp.mean(v, axis=-1, keepdims=True)
    v = vc * lax.rsqrt(jnp.mean(vc * vc, axis=-1, keepdims=True) + EPS) * lng_ref[...] + lnb_ref[...]
    vb = v.astype(BF16)
    for c in range(tm // CHUNK):
        rows = slice(c * CHUNK, (c + 1) * CHUNK)
        for grp in range(w // CHUNK):
            cols = slice(grp * CHUNK, (grp + 1) * CHUNK)
            mixed = jnp.dot(ws_ref[grp], vb[rows, cols], preferred_element_type=F32) + bs_ref[grp]
            um_s[rows, cols] = (z[rows, cols] * mixed).astype(BF16)
    y_ref[...] = x + jnp.dot(um_s[...], wout_ref[...], preferred_element_type=F32)


def _sgu(x, g, w_uv, ln_g, ln_b, w_s, b_s, w_out, tm):
    n, d = x.shape
    w = w_out.shape[0]
    ng = w // CHUNK
    row = pl.BlockSpec((tm, d), lambda i: (i, 0))
    return pl.pallas_call(
        _sgu_kernel,
        grid=(n // tm,),
        in_specs=[row, _const_spec((1, d)), _const_spec((d, 2 * w)), _const_spec((1, w)), _const_spec((1, w)),
                  _const_spec((ng, CHUNK, CHUNK)), _const_spec((ng, CHUNK, 1)), _const_spec((w, d))],
        out_specs=row,
        out_shape=jax.ShapeDtypeStruct((n, d), F32),
        scratch_shapes=[pltpu.VMEM((tm, w), BF16)],
        compiler_params=_params(1),
        name="sgu",
    )(x, g, w_uv, ln_g, ln_b, w_s, b_s, w_out)


def _rope_tables(seq):
    inv = 1.0 / (ROPE_THETA ** (jnp.arange(0, HEAD_DIM, 2, dtype=F32) / HEAD_DIM))
    ang = jnp.arange(seq, dtype=F32)[:, None] * inv[None, :]
    ang = jnp.concatenate([ang, ang, ang, ang], axis=-1)
    half = HEAD_DIM // 2
    sign = jnp.where((jnp.arange(LANES) % HEAD_DIM) < half, -1.0, 1.0).astype(F32)
    cos, sin = jnp.cos(ang), jnp.sin(ang) * sign
    q_scale = (HEAD_DIM ** -0.5) * math.log2(math.e)
    return cos * q_scale, sin * q_scale, cos, sin


def _flash_tiles(seq):
    group = 2
    tq = _tile(seq, 512)
    tk = _tile(seq, max(LANES, min(512, seq // (4 * group))))
    return tq, tk, group


def _tile(n, want):
    t = min(n, want)
    assert n % t == 0, (n, t)
    return t


def _trunk(x, p, w):
    b, seq, d = x.shape
    n = b * seq
    x = x.reshape(n, d)
    tm = _tile(seq, 512)
    qt, k, vt = _attn_pre(x, w["norm_mix_g"][0], w["attn_w_qkv"], _rope_tables(seq), seq, tm)
    lambda_init = 0.8 - 0.6 * math.exp(-0.3 * 0)
    o = _flash(qt, k.reshape(b, seq, d), vt,
               w["attn_lq1"], w["attn_lk1"], w["attn_lq2"], w["attn_lk2"], w["attn_subln_g"],
               lambda_init, *_flash_tiles(seq))
    x = _attn_post(x, o.reshape(n, d), w["attn_w_o"], tm)
    x = _ffn_ple(x, p[0].reshape(n, -1), w["norm_ffn_g"][0], w["ffn_w_up"][0], w["ffn_conv_w"][0],
                 w["ffn_conv_b"][0], w["ffn_w_down"][0], w["norm_ple_g"][0], w["ple_w_gate"][0],
                 w["ple_w_proj"][0], w["final_norm_g"], seq, tm, final=False)
    x = _sgu(x, w["norm_mix_g"][1], w["sgu_w_uv"], w["sgu_ln_g"], w["sgu_ln_b"], w["sgu_w_s"], w["sgu_b_s"],
             w["sgu_w_out"], tm)
    x = _ffn_ple(x, p[1].reshape(n, -1), w["norm_ffn_g"][1], w["ffn_w_up"][1], w["ffn_conv_w"][1],
                 w["ffn_conv_b"][1], w["ffn_w_down"][1], w["norm_ple_g"][1], w["ple_w_gate"][1],
                 w["ple_w_proj"][1], w["final_norm_g"], seq, tm, final=True)
    return x.reshape(b, seq, d)


def kernel(x_prompt, x_sample, p_prompt, p_sample, norm_mix_g, attn_w_qkv, attn_lq1, attn_lk1, attn_lq2, attn_lk2, attn_subln_g, attn_w_o, sgu_w_uv, sgu_ln_g, sgu_ln_b, sgu_w_s, sgu_b_s, sgu_w_out, norm_ffn_g, ffn_w_up, ffn_conv_w, ffn_conv_b, ffn_w_down, norm_ple_g, ple_w_gate, ple_w_proj, final_norm_g):
    depth = norm_mix_g.shape[0]
    assert depth == 2 and attn_w_qkv.shape[0] == 1 and sgu_w_uv.shape[0] == 1
    d = x_prompt.shape[-1]
    row = lambda a: a.reshape(a.shape[:-1] + (1, a.shape[-1]))
    w = dict(
        norm_mix_g=row(norm_mix_g), norm_ffn_g=row(norm_ffn_g), norm_ple_g=row(norm_ple_g),
        final_norm_g=final_norm_g.reshape(1, d),
        attn_w_qkv=attn_w_qkv[0].astype(BF16), attn_w_o=attn_w_o[0].astype(BF16),
        attn_lq1=attn_lq1, attn_lk1=attn_lk1, attn_lq2=attn_lq2, attn_lk2=attn_lk2, attn_subln_g=attn_subln_g,
        sgu_w_uv=sgu_w_uv[0].astype(BF16), sgu_ln_g=sgu_ln_g, sgu_ln_b=sgu_ln_b,
        sgu_w_s=sgu_w_s[0].astype(BF16), sgu_b_s=sgu_b_s[0][:, :, None], sgu_w_out=sgu_w_out[0].astype(BF16),
        ffn_w_up=ffn_w_up.astype(BF16), ffn_conv_w=ffn_conv_w, ffn_conv_b=row(ffn_conv_b),
        ffn_w_down=ffn_w_down.astype(BF16), ple_w_gate=ple_w_gate.astype(BF16), ple_w_proj=ple_w_proj.astype(BF16),
    )
    return _trunk(x_prompt, p_prompt, w), _trunk(x_sample, p_sample, w)
```

```python
import functools
import math

import jax
import jax.numpy as jnp
from jax import lax
from jax.experimental import pallas as pl
from jax.experimental.pallas import tpu as pltpu

F32 = jnp.float32
BF16 = jnp.bfloat16

EPS = 1e-6
ROPE_THETA = 10000.0
HEAD_DIM = 64
CHUNK = 128
CONV_WIDTH = 3
F32_ROWS = 8
BF16_ROWS = 16
HALO = F32_ROWS
LANES = 128
MXU_COLS = 256
ONES_ROWS = 16

VMEM_LIMIT = 56 * 1024 * 1024


def _rms(x, g):
    return x * lax.rsqrt(jnp.mean(x * x, axis=-1, keepdims=True) + EPS) * g


def _const_spec(shape):
    nd = len(shape)
    return pl.BlockSpec(shape, lambda *_: (0,) * nd, pipeline_mode=pl.Buffered(1))


def _params(n_axes, flags=None):
    return pltpu.CompilerParams(dimension_semantics=("arbitrary",) * n_axes,
                                vmem_limit_bytes=VMEM_LIMIT, flags=flags)


def _attn_pre_kernel(x_ref, g_ref, w_ref, cq_ref, sq_ref, ck_ref, sk_ref, qt_ref, k_ref, vt_ref):
    tm, d = x_ref.shape
    hd = 2 * HEAD_DIM
    h = _rms(x_ref[...], g_ref[...]).astype(BF16)
    qkv = jnp.dot(h, w_ref[...], preferred_element_type=F32)
    lane = lax.broadcasted_iota(jnp.int32, (tm, LANES), 1)
    first_half = (lane % HEAD_DIM) < (HEAD_DIM // 2)

    def rope(t, cos, sin):
        rot = jnp.where(first_half, pltpu.roll(t, LANES - HEAD_DIM // 2, 1), pltpu.roll(t, HEAD_DIM // 2, 1))
        return t * cos + rot * sin

    cq, sq, ck, sk = cq_ref[...], sq_ref[...], ck_ref[...], sk_ref[...]
    for j in range(d // hd):
        cols = slice(j * hd, (j + 1) * hd)
        qt_ref[j] = rope(qkv[:, j * hd:(j + 1) * hd], cq, sq).T.astype(BF16)
        k_ref[:, cols] = rope(qkv[:, d + j * hd:d + (j + 1) * hd], ck, sk).astype(BF16)
        vt_ref[j, 0:hd, :] = qkv[:, 2 * d + j * hd:2 * d + (j + 1) * hd].T.astype(BF16)
        vt_ref[j, hd:hd + ONES_ROWS, :] = jnp.ones((ONES_ROWS, tm), BF16)


def _attn_pre(x, g, w_qkv, tables, seq, tm):
    n, d = x.shape
    hd = 2 * HEAD_DIM
    heads = d // hd
    tps = seq // tm
    tab_spec = pl.BlockSpec((tm, LANES), lambda i: (i % tps, 0))
    return pl.pallas_call(
        _attn_pre_kernel,
        grid=(n // tm,),
        in_specs=[pl.BlockSpec((tm, d), lambda i: (i, 0)), _const_spec((1, d)), _const_spec((d, 3 * d)),
                  tab_spec, tab_spec, tab_spec, tab_spec],
        out_specs=[pl.BlockSpec((heads, hd, tm), lambda i: (0, 0, i)),
                   pl.BlockSpec((tm, d), lambda i: (i, 0)),
                   pl.BlockSpec((heads, hd + ONES_ROWS, tm), lambda i: (0, 0, i))],
        out_shape=[jax.ShapeDtypeStruct((heads, hd, n), BF16), jax.ShapeDtypeStruct((n, d), BF16),
                   jax.ShapeDtypeStruct((heads, hd + ONES_ROWS, n), BF16)],
        compiler_params=_params(1),
        name="attn_pre",
    )(x, g, w_qkv, *tables)


def _flash_kernel(qt_ref, k_ref, vt_ref, lq1_ref, lk1_ref, lq2_ref, lk2_ref, g_ref, o_ref,
                  q2t_s, m_s, acc_s, *bufs, tk, group, lambda_init):
    hd, tq = qt_ref.shape[1], qt_ref.shape[2]
    seq = k_ref.shape[1]
    qt = qt_ref[0]
    row = lax.broadcasted_iota(jnp.int32, qt.shape, 0)
    zero = jnp.zeros_like(qt)
    q2t_s[:, 0:tq] = jnp.where(row < HEAD_DIM, qt, zero)
    q2t_s[:, tq:2 * tq] = jnp.where(row >= HEAD_DIM, qt, zero)
    m_s[...] = jnp.full(m_s.shape, -jnp.inf, F32)
    acc_s[...] = jnp.zeros(acc_s.shape, F32)
    slots = 2 * group
    s_bufs, p_bufs, a_bufs, x_bufs = (bufs[i * slots:(i + 1) * slots] for i in range(4))
    n_groups = seq // (tk * group)
    slabs_per_piece = MXU_COLS // LANES
    vrows = vt_ref.shape[1]

    def work(values=None, softmax_half=None, scores=None):
        for g in range(group):
            if softmax_half is not None:
                slot_m = softmax_half * group + g
                m_old = m_s[...]
                m_new = jnp.maximum(m_old, x_bufs[slot_m][...])
                a_bufs[slot_m][...] = jnp.exp2(m_old - m_new)
                m_s[...] = m_new
            if values is not None:
                slot_v = values[1] * group + g
                start_v = pl.multiple_of((values[0] * group + g) * tk, tk)
            if scores is not None:
                slot_s = scores[1] * group + g
                start_s = pl.multiple_of((scores[0] * group + g) * tk, tk)
            for c in range(2 * tq // MXU_COLS):
                cols = slice(c * MXU_COLS, (c + 1) * MXU_COLS)
                slabs = range(c * slabs_per_piece, (c + 1) * slabs_per_piece)
                if values is not None:
                    p = jnp.concatenate([p_bufs[slot_v][j, 0:tk] for j in slabs], axis=1)
                    pv = jnp.dot(vt_ref[0, :, pl.ds(start_v, tk)], p, preferred_element_type=F32)
                    for i, j in enumerate(slabs):
                        alpha = a_bufs[slot_v][:, j * LANES:(j + 1) * LANES]
                        acc_s[j, 0:vrows] = alpha * acc_s[j, 0:vrows] + pv[:, i * LANES:(i + 1) * LANES]
                if softmax_half is not None:
                    for j in slabs:
                        lcols = slice(j * LANES, (j + 1) * LANES)
                        p_bufs[slot_m][j, 0:tk] = jnp.exp2(s_bufs[slot_m][j, 0:tk] - m_new[:, lcols]).astype(BF16)
                if scores is not None:
                    s = jnp.dot(k_ref[0, pl.ds(start_s, tk), :], q2t_s[:, cols], preferred_element_type=F32)
                    for i, j in enumerate(slabs):
                        s_bufs[slot_s][j, 0:tk] = s[:, i * LANES:(i + 1) * LANES]
                    x_bufs[slot_s][:, cols] = jnp.max(s, axis=0, keepdims=True)

    def step(b, half):
        work(values=(b, half), softmax_half=1 - half, scores=(b + 2, half))

    work(scores=(0, 0))
    work(softmax_half=0, scores=(1, 1))

    def body(b, carry):
        for half in range(2):
            pl.when(b % 2 == half)(functools.partial(step, b, half))
        return carry

    lax.fori_loop(0, n_groups - 2, body, 0)
    work(values=(n_groups - 2, n_groups % 2), softmax_half=(n_groups - 1) % 2)
    work(values=(n_groups - 1, (n_groups - 1) % 2))

    acc = jnp.concatenate([acc_s[j, 0:vrows] for j in range(2 * tq // LANES)], axis=1)
    o_maps = acc[0:hd] / acc[hd:hd + 1]
    lam = (jnp.exp(jnp.sum(lq1_ref[...] * lk1_ref[...], axis=-1, keepdims=True))
           - jnp.exp(jnp.sum(lq2_ref[...] * lk2_ref[...], axis=-1, keepdims=True)) + lambda_init)
    o = (o_maps[:, 0:tq] - lam * o_maps[:, tq:2 * tq]).T
    o_ref[0] = (_rms(o, g_ref[...]) * (1.0 - lambda_init)).astype(o_ref.dtype)


def _flash(qt, k, vt, lq1, lk1, lq2, lk2, subln_g, lambda_init, tq, tk, group):
    b, seq, d = k.shape
    heads, hd, _ = qt.shape
    vrows = vt.shape[1]
    qtiles = seq // tq
    slots = 2 * group
    assert seq % (tk * group) == 0 and seq // (tk * group) >= 2, (seq, tk, group)
    small = _const_spec((1, HEAD_DIM))
    return pl.pallas_call(
        functools.partial(_flash_kernel, tk=tk, group=group, lambda_init=lambda_init),
        grid=(b, heads, qtiles),
        in_specs=[pl.BlockSpec((1, hd, tq), lambda bi, h, qi: (h, 0, bi * qtiles + qi)),
                  pl.BlockSpec((1, seq, hd), lambda bi, h, qi: (bi, 0, h)),
                  pl.BlockSpec((1, vrows, seq), lambda bi, h, qi: (h, 0, bi)),
                  small, small, small, small, _const_spec((1, hd))],
        out_specs=pl.BlockSpec((1, tq, hd), lambda bi, h, qi: (bi, qi, h)),
        out_shape=jax.ShapeDtypeStruct((b, seq, d), BF16),
        scratch_shapes=[pltpu.VMEM((hd, 2 * tq), BF16), pltpu.VMEM((1, 2 * tq), F32),
                        pltpu.VMEM((2 * tq // LANES, vrows + F32_ROWS, LANES), F32)]
        + [pltpu.VMEM((2 * tq // LANES, tk + F32_ROWS, LANES), F32)] * slots
        + [pltpu.VMEM((2 * tq // LANES, tk + BF16_ROWS, LANES), BF16)] * slots
        + [pltpu.VMEM((1, 2 * tq), F32)] * slots
        + [pltpu.VMEM((1, 2 * tq), F32)] * slots,
        compiler_params=_params(3),
        name="flash_diff_attn",
    )(qt, k, vt, lq1, lk1, lq2, lk2, subln_g)


def _attn_post_kernel(x_ref, o_ref, w_ref, y_ref):
    y_ref[...] = x_ref[...] + jnp.dot(o_ref[...], w_ref[...], preferred_element_type=F32)


def _attn_post(x, o, w_o, tm):
    n, d = x.shape
    row = pl.BlockSpec((tm, d), lambda i: (i, 0))
    return pl.pallas_call(
        _attn_post_kernel,
        grid=(n // tm,),
        in_specs=[row, row, _const_spec((d, d))],
        out_specs=row,
        out_shape=jax.ShapeDtypeStruct((n, d), F32),
        compiler_params=_params(1),
        name="attn_post",
    )(x, o, w_o)


def _ffn_ple_kernel(x_ref, xp_ref, xn_ref, p_ref, gf_ref, wup_ref, cw_ref, cb_ref, wdn_ref,
                    gp_ref, wg_ref, wp_ref, gfin_ref, y_ref, a_s, act_s, *, tiles_per_seq, fc, final):
    tm, d = x_ref.shape
    f = wdn_ref.shape[0]
    i = pl.program_id(0)
    keep_prev = jnp.where(i % tiles_per_seq == 0, 0.0, 1.0).astype(F32)
    keep_next = jnp.where(i % tiles_per_seq == tiles_per_seq - 1, 0.0, 1.0).astype(F32)
    x = x_ref[...]
    gf = gf_ref[...]
    h = jnp.concatenate([_rms(xp_ref[...], gf) * keep_prev, _rms(x, gf), _rms(xn_ref[...], gf) * keep_next], axis=0)
    a_s[...] = jnp.dot(h.astype(BF16), wup_ref[...], preferred_element_type=F32)

    def conv(col0):
        cols = pl.ds(col0, fc)
        c = cb_ref[:, cols]
        for t in range(CONV_WIDTH):
            c = c + a_s[pl.ds(HALO - 1 + t, tm), cols] * cw_ref[pl.ds(t, 1), cols]
        return c

    for j in range(f // fc):
        val = conv(j * fc)
        gate = conv(f + j * fc)
        act_s[:, pl.ds(j * fc, fc)] = (gate * jax.nn.sigmoid(gate) * val).astype(BF16)

    x = x + jnp.dot(act_s[...], wdn_ref[...], preferred_element_type=F32)
    gate = jax.nn.sigmoid(jnp.dot(_rms(x, gp_ref[...]).astype(BF16), wg_ref[...], preferred_element_type=F32))
    x = x + gate * jnp.dot(p_ref[...].astype(BF16), wp_ref[...], preferred_element_type=F32)
    if final:
        x = _rms(x, gfin_ref[...])
    y_ref[...] = x


def _ffn_ple(x, p, g_ffn, w_up, conv_w, conv_b, w_down, g_ple, w_gate, w_proj, g_final, seq, tm, final):
    n, d = x.shape
    f = w_down.shape[0]
    pd = p.shape[1]
    fc = 2 * LANES
    tiles_per_seq = seq // tm
    hb = tm // HALO
    last_blk = n // HALO - 1
    return pl.pallas_call(
        functools.partial(_ffn_ple_kernel, tiles_per_seq=tiles_per_seq, fc=fc, final=final),
        grid=(n // tm,),
        in_specs=[pl.BlockSpec((tm, d), lambda i: (i, 0)),
                  pl.BlockSpec((HALO, d), lambda i: (jnp.maximum(i * hb - 1, 0), 0)),
                  pl.BlockSpec((HALO, d), lambda i: (jnp.minimum((i + 1) * hb, last_blk), 0)),
                  pl.BlockSpec((tm, pd), lambda i: (i, 0)),
                  _const_spec((1, d)), _const_spec((d, 2 * f)), _const_spec((CONV_WIDTH, 2 * f)),
                  _const_spec((1, 2 * f)), _const_spec((f, d)), _const_spec((1, d)), _const_spec((d, d)),
                  _const_spec((pd, d)), _const_spec((1, d))],
        out_specs=pl.BlockSpec((tm, d), lambda i: (i, 0)),
        out_shape=jax.ShapeDtypeStruct((n, d), F32),
        scratch_shapes=[pltpu.VMEM((tm + 2 * HALO, 2 * f), F32), pltpu.VMEM((tm, f), BF16)],
        compiler_params=_params(1),
        name="ffn_ple_final" if final else "ffn_ple",
    )(x, x, x, p, g_ffn, w_up, conv_w, conv_b, w_down, g_ple, w_gate, w_proj, g_final)


def _sgu_kernel(x_ref, g_ref, wuv_ref, lng_ref, lnb_ref, ws_ref, bs_ref, wout_ref, y_ref, um_s):
    tm, d = x_ref.shape
    w = wout_ref.shape[0]
    x = x_ref[...]
    z = jnp.dot(_rms(x, g_ref[...]).astype(BF16), wuv_ref[...], preferred_element_type=F32)
    z = 0.5 * z * (1.0 + lax.erf(z * (1.0 / math.sqrt(2.0))))
    v = z[:, w:]
    vc = v - jnp.mean(v, axis=-1, keepdims=True)
    v = vc * lax.rsqrt(jnp.mean(vc * vc, axis=-1, keepdims=True) + EPS) * lng_ref[...] + lnb_ref[...]
    vb = v.astype(BF16)
    for c in range(tm // CHUNK):
        rows = slice(c * CHUNK, (c + 1) * CHUNK)
        for grp in range(w // CHUNK):
            cols = slice(grp * CHUNK, (grp + 1) * CHUNK)
            mixed = jnp.dot(ws_ref[grp], vb[rows, cols], preferred_element_type=F32) + bs_ref[grp]
            um_s[rows, cols] = (z[rows, cols] * mixed).astype(BF16)
    y_ref[...] = x + jnp.dot(um_s[...], wout_ref[...], preferred_element_type=F32)


def _sgu(x, g, w_uv, ln_g, ln_b, w_s, b_s, w_out, tm):
    n, d = x.shape
    w = w_out.shape[0]
    ng = w // CHUNK
    row = pl.BlockSpec((tm, d), lambda i: (i, 0))
    return pl.pallas_call(
        _sgu_kernel,
        grid=(n // tm,),
        in_specs=[row, _const_spec((1, d)), _const_spec((d, 2 * w)), _const_spec((1, w)), _const_spec((1, w)),
                  _const_spec((ng, CHUNK, CHUNK)), _const_spec((ng, CHUNK, 1)), _const_spec((w, d))],
        out_specs=row,
        out_shape=jax.ShapeDtypeStruct((n, d), F32),
        scratch_shapes=[pltpu.VMEM((tm, w), BF16)],
        compiler_params=_params(1),
        name="sgu",
    )(x, g, w_uv, ln_g, ln_b, w_s, b_s, w_out)


def _rope_tables(seq):
    inv = 1.0 / (ROPE_THETA ** (jnp.arange(0, HEAD_DIM, 2, dtype=F32) / HEAD_DIM))
    ang = jnp.arange(seq, dtype=F32)[:, None] * inv[None, :]
    ang = jnp.concatenate([ang, ang, ang, ang], axis=-1)
    half = HEAD_DIM // 2
    sign = jnp.where((jnp.arange(LANES) % HEAD_DIM) < half, -1.0, 1.0).astype(F32)
    cos, sin = jnp.cos(ang), jnp.sin(ang) * sign
    q_scale = (HEAD_DIM ** -0.5) * math.log2(math.e)
    return cos * q_scale, sin * q_scale, cos, sin


def _flash_tiles(seq):
    group = 2
    tq = _tile(seq, 512)
    tk = _tile(seq, max(LANES, min(512, seq // (4 * group))))
    return tq, tk, group


def _tile(n, want):
    t = min(n, want)
    assert n % t == 0, (n, t)
    return t


def _trunk(x, p, w):
    b, seq, d = x.shape
    n = b * seq
    x = x.reshape(n, d)
    tm = _tile(seq, 512)
    qt, k, vt = _attn_pre(x, w["norm_mix_g"][0], w["attn_w_qkv"], _rope_tables(seq), seq, tm)
    lambda_init = 0.8 - 0.6 * math.exp(-0.3 * 0)
    o = _flash(qt, k.reshape(b, seq, d), vt,
               w["attn_lq1"], w["attn_lk1"], w["attn_lq2"], w["attn_lk2"], w["attn_subln_g"],
               lambda_init, *_flash_tiles(seq))
    x = _attn_post(x, o.reshape(n, d), w["attn_w_o"], tm)
    x = _ffn_ple(x, p[0].reshape(n, -1), w["norm_ffn_g"][0], w["ffn_w_up"][0], w["ffn_conv_w"][0],
                 w["ffn_conv_b"][0], w["ffn_w_down"][0], w["norm_ple_g"][0], w["ple_w_gate"][0],
                 w["ple_w_proj"][0], w["final_norm_g"], seq, tm, final=False)
    x = _sgu(x, w["norm_mix_g"][1], w["sgu_w_uv"], w["sgu_ln_g"], w["sgu_ln_b"], w["sgu_w_s"], w["sgu_b_s"],
             w["sgu_w_out"], tm)
    x = _ffn_ple(x, p[1].reshape(n, -1), w["norm_ffn_g"][1], w["ffn_w_up"][1], w["ffn_conv_w"][1],
                 w["ffn_conv_b"][1], w["ffn_w_down"][1], w["norm_ple_g"][1], w["ple_w_gate"][1],
                 w["ple_w_proj"][1], w["final_norm_g"], seq, tm, final=True)
    return x.reshape(b, seq, d)


def kernel(x_prompt, x_sample, p_prompt, p_sample, norm_mix_g, attn_w_qkv, attn_lq1, attn_lk1, attn_lq2, attn_lk2, attn_subln_g, attn_w_o, sgu_w_uv, sgu_ln_g, sgu_ln_b, sgu_w_s, sgu_b_s, sgu_w_out, norm_ffn_g, ffn_w_up, ffn_conv_w, ffn_conv_b, ffn_w_down, norm_ple_g, ple_w_gate, ple_w_proj, final_norm_g):
    depth = norm_mix_g.shape[0]
    assert depth == 2 and attn_w_qkv.shape[0] == 1 and sgu_w_uv.shape[0] == 1
    d = x_prompt.shape[-1]
    row = lambda a: a.reshape(a.shape[:-1] + (1, a.shape[-1]))
    w = dict(
        norm_mix_g=row(norm_mix_g), norm_ffn_g=row(norm_ffn_g), norm_ple_g=row(norm_ple_g),
        final_norm_g=final_norm_g.reshape(1, d),
        attn_w_qkv=attn_w_qkv[0].astype(BF16), attn_w_o=attn_w_o[0].astype(BF16),
        attn_lq1=attn_lq1, attn_lk1=attn_lk1, attn_lq2=attn_lq2, attn_lk2=attn_lk2, attn_subln_g=attn_subln_g,
        sgu_w_uv=sgu_w_uv[0].astype(BF16), sgu_ln_g=sgu_ln_g, sgu_ln_b=sgu_ln_b,
        sgu_w_s=sgu_w_s[0].astype(BF16), sgu_b_s=sgu_b_s[0][:, :, None], sgu_w_out=sgu_w_out[0].astype(BF16),
        ffn_w_up=ffn_w_up.astype(BF16), ffn_conv_w=ffn_conv_w, ffn_conv_b=row(ffn_conv_b),
        ffn_w_down=ffn_w_down.astype(BF16), ple_w_gate=ple_w_gate.astype(BF16), ple_w_proj=ple_w_proj.astype(BF16),
    )
    return _trunk(x_prompt, p_prompt, w), _trunk(x_sample, p_sample, w)
```

```python
import functools
import math

import jax
import jax.numpy as jnp
from jax import lax
from jax.experimental import pallas as pl
from jax.experimental.pallas import tpu as pltpu

F32 = jnp.float32
BF16 = jnp.bfloat16

EPS = 1e-6
ROPE_THETA = 10000.0
HEAD_DIM = 64
CHUNK = 128
CONV_WIDTH = 3
F32_ROWS = 8
BF16_ROWS = 16
HALO = F32_ROWS
LANES = 128
MXU_COLS = 256
ONES_ROWS = 16

VMEM_LIMIT = 56 * 1024 * 1024


def _rms(x, g):
    return x * lax.rsqrt(jnp.mean(x * x, axis=-1, keepdims=True) + EPS) * g


def _const_spec(shape):
    nd = len(shape)
    return pl.BlockSpec(shape, lambda *_: (0,) * nd, pipeline_mode=pl.Buffered(1))


def _params(n_axes, flags=None):
    return pltpu.CompilerParams(dimension_semantics=("arbitrary",) * n_axes,
                                vmem_limit_bytes=VMEM_LIMIT, flags=flags)


def _attn_pre_kernel(x_ref, g_ref, w_ref, cq_ref, sq_ref, ck_ref, sk_ref, qt_ref, k_ref, vt_ref):
    tm, d = x_ref.shape
    hd = 2 * HEAD_DIM
    h = _rms(x_ref[...], g_ref[...]).astype(BF16)
    qkv = jnp.dot(h, w_ref[...], preferred_element_type=F32)
    lane = lax.broadcasted_iota(jnp.int32, (tm, LANES), 1)
    first_half = (lane % HEAD_DIM) < (HEAD_DIM // 2)

    def rope(t, cos, sin):
        rot = jnp.where(first_half, pltpu.roll(t, LANES - HEAD_DIM // 2, 1), pltpu.roll(t, HEAD_DIM // 2, 1))
        return t * cos + rot * sin

    cq, sq, ck, sk = cq_ref[...], sq_ref[...], ck_ref[...], sk_ref[...]
    for j in range(d // hd):
        cols = slice(j * hd, (j + 1) * hd)
        qt_ref[j] = rope(qkv[:, j * hd:(j + 1) * hd], cq, sq).T.astype(BF16)
        k_ref[:, cols] = rope(qkv[:, d + j * hd:d + (j + 1) * hd], ck, sk).astype(BF16)
        vt_ref[j, 0:hd, :] = qkv[:, 2 * d + j * hd:2 * d + (j + 1) * hd].T.astype(BF16)
        vt_ref[j, hd:hd + ONES_ROWS, :] = jnp.ones((ONES_ROWS, tm), BF16)


def _attn_pre(x, g, w_qkv, tables, seq, tm):
    n, d = x.shape
    hd = 2 * HEAD_DIM
    heads = d // hd
    tps = seq // tm
    tab_spec = pl.BlockSpec((tm, LANES), lambda i: (i % tps, 0))
    return pl.pallas_call(
        _attn_pre_kernel,
        grid=(n // tm,),
        in_specs=[pl.BlockSpec((tm, d), lambda i: (i, 0)), _const_spec((1, d)), _const_spec((d, 3 * d)),
                  tab_spec, tab_spec, tab_spec, tab_spec],
        out_specs=[pl.BlockSpec((heads, hd, tm), lambda i: (0, 0, i)),
                   pl.BlockSpec((tm, d), lambda i: (i, 0)),
                   pl.BlockSpec((heads, hd + ONES_ROWS, tm), lambda i: (0, 0, i))],
        out_shape=[jax.ShapeDtypeStruct((heads, hd, n), BF16), jax.ShapeDtypeStruct((n, d), BF16),
                   jax.ShapeDtypeStruct((heads, hd + ONES_ROWS, n), BF16)],
        compiler_params=_params(1),
        name="attn_pre",
    )(x, g, w_qkv, *tables)


def _flash_kernel(qt_ref, k_ref, vt_ref, lq1_ref, lk1_ref, lq2_ref, lk2_ref, g_ref, o_ref,
                  q2t_s, m_s, acc_s, *bufs, tk, group, lambda_init):
    hd, tq = qt_ref.shape[1], qt_ref.shape[2]
    seq = k_ref.shape[1]
    qt = qt_ref[0]
    row = lax.broadcasted_iota(jnp.int32, qt.shape, 0)
    zero = jnp.zeros_like(qt)
    q2t_s[:, 0:tq] = jnp.where(row < HEAD_DIM, qt, zero)
    q2t_s[:, tq:2 * tq] = jnp.where(row >= HEAD_DIM, qt, zero)
    m_s[...] = jnp.full(m_s.shape, -jnp.inf, F32)
    acc_s[...] = jnp.zeros(acc_s.shape, F32)
    slots = 2 * group
    s_bufs, p_bufs, a_bufs, x_bufs = (bufs[i * slots:(i + 1) * slots] for i in range(4))
    n_groups = seq // (tk * group)
    slabs_per_piece = MXU_COLS // LANES
    vrows = vt_ref.shape[1]

    def work(values=None, softmax_half=None, scores=None):
        for g in range(group):
            if softmax_half is not None:
                slot_m = softmax_half * group + g
                m_old = m_s[...]
                m_new = jnp.maximum(m_old, x_bufs[slot_m][...])
                a_bufs[slot_m][...] = jnp.exp2(m_old - m_new)
                m_s[...] = m_new
            if values is not None:
                slot_v = values[1] * group + g
                start_v = pl.multiple_of((values[0] * group + g) * tk, tk)
            if scores is not None:
                slot_s = scores[1] * group + g
                start_s = pl.multiple_of((scores[0] * group + g) * tk, tk)
            for c in range(2 * tq // MXU_COLS):
                cols = slice(c * MXU_COLS, (c + 1) * MXU_COLS)
                slabs = range(c * slabs_per_piece, (c + 1) * slabs_per_piece)
                if values is not None:
                    p = jnp.concatenate([p_bufs[slot_v][j, 0:tk] for j in slabs], axis=1)
                    pv = jnp.dot(vt_ref[0, :, pl.ds(start_v, tk)], p, preferred_element_type=F32)
                    for i, j in enumerate(slabs):
                        alpha = a_bufs[slot_v][:, j * LANES:(j + 1) * LANES]
                        acc_s[j, 0:vrows] = alpha * acc_s[j, 0:vrows] + pv[:, i * LANES:(i + 1) * LANES]
                if softmax_half is not None:
                    for j in slabs:
                        lcols = slice(j * LANES, (j + 1) * LANES)
                        p_bufs[slot_m][j, 0:tk] = jnp.exp2(s_bufs[slot_m][j, 0:tk] - m_new[:, lcols]).astype(BF16)
                if scores is not None:
                    s = jnp.dot(k_ref[0, pl.ds(start_s, tk), :], q2t_s[:, cols], preferred_element_type=F32)
                    for i, j in enumerate(slabs):
                        s_bufs[slot_s][j, 0:tk] = s[:, i * LANES:(i + 1) * LANES]
                    x_bufs[slot_s][:, cols] = jnp.max(s, axis=0, keepdims=True)

    def step(b, half):
        work(values=(b, half), softmax_half=1 - half, scores=(b + 2, half))

    work(scores=(0, 0))
    work(softmax_half=0, scores=(1, 1))

    def body(b, carry):
        for half in range(2):
            pl.when(b % 2 == half)(functools.partial(step, b, half))
        return carry

    lax.fori_loop(0, n_groups - 2, body, 0)
    work(values=(n_groups - 2, n_groups % 2), softmax_half=(n_groups - 1) % 2)
    work(values=(n_groups - 1, (n_groups - 1) % 2))

    acc = jnp.concatenate([acc_s[j, 0:vrows] for j in range(2 * tq // LANES)], axis=1)
    o_maps = acc[0:hd] / acc[hd:hd + 1]
    lam = (jnp.exp(jnp.sum(lq1_ref[...] * lk1_ref[...], axis=-1, keepdims=True))
           - jnp.exp(jnp.sum(lq2_ref[...] * lk2_ref[...], axis=-1, keepdims=True)) + lambda_init)
    o = (o_maps[:, 0:tq] - lam * o_maps[:, tq:2 * tq]).T
    o_ref[0] = (_rms(o, g_ref[...]) * (1.0 - lambda_init)).astype(o_ref.dtype)


def _flash(qt, k, vt, lq1, lk1, lq2, lk2, subln_g, lambda_init, tq, tk, group):
    b, seq, d = k.shape
    heads, hd, _ = qt.shape
    vrows = vt.shape[1]
    qtiles = seq // tq
    slots = 2 * group
    assert seq % (tk * group) == 0 and seq // (tk * group) >= 2, (seq, tk, group)
    small = _const_spec((1, HEAD_DIM))
    return pl.pallas_call(
        functools.partial(_flash_kernel, tk=tk, group=group, lambda_init=lambda_init),
        grid=(b, heads, qtiles),
        in_specs=[pl.BlockSpec((1, hd, tq), lambda bi, h, qi: (h, 0, bi * qtiles + qi)),
                  pl.BlockSpec((1, seq, hd), lambda bi, h, qi: (bi, 0, h)),
                  pl.BlockSpec((1, vrows, seq), lambda bi, h, qi: (h, 0, bi)),
                  small, small, small, small, _const_spec((1, hd))],
        out_specs=pl.BlockSpec((1, tq, hd), lambda bi, h, qi: (bi, qi, h)),
        out_shape=jax.ShapeDtypeStruct((b, seq, d), BF16),
        scratch_shapes=[pltpu.VMEM((hd, 2 * tq), BF16), pltpu.VMEM((1, 2 * tq), F32),
                        pltpu.VMEM((2 * tq // LANES, vrows + F32_ROWS, LANES), F32)]
        + [pltpu.VMEM((2 * tq // LANES, tk + F32_ROWS, LANES), F32)] * slots
        + [pltpu.VMEM((2 * tq // LANES, tk + BF16_ROWS, LANES), BF16)] * slots
        + [pltpu.VMEM((1, 2 * tq), F32)] * slots
        + [pltpu.VMEM((1, 2 * tq), F32)] * slots,
        compiler_params=_params(3),
        name="flash_diff_attn",
    )(qt, k, vt, lq1, lk1, lq2, lk2, subln_g)


def _attn_post_kernel(x_ref, o_ref, w_ref, y_ref):
    y_ref[...] = x_ref[...] + jnp.dot(o_ref[...], w_ref[...], preferred_element_type=F32)


def _attn_post(x, o, w_o, tm):
    n, d = x.shape
    row = pl.BlockSpec((tm, d), lambda i: (i, 0))
    return pl.pallas_call(
        _attn_post_kernel,
        grid=(n // tm,),
        in_specs=[row, row, _const_spec((d, d))],
        out_specs=row,
        out_shape=jax.ShapeDtypeStruct((n, d), F32),
        compiler_params=_params(1),
        name="attn_post",
    )(x, o, w_o)


def _ffn_ple_kernel(x_ref, xp_ref, xn_ref, p_ref, gf_ref, wup_ref, cw_ref, cb_ref, wdn_ref,
                    gp_ref, wg_ref, wp_ref, gfin_ref, y_ref, a_s, act_s, *, tiles_per_seq, fc, final):
    tm, d = x_ref.shape
    f = wdn_ref.shape[0]
    i = pl.program_id(0)
    keep_prev = jnp.where(i % tiles_per_seq == 0, 0.0, 1.0).astype(F32)
    keep_next = jnp.where(i % tiles_per_seq == tiles_per_seq - 1, 0.0, 1.0).astype(F32)
    x = x_ref[...]
    gf = gf_ref[...]
    h = jnp.concatenate([_rms(xp_ref[...], gf) * keep_prev, _rms(x, gf), _rms(xn_ref[...], gf) * keep_next], axis=0)
    a_s[...] = jnp.dot(h.astype(BF16), wup_ref[...], preferred_element_type=F32)

    def conv(col0):
        cols = pl.ds(col0, fc)
        c = cb_ref[:, cols]
        for t in range(CONV_WIDTH):
            c = c + a_s[pl.ds(HALO - 1 + t, tm), cols] * cw_ref[pl.ds(t, 1), cols]
        return c

    for j in range(f // fc):
        val = conv(j * fc)
        gate = conv(f + j * fc)
        act_s[:, pl.ds(j * fc, fc)] = (gate * jax.nn.sigmoid(gate) * val).astype(BF16)

    x = x + jnp.dot(act_s[...], wdn_ref[...], preferred_element_type=F32)
    gate = jax.nn.sigmoid(jnp.dot(_rms(x, gp_ref[...]).astype(BF16), wg_ref[...], preferred_element_type=F32))
    x = x + gate * jnp.dot(p_ref[...].astype(BF16), wp_ref[...], preferred_element_type=F32)
    if final:
        x = _rms(x, gfin_ref[...])
    y_ref[...] = x


def _ffn_ple(x, p, g_ffn, w_up, conv_w, conv_b, w_down, g_ple, w_gate, w_proj, g_final, seq, tm, final):
    n, d = x.shape
    f = w_down.shape[0]
    pd = p.shape[1]
    fc = 2 * LANES
    tiles_per_seq = seq // tm
    hb = tm // HALO
    last_blk = n // HALO - 1
    return pl.pallas_call(
        functools.partial(_ffn_ple_kernel, tiles_per_seq=tiles_per_seq, fc=fc, final=final),
        grid=(n // tm,),
        in_specs=[pl.BlockSpec((tm, d), lambda i: (i, 0)),
                  pl.BlockSpec((HALO, d), lambda i: (jnp.maximum(i * hb - 1, 0), 0)),
                  pl.BlockSpec((HALO, d), lambda i: (jnp.minimum((i + 1) * hb, last_blk), 0)),
                  pl.BlockSpec((tm, pd), lambda i: (i, 0)),
                  _const_spec((1, d)), _const_spec((d, 2 * f)), _const_spec((CONV_WIDTH, 2 * f)),
                  _const_spec((1, 2 * f)), _const_spec((f, d)), _const_spec((1, d)), _const_spec((d, d)),
                  _const_spec((pd, d)), _const_spec((1, d))],
        out_specs=pl.BlockSpec((tm, d), lambda i: (i, 0)),
        out_shape=jax.ShapeDtypeStruct((n, d), F32),
        scratch_shapes=[pltpu.VMEM((tm + 2 * HALO, 2 * f), F32), pltpu.VMEM((tm, f), BF16)],
        compiler_params=_params(1),
        name="ffn_ple_final" if final else "ffn_ple",
    )(x, x, x, p, g_ffn, w_up, conv_w, conv_b, w_down, g_ple, w_gate, w_proj, g_final)


def _sgu_kernel(x_ref, g_ref, wuv_ref, lng_ref, lnb_ref, ws_ref, bs_ref, wout_ref, y_ref, um_s):
    tm, d = x_ref.shape
    w = wout_ref.shape[0]
    x = x_ref[...]
    z = jnp.dot(_rms(x, g_ref[...]).astype(BF16), wuv_ref[...], preferred_element_type=F32)
    z = 0.5 * z * (1.0 + lax.erf(z * (1.0 / math.sqrt(2.0))))
    v = z[:, w:]
    vc = v - jnp.mean(v, axis=-1, keepdims=True)
    v = vc * lax.rsqrt(jnp.mean(vc * vc, axis=-1, keepdims=True) + EPS) * lng_ref[...] + lnb_ref[...]
    vb = v.astype(BF16)
    for c in range(tm // CHUNK):
        rows = slice(c * CHUNK, (c + 1) * CHUNK)
        for grp in range(w // CHUNK):
            cols = slice(grp * CHUNK, (grp + 1) * CHUNK)
            mixed = jnp.dot(ws_ref[grp], vb[rows, cols], preferred_element_type=F32) + bs_ref[grp]
            um_s[rows, cols] = (z[rows, cols] * mixed).astype(BF16)
    y_ref[...] = x + jnp.dot(um_s[...], wout_ref[...], preferred_element_type=F32)


def _sgu(x, g, w_uv, ln_g, ln_b, w_s, b_s, w_out, tm):
    n, d = x.shape
    w = w_out.shape[0]
    ng = w // CHUNK
    row = pl.BlockSpec((tm, d), lambda i: (i, 0))
    return pl.pallas_call(
        _sgu_kernel,
        grid=(n // tm,),
        in_specs=[row, _const_spec((1, d)), _const_spec((d, 2 * w)), _const_spec((1, w)), _const_spec((1, w)),
                  _const_spec((ng, CHUNK, CHUNK)), _const_spec((ng, CHUNK, 1)), _const_spec((w, d))],
        out_specs=row,
        out_shape=jax.ShapeDtypeStruct((n, d), F32),
        scratch_shapes=[pltpu.VMEM((tm, w), BF16)],
        compiler_params=_params(1),
        name="sgu",
    )(x, g, w_uv, ln_g, ln_b, w_s, b_s, w_out)


def _rope_tables(seq):
    inv = 1.0 / (ROPE_THETA ** (jnp.arange(0, HEAD_DIM, 2, dtype=F32) / HEAD_DIM))
    ang = jnp.arange(seq, dtype=F32)[:, None] * inv[None, :]
    ang = jnp.concatenate([ang, ang, ang, ang], axis=-1)
    half = HEAD_DIM // 2
    sign = jnp.where((jnp.arange(LANES) % HEAD_DIM) < half, -1.0, 1.0).astype(F32)
    cos, sin = jnp.cos(ang), jnp.sin(ang) * sign
    q_scale = (HEAD_DIM ** -0.5) * math.log2(math.e)
    return cos * q_scale, sin * q_scale, cos, sin


def _flash_tiles(seq):
    group = 2
    tq = _tile(seq, 2048)
    tk = _tile(seq, max(LANES, min(256, seq // (4 * group))))
    return tq, tk, group


def _tile(n, want):
    t = min(n, want)
    assert n % t == 0, (n, t)
    return t


def _trunk(x, p, w):
    b, seq, d = x.shape
    n = b * seq
    x = x.reshape(n, d)
    tm = _tile(seq, 512)
    qt, k, vt = _attn_pre(x, w["norm_mix_g"][0], w["attn_w_qkv"], _rope_tables(seq), seq, tm)
    lambda_init = 0.8 - 0.6 * math.exp(-0.3 * 0)
    o = _flash(qt, k.reshape(b, seq, d), vt,
               w["attn_lq1"], w["attn_lk1"], w["attn_lq2"], w["attn_lk2"], w["attn_subln_g"],
               lambda_init, *_flash_tiles(seq))
    x = _attn_post(x, o.reshape(n, d), w["attn_w_o"], tm)
    x = _ffn_ple(x, p[0].reshape(n, -1), w["norm_ffn_g"][0], w["ffn_w_up"][0], w["ffn_conv_w"][0],
                 w["ffn_conv_b"][0], w["ffn_w_down"][0], w["norm_ple_g"][0], w["ple_w_gate"][0],
                 w["ple_w_proj"][0], w["final_norm_g"], seq, tm, final=False)
    x = _sgu(x, w["norm_mix_g"][1], w["sgu_w_uv"], w["sgu_ln_g"], w["sgu_ln_b"], w["sgu_w_s"], w["sgu_b_s"],
             w["sgu_w_out"], tm)
    x = _ffn_ple(x, p[1].reshape(n, -1), w["norm_ffn_g"][1], w["ffn_w_up"][1], w["ffn_conv_w"][1],
                 w["ffn_conv_b"][1], w["ffn_w_down"][1], w["norm_ple_g"][1], w["ple_w_gate"][1],
                 w["ple_w_proj"][1], w["final_norm_g"], seq, tm, final=True)
    return x.reshape(b, seq, d)


def kernel(x_prompt, x_sample, p_prompt, p_sample, norm_mix_g, attn_w_qkv, attn_lq1, attn_lk1, attn_lq2, attn_lk2, attn_subln_g, attn_w_o, sgu_w_uv, sgu_ln_g, sgu_ln_b, sgu_w_s, sgu_b_s, sgu_w_out, norm_ffn_g, ffn_w_up, ffn_conv_w, ffn_conv_b, ffn_w_down, norm_ple_g, ple_w_gate, ple_w_proj, final_norm_g):
    depth = norm_mix_g.shape[0]
    assert depth == 2 and attn_w_qkv.shape[0] == 1 and sgu_w_uv.shape[0] == 1
    d = x_prompt.shape[-1]
    row = lambda a: a.reshape(a.shape[:-1] + (1, a.shape[-1]))
    w = dict(
        norm_mix_g=row(norm_mix_g), norm_ffn_g=row(norm_ffn_g), norm_ple_g=row(norm_ple_g),
        final_norm_g=final_norm_g.reshape(1, d),
        attn_w_qkv=attn_w_qkv[0].astype(BF16), attn_w_o=attn_w_o[0].astype(BF16),
        attn_lq1=attn_lq1, attn_lk1=attn_lk1, attn_lq2=attn_lq2, attn_lk2=attn_lk2, attn_subln_g=attn_subln_g,
        sgu_w_uv=sgu_w_uv[0].astype(BF16), sgu_ln_g=sgu_ln_g, sgu_ln_b=sgu_ln_b,
        sgu_w_s=sgu_w_s[0].astype(BF16), sgu_b_s=sgu_b_s[0][:, :, None], sgu_w_out=sgu_w_out[0].astype(BF16),
        ffn_w_up=ffn_w_up.astype(BF16), ffn_conv_w=ffn_conv_w, ffn_conv_b=row(ffn_conv_b),
        ffn_w_down=ffn_w_down.astype(BF16), ple_w_gate=ple_w_gate.astype(BF16), ple_w_proj=ple_w_proj.astype(BF16),
    )
    return _trunk(x_prompt, p_prompt, w), _trunk(x_sample, p_sample, w)
```

```python
import functools
import math

import jax
import jax.numpy as jnp
from jax import lax
from jax.experimental import pallas as pl
from jax.experimental.pallas import tpu as pltpu

F32 = jnp.float32
BF16 = jnp.bfloat16

EPS = 1e-6
ROPE_THETA = 10000.0
HEAD_DIM = 64
CHUNK = 128
CONV_WIDTH = 3
F32_ROWS = 8
BF16_ROWS = 16
HALO = F32_ROWS
LANES = 128
MXU_COLS = 256
ONES_ROWS = 16
Q_SCALE = HEAD_DIM ** -0.5 * math.log2(math.e)

VMEM_LIMIT = 56 * 1024 * 1024


def _rms(x, g):
    return x * lax.rsqrt(jnp.mean(x * x, axis=-1, keepdims=True) + EPS) * g


def _const_spec(shape):
    nd = len(shape)
    return pl.BlockSpec(shape, lambda *_: (0,) * nd, pipeline_mode=pl.Buffered(1))


def _params(n_axes, flags=None):
    return pltpu.CompilerParams(dimension_semantics=("arbitrary",) * n_axes,
                                vmem_limit_bytes=VMEM_LIMIT, flags=flags)


def _attn_pre_kernel(x_ref, g_ref, w_ref, cos_ref, sin_ref, qt_ref, k_ref, vt_ref):
    tm, d = x_ref.shape
    hd = 2 * HEAD_DIM
    h = _rms(x_ref[...], g_ref[...]).astype(BF16)
    qkv = jnp.dot(h, w_ref[...], preferred_element_type=F32)
    lane = lax.broadcasted_iota(jnp.int32, (tm, LANES), 1)
    first_half = (lane % HEAD_DIM) < (HEAD_DIM // 2)

    def rope(t, cos, sin):
        rot = jnp.where(first_half, pltpu.roll(t, LANES - HEAD_DIM // 2, 1), pltpu.roll(t, HEAD_DIM // 2, 1))
        return t * cos + rot * sin

    cos, sin = cos_ref[...], sin_ref[...]
    for j in range(d // hd):
        cols = slice(j * hd, (j + 1) * hd)
        qt_ref[j] = (rope(qkv[:, j * hd:(j + 1) * hd], cos, sin) * Q_SCALE).T.astype(BF16)
        k_ref[:, cols] = rope(qkv[:, d + j * hd:d + (j + 1) * hd], cos, sin).astype(BF16)
        vt_ref[j, 0:hd, :] = qkv[:, 2 * d + j * hd:2 * d + (j + 1) * hd].T.astype(BF16)
        vt_ref[j, hd:hd + ONES_ROWS, :] = jnp.ones((ONES_ROWS, tm), BF16)


def _attn_pre(x, g, w_qkv, tables, seq, tm):
    n, d = x.shape
    hd = 2 * HEAD_DIM
    heads = d // hd
    tps = seq // tm
    tab_spec = pl.BlockSpec((tm, LANES), lambda i: (i % tps, 0))
    return pl.pallas_call(
        _attn_pre_kernel,
        grid=(n // tm,),
        in_specs=[pl.BlockSpec((tm, d), lambda i: (i, 0)), _const_spec((1, d)), _const_spec((d, 3 * d)),
                  tab_spec, tab_spec],
        out_specs=[pl.BlockSpec((heads, hd, tm), lambda i: (0, 0, i)),
                   pl.BlockSpec((tm, d), lambda i: (i, 0)),
                   pl.BlockSpec((heads, hd + ONES_ROWS, tm), lambda i: (0, 0, i))],
        out_shape=[jax.ShapeDtypeStruct((heads, hd, n), BF16), jax.ShapeDtypeStruct((n, d), BF16),
                   jax.ShapeDtypeStruct((heads, hd + ONES_ROWS, n), BF16)],
        compiler_params=_params(1),
        name="attn_pre",
    )(x, g, w_qkv, *tables)


def _flash_kernel(qt_ref, k_ref, vt_ref, lq1_ref, lk1_ref, lq2_ref, lk2_ref, g_ref, o_ref,
                  q2t_s, m_s, acc_s, *bufs, tk, group, lambda_init):
    hd, tq = qt_ref.shape[1], qt_ref.shape[2]
    seq = k_ref.shape[1]
    qt = qt_ref[0]
    row = lax.broadcasted_iota(jnp.int32, qt.shape, 0)
    zero = jnp.zeros_like(qt)
    q2t_s[:, 0:tq] = jnp.where(row < HEAD_DIM, qt, zero)
    q2t_s[:, tq:2 * tq] = jnp.where(row >= HEAD_DIM, qt, zero)
    m_s[...] = jnp.full(m_s.shape, -jnp.inf, F32)
    acc_s[...] = jnp.zeros(acc_s.shape, F32)
    slots = 2 * group
    s_bufs, p_bufs, a_bufs, x_bufs = (bufs[i * slots:(i + 1) * slots] for i in range(4))
    n_groups = seq // (tk * group)
    slabs_per_piece = MXU_COLS // LANES
    vrows = vt_ref.shape[1]

    def work(values=None, softmax_half=None, scores=None):
        for g in range(group):
            if softmax_half is not None:
                slot_m = softmax_half * group + g
                m_old = m_s[...]
                m_new = jnp.maximum(m_old, x_bufs[slot_m][...])
                a_bufs[slot_m][...] = jnp.exp2(m_old - m_new)
                m_s[...] = m_new
            if values is not None:
                slot_v = values[1] * group + g
                start_v = pl.multiple_of((values[0] * group + g) * tk, tk)
            if scores is not None:
                slot_s = scores[1] * group + g
                start_s = pl.multiple_of((scores[0] * group + g) * tk, tk)
            for c in range(2 * tq // MXU_COLS):
                cols = slice(c * MXU_COLS, (c + 1) * MXU_COLS)
                slabs = range(c * slabs_per_piece, (c + 1) * slabs_per_piece)
                if values is not None:
                    p = jnp.concatenate([p_bufs[slot_v][j, 0:tk] for j in slabs], axis=1)
                    pv = jnp.dot(vt_ref[0, :, pl.ds(start_v, tk)], p, preferred_element_type=F32)
                    for i, j in enumerate(slabs):
                        alpha = a_bufs[slot_v][:, j * LANES:(j + 1) * LANES]
                        acc_s[j, 0:vrows] = alpha * acc_s[j, 0:vrows] + pv[:, i * LANES:(i + 1) * LANES]
                if softmax_half is not None:
                    for j in slabs:
                        lcols = slice(j * LANES, (j + 1) * LANES)
                        p_bufs[slot_m][j, 0:tk] = jnp.exp2(s_bufs[slot_m][j, 0:tk] - m_new[:, lcols]).astype(BF16)
                if scores is not None:
                    s = jnp.dot(k_ref[0, pl.ds(start_s, tk), :], q2t_s[:, cols], preferred_element_type=F32)
                    for i, j in enumerate(slabs):
                        s_bufs[slot_s][j, 0:tk] = s[:, i * LANES:(i + 1) * LANES]
                    x_bufs[slot_s][:, cols] = jnp.max(s, axis=0, keepdims=True)

    def step(b, half):
        work(values=(b, half), softmax_half=1 - half, scores=(b + 2, half))

    work(scores=(0, 0))
    work(softmax_half=0, scores=(1, 1))

    def body(b, carry):
        for half in range(2):
            pl.when(b % 2 == half)(functools.partial(step, b, half))
        return carry

    lax.fori_loop(0, n_groups - 2, body, 0)
    work(values=(n_groups - 2, n_groups % 2), softmax_half=(n_groups - 1) % 2)
    work(values=(n_groups - 1, (n_groups - 1) % 2))

    acc = jnp.concatenate([acc_s[j, 0:vrows] for j in range(2 * tq // LANES)], axis=1)
    o_maps = acc[0:hd] / acc[hd:hd + 1]
    lam = (jnp.exp(jnp.sum(lq1_ref[...] * lk1_ref[...], axis=-1, keepdims=True))
           - jnp.exp(jnp.sum(lq2_ref[...] * lk2_ref[...], axis=-1, keepdims=True)) + lambda_init)
    o = (o_maps[:, 0:tq] - lam * o_maps[:, tq:2 * tq]).T
    o_ref[0] = (_rms(o, g_ref[...]) * (1.0 - lambda_init)).astype(o_ref.dtype)


def _flash(qt, k, vt, lq1, lk1, lq2, lk2, subln_g, lambda_init, tq, tk, group):
    b, seq, d = k.shape
    heads, hd, _ = qt.shape
    vrows = vt.shape[1]
    qtiles = seq // tq
    slots = 2 * group
    assert seq % (tk * group) == 0 and seq // (tk * group) >= 2, (seq, tk, group)
    small = _const_spec((1, HEAD_DIM))
    return pl.pallas_call(
        functools.partial(_flash_kernel, tk=tk, group=group, lambda_init=lambda_init),
        grid=(b, heads, qtiles),
        in_specs=[pl.BlockSpec((1, hd, tq), lambda bi, h, qi: (h, 0, bi * qtiles + qi)),
                  pl.BlockSpec((1, seq, hd), lambda bi, h, qi: (bi, 0, h)),
                  pl.BlockSpec((1, vrows, seq), lambda bi, h, qi: (h, 0, bi)),
                  small, small, small, small, _const_spec((1, hd))],
        out_specs=pl.BlockSpec((1, tq, hd), lambda bi, h, qi: (bi, qi, h)),
        out_shape=jax.ShapeDtypeStruct((b, seq, d), BF16),
        scratch_shapes=[pltpu.VMEM((hd, 2 * tq), BF16), pltpu.VMEM((1, 2 * tq), F32),
                        pltpu.VMEM((2 * tq // LANES, vrows + F32_ROWS, LANES), F32)]
        + [pltpu.VMEM((2 * tq // LANES, tk + F32_ROWS, LANES), F32)] * slots
        + [pltpu.VMEM((2 * tq // LANES, tk + BF16_ROWS, LANES), BF16)] * slots
        + [pltpu.VMEM((1, 2 * tq), F32)] * slots
        + [pltpu.VMEM((1, 2 * tq), F32)] * slots,
        compiler_params=_params(3),
        name="flash_diff_attn",
    )(qt, k, vt, lq1, lk1, lq2, lk2, subln_g)


def _attn_post_kernel(x_ref, o_ref, w_ref, y_ref):
    y_ref[...] = x_ref[...] + jnp.dot(o_ref[...], w_ref[...], preferred_element_type=F32)


def _attn_post(x, o, w_o, tm):
    n, d = x.shape
    row = pl.BlockSpec((tm, d), lambda i: (i, 0))
    return pl.pallas_call(
        _attn_post_kernel,
        grid=(n // tm,),
        in_specs=[row, row, _const_spec((d, d))],
        out_specs=row,
        out_shape=jax.ShapeDtypeStruct((n, d), F32),
        compiler_params=_params(1),
        name="attn_post",
    )(x, o, w_o)


def _ffn_ple_kernel(x_ref, xp_ref, xn_ref, p_ref, gf_ref, wup_ref, cw_ref, cb_ref, wdn_ref,
                    gp_ref, wg_ref, wp_ref, gfin_ref, y_ref, a_s, act_s, *, tiles_per_seq, fc, final):
    tm, d = x_ref.shape
    f = wdn_ref.shape[0]
    i = pl.program_id(0)
    keep_prev = jnp.where(i % tiles_per_seq == 0, 0.0, 1.0).astype(F32)
    keep_next = jnp.where(i % tiles_per_seq == tiles_per_seq - 1, 0.0, 1.0).astype(F32)
    x = x_ref[...]
    gf = gf_ref[...]
    h = jnp.concatenate([_rms(xp_ref[...], gf) * keep_prev, _rms(x, gf), _rms(xn_ref[...], gf) * keep_next], axis=0)
    a_s[...] = jnp.dot(h.astype(BF16), wup_ref[...], preferred_element_type=F32)

    def conv(col0):
        cols = pl.ds(col0, fc)
        c = cb_ref[:, cols]
        for t in range(CONV_WIDTH):
            c = c + a_s[pl.ds(HALO - 1 + t, tm), cols] * cw_ref[pl.ds(t, 1), cols]
        return c

    for j in range(f // fc):
        val = conv(j * fc)
        gate = conv(f + j * fc)
        act_s[:, pl.ds(j * fc, fc)] = (gate * jax.nn.sigmoid(gate) * val).astype(BF16)

    x = x + jnp.dot(act_s[...], wdn_ref[...], preferred_element_type=F32)
    gate = jax.nn.sigmoid(jnp.dot(_rms(x, gp_ref[...]).astype(BF16), wg_ref[...], preferred_element_type=F32))
    x = x + gate * jnp.dot(p_ref[...].astype(BF16), wp_ref[...], preferred_element_type=F32)
    if final:
        x = _rms(x, gfin_ref[...])
    y_ref[...] = x


def _ffn_ple(x, p, g_ffn, w_up, conv_w, conv_b, w_down, g_ple, w_gate, w_proj, g_final, layer, seq, tm, final):
    n, d = x.shape
    f = w_down.shape[1]
    pd = p.shape[2]

    def layer_spec(*shape):
        return pl.BlockSpec((None,) + shape, lambda i: (layer,) + (0,) * len(shape), pipeline_mode=pl.Buffered(1))

    fc = 2 * LANES
    tiles_per_seq = seq // tm
    hb = tm // HALO
    last_blk = n // HALO - 1
    return pl.pallas_call(
        functools.partial(_ffn_ple_kernel, tiles_per_seq=tiles_per_seq, fc=fc, final=final),
        grid=(n // tm,),
        in_specs=[pl.BlockSpec((tm, d), lambda i: (i, 0)),
                  pl.BlockSpec((HALO, d), lambda i: (jnp.maximum(i * hb - 1, 0), 0)),
                  pl.BlockSpec((HALO, d), lambda i: (jnp.minimum((i + 1) * hb, last_blk), 0)),
                  pl.BlockSpec((None, tm, pd), lambda i: (layer, i, 0)),
                  layer_spec(1, d), layer_spec(d, 2 * f), layer_spec(CONV_WIDTH, 2 * f),
                  layer_spec(1, 2 * f), layer_spec(f, d), layer_spec(1, d), layer_spec(d, d),
                  layer_spec(pd, d), _const_spec((1, d))],
        out_specs=pl.BlockSpec((tm, d), lambda i: (i, 0)),
        out_shape=jax.ShapeDtypeStruct((n, d), F32),
        scratch_shapes=[pltpu.VMEM((tm + 2 * HALO, 2 * f), F32), pltpu.VMEM((tm, f), BF16)],
        compiler_params=_params(1),
        name="ffn_ple_final" if final else "ffn_ple",
    )(x, x, x, p, g_ffn, w_up, conv_w, conv_b, w_down, g_ple, w_gate, w_proj, g_final)


def _sgu_kernel(x_ref, g_ref, wuv_ref, lng_ref, lnb_ref, ws_ref, bs_ref, wout_ref, y_ref, um_s):
    tm, d = x_ref.shape
    w = wout_ref.shape[0]
    x = x_ref[...]
    z = jnp.dot(_rms(x, g_ref[...]).astype(BF16), wuv_ref[...], preferred_element_type=F32)
    z = 0.5 * z * (1.0 + lax.erf(z * (1.0 / math.sqrt(2.0))))
    v = z[:, w:]
    vc = v - jnp.mean(v, axis=-1, keepdims=True)
    v = vc * lax.rsqrt(jnp.mean(vc * vc, axis=-1, keepdims=True) + EPS) * lng_ref[...] + lnb_ref[...]
    vb = v.astype(BF16)
    for c in range(tm // CHUNK):
        rows = slice(c * CHUNK, (c + 1) * CHUNK)
        for grp in range(w // CHUNK):
            cols = slice(grp * CHUNK, (grp + 1) * CHUNK)
            mixed = jnp.dot(ws_ref[grp], vb[rows, cols], preferred_element_type=F32) + bs_ref[grp]
            um_s[rows, cols] = (z[rows, cols] * mixed).astype(BF16)
    y_ref[...] = x + jnp.dot(um_s[...], wout_ref[...], preferred_element_type=F32)


def _sgu(x, g, w_uv, ln_g, ln_b, w_s, b_s, w_out, tm):
    n, d = x.shape
    w = w_out.shape[0]
    ng = w // CHUNK
    row = pl.BlockSpec((tm, d), lambda i: (i, 0))
    return pl.pallas_call(
        _sgu_kernel,
        grid=(n // tm,),
        in_specs=[row, _const_spec((1, d)), _const_spec((d, 2 * w)), _const_spec((1, w)), _const_spec((1, w)),
                  _const_spec((ng, CHUNK, CHUNK)), _const_spec((ng, CHUNK, 1)), _const_spec((w, d))],
        out_specs=row,
        out_shape=jax.ShapeDtypeStruct((n, d), F32),
        scratch_shapes=[pltpu.VMEM((tm, w), BF16)],
        compiler_params=_params(1),
        name="sgu",
    )(x, g, w_uv, ln_g, ln_b, w_s, b_s, w_out)


def _rope_tables(seq):
    half = HEAD_DIM // 2
    inv = 1.0 / (ROPE_THETA ** (jnp.arange(0, HEAD_DIM, 2, dtype=F32) / HEAD_DIM))
    ang = jnp.arange(seq, dtype=F32)[:, None] * inv[None, :]
    cos, sin = jnp.cos(ang), jnp.sin(ang)
    sign = jnp.where((jnp.arange(LANES) % HEAD_DIM) < half, -1.0, 1.0).astype(F32)
    reps = LANES // half
    return jnp.concatenate([cos] * reps, axis=-1), jnp.concatenate([sin] * reps, axis=-1) * sign


def _flash_tiles(seq):
    group = 2
    tq = _tile(seq, 2048)
    tk = _tile(seq, max(LANES, min(256, seq // (4 * group))))
    return tq, tk, group


def _tile(n, want):
    t = min(n, want)
    assert n % t == 0, (n, t)
    return t


def _trunk(x, p, w):
    b, seq, d = x.shape
    n = b * seq
    x = x.reshape(n, d)
    tm = _tile(seq, 512)
    qt, k, vt = _attn_pre(x, w["norm_mix_g"][0], w["attn_w_qkv"], _rope_tables(seq), seq, tm)
    lambda_init = 0.8 - 0.6 * math.exp(-0.3 * 0)
    o = _flash(qt, k.reshape(b, seq, d), vt,
               w["attn_lq1"], w["attn_lk1"], w["attn_lq2"], w["attn_lk2"], w["attn_subln_g"],
               lambda_init, *_flash_tiles(seq))
    x = _attn_post(x, o.reshape(n, d), w["attn_w_o"], tm)
    p = p.reshape(p.shape[0], n, p.shape[-1])
    ffn = functools.partial(_ffn_ple, p=p, g_ffn=w["norm_ffn_g"], w_up=w["ffn_w_up"], conv_w=w["ffn_conv_w"],
                            conv_b=w["ffn_conv_b"], w_down=w["ffn_w_down"], g_ple=w["norm_ple_g"],
                            w_gate=w["ple_w_gate"], w_proj=w["ple_w_proj"], g_final=w["final_norm_g"], seq=seq, tm=tm)
    x = ffn(x, layer=0, final=False)
    x = _sgu(x, w["norm_mix_g"][1], w["sgu_w_uv"], w["sgu_ln_g"], w["sgu_ln_b"], w["sgu_w_s"], w["sgu_b_s"],
             w["sgu_w_out"], tm)
    x = ffn(x, layer=1, final=True)
    return x.reshape(b, seq, d)


def kernel(x_prompt, x_sample, p_prompt, p_sample, norm_mix_g, attn_w_qkv, attn_lq1, attn_lk1, attn_lq2, attn_lk2, attn_subln_g, attn_w_o, sgu_w_uv, sgu_ln_g, sgu_ln_b, sgu_w_s, sgu_b_s, sgu_w_out, norm_ffn_g, ffn_w_up, ffn_conv_w, ffn_conv_b, ffn_w_down, norm_ple_g, ple_w_gate, ple_w_proj, final_norm_g):
    depth = norm_mix_g.shape[0]
    assert depth == 2 and attn_w_qkv.shape[0] == 1 and sgu_w_uv.shape[0] == 1
    d = x_prompt.shape[-1]
    row = lambda a: a.reshape(a.shape[:-1] + (1, a.shape[-1]))
    w = dict(
        norm_mix_g=row(norm_mix_g), norm_ffn_g=row(norm_ffn_g), norm_ple_g=row(norm_ple_g),
        final_norm_g=final_norm_g.reshape(1, d),
        attn_w_qkv=attn_w_qkv[0].astype(BF16), attn_w_o=attn_w_o[0].astype(BF16),
        attn_lq1=attn_lq1, attn_lk1=attn_lk1, attn_lq2=attn_lq2, attn_lk2=attn_lk2, attn_subln_g=attn_subln_g,
        sgu_w_uv=sgu_w_uv[0].astype(BF16), sgu_ln_g=sgu_ln_g, sgu_ln_b=sgu_ln_b,
        sgu_w_s=sgu_w_s[0].astype(BF16), sgu_b_s=sgu_b_s[0][:, :, None], sgu_w_out=sgu_w_out[0].astype(BF16),
        ffn_w_up=ffn_w_up.astype(BF16), ffn_conv_w=ffn_conv_w, ffn_conv_b=row(ffn_conv_b),
        ffn_w_down=ffn_w_down.astype(BF16), ple_w_gate=ple_w_gate.astype(BF16), ple_w_proj=ple_w_proj.astype(BF16),
    )
    return _trunk(x_prompt, p_prompt, w), _trunk(x_sample, p_sample, w)
```

```python
import functools
import math

import jax
import jax.numpy as jnp
from jax import lax
from jax.experimental import pallas as pl
from jax.experimental.pallas import tpu as pltpu

F32 = jnp.float32
BF16 = jnp.bfloat16

EPS = 1e-6
ROPE_THETA = 10000.0
HEAD_DIM = 64
CHUNK = 128
CONV_WIDTH = 3
F32_ROWS = 8
BF16_ROWS = 16
HALO = F32_ROWS
LANES = 128
MXU_COLS = 256
ONES_ROWS = 16
Q_SCALE = HEAD_DIM ** -0.5 * math.log2(math.e)

VMEM_LIMIT = 56 * 1024 * 1024


def _rms(x, g):
    return x * lax.rsqrt(jnp.mean(x * x, axis=-1, keepdims=True) + EPS) * g


def _const_spec(shape):
    nd = len(shape)
    return pl.BlockSpec(shape, lambda *_: (0,) * nd, pipeline_mode=pl.Buffered(1))


def _params(n_axes):
    return pltpu.CompilerParams(dimension_semantics=("arbitrary",) * n_axes, vmem_limit_bytes=VMEM_LIMIT)


def _attn_pre_kernel(x_ref, g_ref, w_ref, cos_ref, sin_ref, qt_ref, k_ref, vt_ref):
    tm, d = x_ref.shape
    hd = 2 * HEAD_DIM
    h = _rms(x_ref[...], g_ref[...]).astype(BF16)
    qkv = jnp.dot(h, w_ref[...], preferred_element_type=F32)
    lane = lax.broadcasted_iota(jnp.int32, (tm, LANES), 1)
    first_half = (lane % HEAD_DIM) < (HEAD_DIM // 2)

    def rope(t, cos, sin):
        rot = jnp.where(first_half, pltpu.roll(t, LANES - HEAD_DIM // 2, 1), pltpu.roll(t, HEAD_DIM // 2, 1))
        return t * cos + rot * sin

    cos, sin = cos_ref[...], sin_ref[...]
    for j in range(d // hd):
        cols = slice(j * hd, (j + 1) * hd)
        qt_ref[j] = (rope(qkv[:, j * hd:(j + 1) * hd], cos, sin) * Q_SCALE).T.astype(BF16)
        k_ref[:, cols] = rope(qkv[:, d + j * hd:d + (j + 1) * hd], cos, sin).astype(BF16)
        vt_ref[j, 0:hd, :] = qkv[:, 2 * d + j * hd:2 * d + (j + 1) * hd].T.astype(BF16)
        vt_ref[j, hd:hd + ONES_ROWS, :] = jnp.ones((ONES_ROWS, tm), BF16)


def _attn_pre(x, g, w_qkv, tables, seq, tm):
    n, d = x.shape
    hd = 2 * HEAD_DIM
    heads = d // hd
    tps = seq // tm
    tab_spec = pl.BlockSpec((tm, LANES), lambda i: (i % tps, 0))
    return pl.pallas_call(
        _attn_pre_kernel,
        grid=(n // tm,),
        in_specs=[pl.BlockSpec((tm, d), lambda i: (i, 0)), _const_spec((1, d)), _const_spec((d, 3 * d)),
                  tab_spec, tab_spec],
        out_specs=[pl.BlockSpec((heads, hd, tm), lambda i: (0, 0, i)),
                   pl.BlockSpec((tm, d), lambda i: (i, 0)),
                   pl.BlockSpec((heads, hd + ONES_ROWS, tm), lambda i: (0, 0, i))],
        out_shape=[jax.ShapeDtypeStruct((heads, hd, n), BF16), jax.ShapeDtypeStruct((n, d), BF16),
                   jax.ShapeDtypeStruct((heads, hd + ONES_ROWS, n), BF16)],
        compiler_params=_params(1),
        name="attn_pre",
    )(x, g, w_qkv, *tables)


def _flash_kernel(qt_ref, k_ref, vt_ref, lq1_ref, lk1_ref, lq2_ref, lk2_ref, g_ref, o_ref,
                  q2t_s, m_s, acc_s, *bufs, tk, group, lambda_init):
    hd, tq = qt_ref.shape[1], qt_ref.shape[2]
    seq = k_ref.shape[1]
    qt = qt_ref[0]
    row = lax.broadcasted_iota(jnp.int32, qt.shape, 0)
    zero = jnp.zeros_like(qt)
    q2t_s[:, 0:tq] = jnp.where(row < HEAD_DIM, qt, zero)
    q2t_s[:, tq:2 * tq] = jnp.where(row >= HEAD_DIM, qt, zero)
    m_s[...] = jnp.full(m_s.shape, -jnp.inf, F32)
    acc_s[...] = jnp.zeros(acc_s.shape, F32)
    slots = 2 * group
    s_bufs, p_bufs, x_bufs, a_bufs = (bufs[i * slots:(i + 1) * slots] for i in range(4))
    n_groups = seq // (tk * group)
    slabs_per_piece = MXU_COLS // LANES
    vrows = vt_ref.shape[1]

    def work(values=None, softmax_half=None, scores=None):
        if softmax_half is not None:
            slots_m = range(softmax_half * group, (softmax_half + 1) * group)
            m_old = m_s[...]
            m_new = functools.reduce(jnp.maximum, [x_bufs[slot][...] for slot in slots_m], m_old)
            a_bufs[softmax_half][...] = jnp.exp2(m_old - m_new)
            m_s[...] = m_new
        if values is not None:
            slots_v = range(values[1] * group, (values[1] + 1) * group)
            start_v = pl.multiple_of(values[0] * group * tk, group * tk)
        for c in range(2 * tq // MXU_COLS):
            cols = slice(c * MXU_COLS, (c + 1) * MXU_COLS)
            slabs = range(c * slabs_per_piece, (c + 1) * slabs_per_piece)
            if values is not None:
                p = jnp.concatenate([jnp.concatenate([p_bufs[slot][j, 0:tk] for slot in slots_v], axis=0)
                                     for j in slabs], axis=1)
                pv = jnp.dot(vt_ref[0, :, pl.ds(start_v, group * tk)], p, preferred_element_type=F32)
                for i, j in enumerate(slabs):
                    alpha = a_bufs[values[1]][:, j * LANES:(j + 1) * LANES]
                    acc_s[j, 0:vrows] = alpha * acc_s[j, 0:vrows] + pv[:, i * LANES:(i + 1) * LANES]
            for g in range(group):
                if softmax_half is not None:
                    slot_m = softmax_half * group + g
                    for j in slabs:
                        lcols = slice(j * LANES, (j + 1) * LANES)
                        p_bufs[slot_m][j, 0:tk] = jnp.exp2(s_bufs[slot_m][j, 0:tk] - m_new[:, lcols]).astype(BF16)
                if scores is not None:
                    slot_s = scores[1] * group + g
                    start_s = pl.multiple_of((scores[0] * group + g) * tk, tk)
                    s = jnp.dot(k_ref[0, pl.ds(start_s, tk), :], q2t_s[:, cols], preferred_element_type=F32)
                    for i, j in enumerate(slabs):
                        s_bufs[slot_s][j, 0:tk] = s[:, i * LANES:(i + 1) * LANES]
                    x_bufs[slot_s][:, cols] = jnp.max(s, axis=0, keepdims=True)

    def step(b, half):
        work(values=(b, half), softmax_half=1 - half, scores=(b + 2, half))

    work(scores=(0, 0))
    work(softmax_half=0, scores=(1, 1))

    def body(b, carry):
        for half in range(2):
            pl.when(b % 2 == half)(functools.partial(step, b, half))
        return carry

    lax.fori_loop(0, n_groups - 2, body, 0)
    work(values=(n_groups - 2, n_groups % 2), softmax_half=(n_groups - 1) % 2)
    work(values=(n_groups - 1, (n_groups - 1) % 2))

    acc = jnp.concatenate([acc_s[j, 0:vrows] for j in range(2 * tq // LANES)], axis=1)
    o_maps = acc[0:hd] / acc[hd:hd + 1]
    lam = (jnp.exp(jnp.sum(lq1_ref[...] * lk1_ref[...], axis=-1, keepdims=True))
           - jnp.exp(jnp.sum(lq2_ref[...] * lk2_ref[...], axis=-1, keepdims=True)) + lambda_init)
    o = (o_maps[:, 0:tq] - lam * o_maps[:, tq:2 * tq]).T
    o_ref[0] = (_rms(o, g_ref[...]) * (1.0 - lambda_init)).astype(o_ref.dtype)


def _flash(qt, k, vt, lq1, lk1, lq2, lk2, subln_g, lambda_init, tq, tk, group):
    b, seq, d = k.shape
    heads, hd, _ = qt.shape
    vrows = vt.shape[1]
    qtiles = seq // tq
    slots = 2 * group
    assert seq % (tk * group) == 0 and seq // (tk * group) >= 2, (seq, tk, group)
    small = _const_spec((1, HEAD_DIM))
    return pl.pallas_call(
        functools.partial(_flash_kernel, tk=tk, group=group, lambda_init=lambda_init),
        grid=(b, heads, qtiles),
        in_specs=[pl.BlockSpec((1, hd, tq), lambda bi, h, qi: (h, 0, bi * qtiles + qi)),
                  pl.BlockSpec((1, seq, hd), lambda bi, h, qi: (bi, 0, h)),
                  pl.BlockSpec((1, vrows, seq), lambda bi, h, qi: (h, 0, bi)),
                  small, small, small, small, _const_spec((1, hd))],
        out_specs=pl.BlockSpec((1, tq, hd), lambda bi, h, qi: (bi, qi, h)),
        out_shape=jax.ShapeDtypeStruct((b, seq, d), BF16),
        scratch_shapes=[pltpu.VMEM((hd, 2 * tq), BF16), pltpu.VMEM((1, 2 * tq), F32),
                        pltpu.VMEM((2 * tq // LANES, vrows + F32_ROWS, LANES), F32)]
        + [pltpu.VMEM((2 * tq // LANES, tk + F32_ROWS, LANES), F32)] * slots
        + [pltpu.VMEM((2 * tq // LANES, tk + BF16_ROWS, LANES), BF16)] * slots
        + [pltpu.VMEM((1, 2 * tq), F32)] * slots
        + [pltpu.VMEM((1, 2 * tq), F32)] * 2,
        compiler_params=_params(3),
        name="flash_diff_attn",
    )(qt, k, vt, lq1, lk1, lq2, lk2, subln_g)


def _attn_post_kernel(x_ref, o_ref, w_ref, y_ref):
    y_ref[...] = x_ref[...] + jnp.dot(o_ref[...], w_ref[...], preferred_element_type=F32)


def _attn_post(x, o, w_o, tm):
    n, d = x.shape
    row = pl.BlockSpec((tm, d), lambda i: (i, 0))
    return pl.pallas_call(
        _attn_post_kernel,
        grid=(n // tm,),
        in_specs=[row, row, _const_spec((d, d))],
        out_specs=row,
        out_shape=jax.ShapeDtypeStruct((n, d), F32),
        compiler_params=_params(1),
        name="attn_post",
    )(x, o, w_o)


def _ffn_ple_kernel(x_ref, xp_ref, xn_ref, p_ref, gf_ref, wup_ref, cw_ref, cb_ref, wdn_ref,
                    gp_ref, wg_ref, wp_ref, gfin_ref, y_ref, a_s, act_s, *, tiles_per_seq, fc, final):
    tm, d = x_ref.shape
    f = wdn_ref.shape[0]
    i = pl.program_id(0)
    keep_prev = jnp.where(i % tiles_per_seq == 0, 0.0, 1.0).astype(F32)
    keep_next = jnp.where(i % tiles_per_seq == tiles_per_seq - 1, 0.0, 1.0).astype(F32)
    x = x_ref[...]
    gf = gf_ref[...]
    h = jnp.concatenate([_rms(xp_ref[...], gf) * keep_prev, _rms(x, gf), _rms(xn_ref[...], gf) * keep_next], axis=0)
    a_s[...] = jnp.dot(h.astype(BF16), wup_ref[...], preferred_element_type=F32)

    def conv(col0):
        cols = pl.ds(col0, fc)
        c = cb_ref[:, cols]
        for t in range(CONV_WIDTH):
            c = c + a_s[pl.ds(HALO - 1 + t, tm), cols] * cw_ref[pl.ds(t, 1), cols]
        return c

    for j in range(f // fc):
        val = conv(j * fc)
        gate = conv(f + j * fc)
        act_s[:, pl.ds(j * fc, fc)] = (gate * jax.nn.sigmoid(gate) * val).astype(BF16)

    x = x + jnp.dot(act_s[...], wdn_ref[...], preferred_element_type=F32)
    gate = jax.nn.sigmoid(jnp.dot(_rms(x, gp_ref[...]).astype(BF16), wg_ref[...], preferred_element_type=F32))
    x = x + gate * jnp.dot(p_ref[...].astype(BF16), wp_ref[...], preferred_element_type=F32)
    if final:
        x = _rms(x, gfin_ref[...])
    y_ref[...] = x


def _ffn_ple(x, p, g_ffn, w_up, conv_w, conv_b, w_down, g_ple, w_gate, w_proj, g_final, layer, seq, tm, final):
    n, d = x.shape
    f = w_down.shape[1]
    pd = p.shape[2]

    def layer_spec(*shape):
        return pl.BlockSpec((None,) + shape, lambda i: (layer,) + (0,) * len(shape), pipeline_mode=pl.Buffered(1))

    fc = 2 * LANES
    tiles_per_seq = seq // tm
    hb = tm // HALO
    last_blk = n // HALO - 1
    return pl.pallas_call(
        functools.partial(_ffn_ple_kernel, tiles_per_seq=tiles_per_seq, fc=fc, final=final),
        grid=(n // tm,),
        in_specs=[pl.BlockSpec((tm, d), lambda i: (i, 0)),
                  pl.BlockSpec((HALO, d), lambda i: (jnp.maximum(i * hb - 1, 0), 0)),
                  pl.BlockSpec((HALO, d), lambda i: (jnp.minimum((i + 1) * hb, last_blk), 0)),
                  pl.BlockSpec((None, tm, pd), lambda i: (layer, i, 0)),
                  layer_spec(1, d), layer_spec(d, 2 * f), layer_spec(CONV_WIDTH, 2 * f),
                  layer_spec(1, 2 * f), layer_spec(f, d), layer_spec(1, d), layer_spec(d, d),
                  layer_spec(pd, d), _const_spec((1, d))],
        out_specs=pl.BlockSpec((tm, d), lambda i: (i, 0)),
        out_shape=jax.ShapeDtypeStruct((n, d), F32),
        scratch_shapes=[pltpu.VMEM((tm + 2 * HALO, 2 * f), F32), pltpu.VMEM((tm, f), BF16)],
        compiler_params=_params(1),
        name="ffn_ple_final" if final else "ffn_ple",
    )(x, x, x, p, g_ffn, w_up, conv_w, conv_b, w_down, g_ple, w_gate, w_proj, g_final)


def _sgu_kernel(x_ref, g_ref, wuv_ref, lng_ref, lnb_ref, ws_ref, bs_ref, wout_ref, y_ref, um_s):
    tm, d = x_ref.shape
    w = wout_ref.shape[0]
    x = x_ref[...]
    z = jnp.dot(_rms(x, g_ref[...]).astype(BF16), wuv_ref[...], preferred_element_type=F32)
    z = 0.5 * z * (1.0 + lax.erf(z * (1.0 / math.sqrt(2.0))))
    v = z[:, w:]
    vc = v - jnp.mean(v, axis=-1, keepdims=True)
    v = vc * lax.rsqrt(jnp.mean(vc * vc, axis=-1, keepdims=True) + EPS) * lng_ref[...] + lnb_ref[...]
    vb = v.astype(BF16)
    for c in range(tm // CHUNK):
        rows = slice(c * CHUNK, (c + 1) * CHUNK)
        for grp in range(w // CHUNK):
            cols = slice(grp * CHUNK, (grp + 1) * CHUNK)
            mixed = jnp.dot(ws_ref[grp], vb[rows, cols], preferred_element_type=F32) + bs_ref[grp]
            um_s[rows, cols] = (z[rows, cols] * mixed).astype(BF16)
    y_ref[...] = x + jnp.dot(um_s[...], wout_ref[...], preferred_element_type=F32)


def _sgu(x, g, w_uv, ln_g, ln_b, w_s, b_s, w_out, tm):
    n, d = x.shape
    w = w_out.shape[0]
    ng = w // CHUNK
    row = pl.BlockSpec((tm, d), lambda i: (i, 0))
    return pl.pallas_call(
        _sgu_kernel,
        grid=(n // tm,),
        in_specs=[row, _const_spec((1, d)), _const_spec((d, 2 * w)), _const_spec((1, w)), _const_spec((1, w)),
                  _const_spec((ng, CHUNK, CHUNK)), _const_spec((ng, CHUNK, 1)), _const_spec((w, d))],
        out_specs=row,
        out_shape=jax.ShapeDtypeStruct((n, d), F32),
        scratch_shapes=[pltpu.VMEM((tm, w), BF16)],
        compiler_params=_params(1),
        name="sgu",
    )(x, g, w_uv, ln_g, ln_b, w_s, b_s, w_out)


def _rope_tables(seq):
    half = HEAD_DIM // 2
    inv = 1.0 / (ROPE_THETA ** (jnp.arange(0, HEAD_DIM, 2, dtype=F32) / HEAD_DIM))
    ang = jnp.arange(seq, dtype=F32)[:, None] * inv[None, :]
    cos, sin = jnp.cos(ang), jnp.sin(ang)
    sign = jnp.where((jnp.arange(LANES) % HEAD_DIM) < half, -1.0, 1.0).astype(F32)
    reps = LANES // half
    return jnp.concatenate([cos] * reps, axis=-1), jnp.concatenate([sin] * reps, axis=-1) * sign


def _flash_tiles(seq):
    group = 2
    tq = _tile(seq, 2048)
    tk = _tile(seq, max(LANES, min(256, seq // (4 * group))))
    return tq, tk, group


def _tile(n, want):
    t = min(n, want)
    assert n % t == 0, (n, t)
    return t


def _trunk(x, p, w):
    b, seq, d = x.shape
    n = b * seq
    x = x.reshape(n, d)
    tm = _tile(seq, 512)
    qt, k, vt = _attn_pre(x, w["norm_mix_g"][0], w["attn_w_qkv"], _rope_tables(seq), seq, tm)
    lambda_init = 0.8 - 0.6 * math.exp(-0.3 * 0)
    o = _flash(qt, k.reshape(b, seq, d), vt,
               w["attn_lq1"], w["attn_lk1"], w["attn_lq2"], w["attn_lk2"], w["attn_subln_g"],
               lambda_init, *_flash_tiles(seq))
    x = _attn_post(x, o.reshape(n, d), w["attn_w_o"], tm)
    p = p.reshape(p.shape[0], n, p.shape[-1])
    ffn = functools.partial(_ffn_ple, p=p, g_ffn=w["norm_ffn_g"], w_up=w["ffn_w_up"], conv_w=w["ffn_conv_w"],
                            conv_b=w["ffn_conv_b"], w_down=w["ffn_w_down"], g_ple=w["norm_ple_g"],
                            w_gate=w["ple_w_gate"], w_proj=w["ple_w_proj"], g_final=w["final_norm_g"], seq=seq, tm=tm)
    x = ffn(x, layer=0, final=False)
    x = _sgu(x, w["norm_mix_g"][1], w["sgu_w_uv"], w["sgu_ln_g"], w["sgu_ln_b"], w["sgu_w_s"], w["sgu_b_s"],
             w["sgu_w_out"], tm)
    x = ffn(x, layer=1, final=True)
    return x.reshape(b, seq, d)


def kernel(x_prompt, x_sample, p_prompt, p_sample, norm_mix_g, attn_w_qkv, attn_lq1, attn_lk1, attn_lq2, attn_lk2, attn_subln_g, attn_w_o, sgu_w_uv, sgu_ln_g, sgu_ln_b, sgu_w_s, sgu_b_s, sgu_w_out, norm_ffn_g, ffn_w_up, ffn_conv_w, ffn_conv_b, ffn_w_down, norm_ple_g, ple_w_gate, ple_w_proj, final_norm_g):
    depth = norm_mix_g.shape[0]
    assert depth == 2 and attn_w_qkv.shape[0] == 1 and sgu_w_uv.shape[0] == 1
    d = x_prompt.shape[-1]
    row = lambda a: a.reshape(a.shape[:-1] + (1, a.shape[-1]))
    w = dict(
        norm_mix_g=row(norm_mix_g), norm_ffn_g=row(norm_ffn_g), norm_ple_g=row(norm_ple_g),
        final_norm_g=final_norm_g.reshape(1, d),
        attn_w_qkv=attn_w_qkv[0].astype(BF16), attn_w_o=attn_w_o[0].astype(BF16),
        attn_lq1=attn_lq1, attn_lk1=attn_lk1, attn_lq2=attn_lq2, attn_lk2=attn_lk2, attn_subln_g=attn_subln_g,
        sgu_w_uv=sgu_w_uv[0].astype(BF16), sgu_ln_g=sgu_ln_g, sgu_ln_b=sgu_ln_b,
        sgu_w_s=sgu_w_s[0].astype(BF16), sgu_b_s=sgu_b_s[0][:, :, None], sgu_w_out=sgu_w_out[0].astype(BF16),
        ffn_w_up=ffn_w_up.astype(BF16), ffn_conv_w=ffn_conv_w, ffn_conv_b=row(ffn_conv_b),
        ffn_w_down=ffn_w_down.astype(BF16), ple_w_gate=ple_w_gate.astype(BF16), ple_w_proj=ple_w_proj.astype(BF16),
    )
    return _trunk(x_prompt, p_prompt, w), _trunk(x_sample, p_sample, w)
```

```python
import functools
import math

import jax
import jax.numpy as jnp
from jax import lax
from jax.experimental import pallas as pl
from jax.experimental.pallas import tpu as pltpu

F32 = jnp.float32
BF16 = jnp.bfloat16

EPS = 1e-6
ROPE_THETA = 10000.0
HEAD_DIM = 64
CHUNK = 128
CONV_WIDTH = 3
F32_ROWS = 8
BF16_ROWS = 16
HALO = F32_ROWS
LANES = 128
MXU_COLS = 256
PIECE_COLS = 2 * MXU_COLS
ONES_ROWS = 16
Q_SCALE = HEAD_DIM ** -0.5 * math.log2(math.e)

VMEM_LIMIT = 56 * 1024 * 1024


def _rms(x, g):
    return x * lax.rsqrt(jnp.mean(x * x, axis=-1, keepdims=True) + EPS) * g


def _const_spec(shape):
    nd = len(shape)
    return pl.BlockSpec(shape, lambda *_: (0,) * nd, pipeline_mode=pl.Buffered(1))


def _params(n_axes):
    return pltpu.CompilerParams(dimension_semantics=("arbitrary",) * n_axes, vmem_limit_bytes=VMEM_LIMIT)


def _attn_pre_kernel(x_ref, g_ref, w_ref, cos_ref, sin_ref, qt_ref, k_ref, vt_ref):
    tm, d = x_ref.shape
    hd = 2 * HEAD_DIM
    h = _rms(x_ref[...], g_ref[...]).astype(BF16)
    qkv = jnp.dot(h, w_ref[...], preferred_element_type=F32)
    lane = lax.broadcasted_iota(jnp.int32, (tm, LANES), 1)
    first_half = (lane % HEAD_DIM) < (HEAD_DIM // 2)

    def rope(t, cos, sin):
        rot = jnp.where(first_half, pltpu.roll(t, LANES - HEAD_DIM // 2, 1), pltpu.roll(t, HEAD_DIM // 2, 1))
        return t * cos + rot * sin

    cos, sin = cos_ref[...], sin_ref[...]
    for j in range(d // hd):
        cols = slice(j * hd, (j + 1) * hd)
        qt_ref[j] = (rope(qkv[:, j * hd:(j + 1) * hd], cos, sin) * Q_SCALE).T.astype(BF16)
        k_ref[:, cols] = rope(qkv[:, d + j * hd:d + (j + 1) * hd], cos, sin).astype(BF16)
        vt_ref[j, 0:hd, :] = qkv[:, 2 * d + j * hd:2 * d + (j + 1) * hd].T.astype(BF16)
        vt_ref[j, hd:hd + ONES_ROWS, :] = jnp.ones((ONES_ROWS, tm), BF16)


def _attn_pre(x, g, w_qkv, tables, seq, tm):
    n, d = x.shape
    hd = 2 * HEAD_DIM
    heads = d // hd
    tps = seq // tm
    tab_spec = pl.BlockSpec((tm, LANES), lambda i: (i % tps, 0))
    return pl.pallas_call(
        _attn_pre_kernel,
        grid=(n // tm,),
        in_specs=[pl.BlockSpec((tm, d), lambda i: (i, 0)), _const_spec((1, d)), _const_spec((d, 3 * d)),
                  tab_spec, tab_spec],
        out_specs=[pl.BlockSpec((heads, hd, tm), lambda i: (0, 0, i)),
                   pl.BlockSpec((tm, d), lambda i: (i, 0)),
                   pl.BlockSpec((heads, hd + ONES_ROWS, tm), lambda i: (0, 0, i))],
        out_shape=[jax.ShapeDtypeStruct((heads, hd, n), BF16), jax.ShapeDtypeStruct((n, d), BF16),
                   jax.ShapeDtypeStruct((heads, hd + ONES_ROWS, n), BF16)],
        compiler_params=_params(1),
        name="attn_pre",
    )(x, g, w_qkv, *tables)


def _flash_kernel(qt_ref, k_ref, vt_ref, lq1_ref, lk1_ref, lq2_ref, lk2_ref, g_ref, o_ref,
                  q2t_s, m_s, acc_s, *bufs, tk, group, lambda_init):
    hd, tq = qt_ref.shape[1], qt_ref.shape[2]
    seq = k_ref.shape[1]
    qt = qt_ref[0]
    row = lax.broadcasted_iota(jnp.int32, qt.shape, 0)
    zero = jnp.zeros_like(qt)
    q2t_s[:, 0:tq] = jnp.where(row < HEAD_DIM, qt, zero)
    q2t_s[:, tq:2 * tq] = jnp.where(row >= HEAD_DIM, qt, zero)
    m_s[...] = jnp.full(m_s.shape, -jnp.inf, F32)
    acc_s[...] = jnp.zeros(acc_s.shape, F32)
    slots = 2 * group
    s_bufs, p_bufs, x_bufs, a_bufs = (bufs[i * slots:(i + 1) * slots] for i in range(4))
    n_groups = seq // (tk * group)
    slabs_per_piece = PIECE_COLS // LANES
    vrows = vt_ref.shape[1]

    def work(values=None, softmax_half=None, scores=None):
        if softmax_half is not None:
            slots_m = range(softmax_half * group, (softmax_half + 1) * group)
            m_old = m_s[...]
            m_new = functools.reduce(jnp.maximum, [x_bufs[slot][...] for slot in slots_m], m_old)
            a_bufs[softmax_half][...] = jnp.exp2(m_old - m_new)
            m_s[...] = m_new
        if values is not None:
            slots_v = range(values[1] * group, (values[1] + 1) * group)
            start_v = pl.multiple_of(values[0] * group * tk, group * tk)
        for c in range(2 * tq // PIECE_COLS):
            cols = slice(c * PIECE_COLS, (c + 1) * PIECE_COLS)
            slabs = range(c * slabs_per_piece, (c + 1) * slabs_per_piece)
            if values is not None:
                p = jnp.concatenate([jnp.concatenate([p_bufs[slot][j, 0:tk] for slot in slots_v], axis=0)
                                     for j in slabs], axis=1)
                pv = jnp.dot(vt_ref[0, :, pl.ds(start_v, group * tk)], p, preferred_element_type=F32)
                for i, j in enumerate(slabs):
                    alpha = a_bufs[values[1]][:, j * LANES:(j + 1) * LANES]
                    acc_s[j, 0:vrows] = alpha * acc_s[j, 0:vrows] + pv[:, i * LANES:(i + 1) * LANES]
            for g in range(group):
                if softmax_half is not None:
                    slot_m = softmax_half * group + g
                    for j in slabs:
                        lcols = slice(j * LANES, (j + 1) * LANES)
                        p_bufs[slot_m][j, 0:tk] = jnp.exp2(s_bufs[slot_m][j, 0:tk] - m_new[:, lcols]).astype(BF16)
                if scores is not None:
                    slot_s = scores[1] * group + g
                    start_s = pl.multiple_of((scores[0] * group + g) * tk, tk)
                    s = jnp.dot(k_ref[0, pl.ds(start_s, tk), :], q2t_s[:, cols], preferred_element_type=F32)
                    for i, j in enumerate(slabs):
                        s_bufs[slot_s][j, 0:tk] = s[:, i * LANES:(i + 1) * LANES]
                    x_bufs[slot_s][:, cols] = jnp.max(s, axis=0, keepdims=True)

    def step(b, half):
        work(values=(b, half), softmax_half=1 - half, scores=(b + 2, half))

    work(scores=(0, 0))
    work(softmax_half=0, scores=(1, 1))

    def body(b, carry):
        for half in range(2):
            pl.when(b % 2 == half)(functools.partial(step, b, half))
        return carry

    lax.fori_loop(0, n_groups - 2, body, 0)
    work(values=(n_groups - 2, n_groups % 2), softmax_half=(n_groups - 1) % 2)
    work(values=(n_groups - 1, (n_groups - 1) % 2))

    acc = jnp.concatenate([acc_s[j, 0:vrows] for j in range(2 * tq // LANES)], axis=1)
    o_maps = acc[0:hd] / acc[hd:hd + 1]
    lam = (jnp.exp(jnp.sum(lq1_ref[...] * lk1_ref[...], axis=-1, keepdims=True))
           - jnp.exp(jnp.sum(lq2_ref[...] * lk2_ref[...], axis=-1, keepdims=True)) + lambda_init)
    o = (o_maps[:, 0:tq] - lam * o_maps[:, tq:2 * tq]).T
    o_ref[0] = (_rms(o, g_ref[...]) * (1.0 - lambda_init)).astype(o_ref.dtype)


def _flash(qt, k, vt, lq1, lk1, lq2, lk2, subln_g, lambda_init, tq, tk, group):
    b, seq, d = k.shape
    heads, hd, _ = qt.shape
    vrows = vt.shape[1]
    qtiles = seq // tq
    slots = 2 * group
    assert seq % (tk * group) == 0 and seq // (tk * group) >= 2, (seq, tk, group)
    small = _const_spec((1, HEAD_DIM))
    return pl.pallas_call(
        functools.partial(_flash_kernel, tk=tk, group=group, lambda_init=lambda_init),
        grid=(b, heads, qtiles),
        in_specs=[pl.BlockSpec((1, hd, tq), lambda bi, h, qi: (h, 0, bi * qtiles + qi)),
                  pl.BlockSpec((1, seq, hd), lambda bi, h, qi: (bi, 0, h)),
                  pl.BlockSpec((1, vrows, seq), lambda bi, h, qi: (h, 0, bi)),
                  small, small, small, small, _const_spec((1, hd))],
        out_specs=pl.BlockSpec((1, tq, hd), lambda bi, h, qi: (bi, qi, h)),
        out_shape=jax.ShapeDtypeStruct((b, seq, d), BF16),
        scratch_shapes=[pltpu.VMEM((hd, 2 * tq), BF16), pltpu.VMEM((1, 2 * tq), F32),
                        pltpu.VMEM((2 * tq // LANES, vrows + F32_ROWS, LANES), F32)]
        + [pltpu.VMEM((2 * tq // LANES, tk + F32_ROWS, LANES), F32)] * slots
        + [pltpu.VMEM((2 * tq // LANES, tk + BF16_ROWS, LANES), BF16)] * slots
        + [pltpu.VMEM((1, 2 * tq), F32)] * slots
        + [pltpu.VMEM((1, 2 * tq), F32)] * 2,
        compiler_params=_params(3),
        name="flash_diff_attn",
    )(qt, k, vt, lq1, lk1, lq2, lk2, subln_g)


def _attn_post_kernel(x_ref, o_ref, w_ref, y_ref):
    y_ref[...] = x_ref[...] + jnp.dot(o_ref[...], w_ref[...], preferred_element_type=F32)


def _attn_post(x, o, w_o, tm):
    n, d = x.shape
    row = pl.BlockSpec((tm, d), lambda i: (i, 0))
    return pl.pallas_call(
        _attn_post_kernel,
        grid=(n // tm,),
        in_specs=[row, row, _const_spec((d, d))],
        out_specs=row,
        out_shape=jax.ShapeDtypeStruct((n, d), F32),
        compiler_params=_params(1),
        name="attn_post",
    )(x, o, w_o)


def _ffn_ple_kernel(x_ref, xp_ref, xn_ref, p_ref, gf_ref, wup_ref, cw_ref, cb_ref, wdn_ref,
                    gp_ref, wg_ref, wp_ref, gfin_ref, y_ref, a_s, act_s, *, tiles_per_seq, fc, final):
    tm, d = x_ref.shape
    f = wdn_ref.shape[0]
    i = pl.program_id(0)
    keep_prev = jnp.where(i % tiles_per_seq == 0, 0.0, 1.0).astype(F32)
    keep_next = jnp.where(i % tiles_per_seq == tiles_per_seq - 1, 0.0, 1.0).astype(F32)
    x = x_ref[...]
    gf = gf_ref[...]
    h = jnp.concatenate([_rms(xp_ref[...], gf) * keep_prev, _rms(x, gf), _rms(xn_ref[...], gf) * keep_next], axis=0)
    a_s[...] = jnp.dot(h.astype(BF16), wup_ref[...], preferred_element_type=F32)

    def conv(col0):
        cols = pl.ds(col0, fc)
        c = cb_ref[:, cols]
        for t in range(CONV_WIDTH):
            c = c + a_s[pl.ds(HALO - 1 + t, tm), cols] * cw_ref[pl.ds(t, 1), cols]
        return c

    for j in range(f // fc):
        val = conv(j * fc)
        gate = conv(f + j * fc)
        act_s[:, pl.ds(j * fc, fc)] = (gate * jax.nn.sigmoid(gate) * val).astype(BF16)

    x = x + jnp.dot(act_s[...], wdn_ref[...], preferred_element_type=F32)
    gate = jax.nn.sigmoid(jnp.dot(_rms(x, gp_ref[...]).astype(BF16), wg_ref[...], preferred_element_type=F32))
    x = x + gate * jnp.dot(p_ref[...].astype(BF16), wp_ref[...], preferred_element_type=F32)
    if final:
        x = _rms(x, gfin_ref[...])
    y_ref[...] = x


def _ffn_ple(x, p, g_ffn, w_up, conv_w, conv_b, w_down, g_ple, w_gate, w_proj, g_final, layer, seq, tm, final):
    n, d = x.shape
    f = w_down.shape[1]
    pd = p.shape[2]

    def layer_spec(*shape):
        return pl.BlockSpec((None,) + shape, lambda i: (layer,) + (0,) * len(shape), pipeline_mode=pl.Buffered(1))

    fc = 2 * LANES
    tiles_per_seq = seq // tm
    hb = tm // HALO
    last_blk = n // HALO - 1
    return pl.pallas_call(
        functools.partial(_ffn_ple_kernel, tiles_per_seq=tiles_per_seq, fc=fc, final=final),
        grid=(n // tm,),
        in_specs=[pl.BlockSpec((tm, d), lambda i: (i, 0)),
                  pl.BlockSpec((HALO, d), lambda i: (jnp.maximum(i * hb - 1, 0), 0)),
                  pl.BlockSpec((HALO, d), lambda i: (jnp.minimum((i + 1) * hb, last_blk), 0)),
                  pl.BlockSpec((None, tm, pd), lambda i: (layer, i, 0)),
                  layer_spec(1, d), layer_spec(d, 2 * f), layer_spec(CONV_WIDTH, 2 * f),
                  layer_spec(1, 2 * f), layer_spec(f, d), layer_spec(1, d), layer_spec(d, d),
                  layer_spec(pd, d), _const_spec((1, d))],
        out_specs=pl.BlockSpec((tm, d), lambda i: (i, 0)),
        out_shape=jax.ShapeDtypeStruct((n, d), F32),
        scratch_shapes=[pltpu.VMEM((tm + 2 * HALO, 2 * f), F32), pltpu.VMEM((tm, f), BF16)],
        compiler_params=_params(1),
        name="ffn_ple_final" if final else "ffn_ple",
    )(x, x, x, p, g_ffn, w_up, conv_w, conv_b, w_down, g_ple, w_gate, w_proj, g_final)


def _sgu_kernel(x_ref, g_ref, wuv_ref, lng_ref, lnb_ref, ws_ref, bs_ref, wout_ref, y_ref, um_s):
    tm, d = x_ref.shape
    w = wout_ref.shape[0]
    x = x_ref[...]
    z = jnp.dot(_rms(x, g_ref[...]).astype(BF16), wuv_ref[...], preferred_element_type=F32)
    z = 0.5 * z * (1.0 + lax.erf(z * (1.0 / math.sqrt(2.0))))
    v = z[:, w:]
    vc = v - jnp.mean(v, axis=-1, keepdims=True)
    v = vc * lax.rsqrt(jnp.mean(vc * vc, axis=-1, keepdims=True) + EPS) * lng_ref[...] + lnb_ref[...]
    vb = v.astype(BF16)
    for c in range(tm // CHUNK):
        rows = slice(c * CHUNK, (c + 1) * CHUNK)
        for grp in range(w // CHUNK):
            cols = slice(grp * CHUNK, (grp + 1) * CHUNK)
            mixed = jnp.dot(ws_ref[grp], vb[rows, cols], preferred_element_type=F32) + bs_ref[grp]
            um_s[rows, cols] = (z[rows, cols] * mixed).astype(BF16)
    y_ref[...] = x + jnp.dot(um_s[...], wout_ref[...], preferred_element_type=F32)


def _sgu(x, g, w_uv, ln_g, ln_b, w_s, b_s, w_out, tm):
    n, d = x.shape
    w = w_out.shape[0]
    ng = w // CHUNK
    row = pl.BlockSpec((tm, d), lambda i: (i, 0))
    return pl.pallas_call(
        _sgu_kernel,
        grid=(n // tm,),
        in_specs=[row, _const_spec((1, d)), _const_spec((d, 2 * w)), _const_spec((1, w)), _const_spec((1, w)),
                  _const_spec((ng, CHUNK, CHUNK)), _const_spec((ng, CHUNK, 1)), _const_spec((w, d))],
        out_specs=row,
        out_shape=jax.ShapeDtypeStruct((n, d), F32),
        scratch_shapes=[pltpu.VMEM((tm, w), BF16)],
        compiler_params=_params(1),
        name="sgu",
    )(x, g, w_uv, ln_g, ln_b, w_s, b_s, w_out)


def _rope_tables(seq):
    half = HEAD_DIM // 2
    inv = 1.0 / (ROPE_THETA ** (jnp.arange(0, HEAD_DIM, 2, dtype=F32) / HEAD_DIM))
    ang = jnp.arange(seq, dtype=F32)[:, None] * inv[None, :]
    cos, sin = jnp.cos(ang), jnp.sin(ang)
    sign = jnp.where((jnp.arange(LANES) % HEAD_DIM) < half, -1.0, 1.0).astype(F32)
    reps = LANES // half
    return jnp.concatenate([cos] * reps, axis=-1), jnp.concatenate([sin] * reps, axis=-1) * sign


def _flash_tiles(seq):
    group = 1
    tq = _tile(seq, 2048)
    tk = _tile(seq, max(LANES, min(512, seq // (4 * group))))
    return tq, tk, group


def _tile(n, want):
    t = min(n, want)
    assert n % t == 0, (n, t)
    return t


def _trunk(x, p, w):
    b, seq, d = x.shape
    n = b * seq
    x = x.reshape(n, d)
    tm = _tile(seq, 512)
    qt, k, vt = _attn_pre(x, w["norm_mix_g"][0], w["attn_w_qkv"], _rope_tables(seq), seq, tm)
    lambda_init = 0.8 - 0.6 * math.exp(-0.3 * 0)
    o = _flash(qt, k.reshape(b, seq, d), vt,
               w["attn_lq1"], w["attn_lk1"], w["attn_lq2"], w["attn_lk2"], w["attn_subln_g"],
               lambda_init, *_flash_tiles(seq))
    x = _attn_post(x, o.reshape(n, d), w["attn_w_o"], tm)
    p = p.reshape(p.shape[0], n, p.shape[-1])
    ffn = functools.partial(_ffn_ple, p=p, g_ffn=w["norm_ffn_g"], w_up=w["ffn_w_up"], conv_w=w["ffn_conv_w"],
                            conv_b=w["ffn_conv_b"], w_down=w["ffn_w_down"], g_ple=w["norm_ple_g"],
                            w_gate=w["ple_w_gate"], w_proj=w["ple_w_proj"], g_final=w["final_norm_g"], seq=seq, tm=tm)
    x = ffn(x, layer=0, final=False)
    x = _sgu(x, w["norm_mix_g"][1], w["sgu_w_uv"], w["sgu_ln_g"], w["sgu_ln_b"], w["sgu_w_s"], w["sgu_b_s"],
             w["sgu_w_out"], tm)
    x = ffn(x, layer=1, final=True)
    return x.reshape(b, seq, d)


def kernel(x_prompt, x_sample, p_prompt, p_sample, norm_mix_g, attn_w_qkv, attn_lq1, attn_lk1, attn_lq2, attn_lk2, attn_subln_g, attn_w_o, sgu_w_uv, sgu_ln_g, sgu_ln_b, sgu_w_s, sgu_b_s, sgu_w_out, norm_ffn_g, ffn_w_up, ffn_conv_w, ffn_conv_b, ffn_w_down, norm_ple_g, ple_w_gate, ple_w_proj, final_norm_g):
    depth = norm_mix_g.shape[0]
    assert depth == 2 and attn_w_qkv.shape[0] == 1 and sgu_w_uv.shape[0] == 1
    d = x_prompt.shape[-1]
    row = lambda a: a.reshape(a.shape[:-1] + (1, a.shape[-1]))
    w = dict(
        norm_mix_g=row(norm_mix_g), norm_ffn_g=row(norm_ffn_g), norm_ple_g=row(norm_ple_g),
        final_norm_g=final_norm_g.reshape(1, d),
        attn_w_qkv=attn_w_qkv[0].astype(BF16), attn_w_o=attn_w_o[0].astype(BF16),
        attn_lq1=attn_lq1, attn_lk1=attn_lk1, attn_lq2=attn_lq2, attn_lk2=attn_lk2, attn_subln_g=attn_subln_g,
        sgu_w_uv=sgu_w_uv[0].astype(BF16), sgu_ln_g=sgu_ln_g, sgu_ln_b=sgu_ln_b,
        sgu_w_s=sgu_w_s[0].astype(BF16), sgu_b_s=sgu_b_s[0][:, :, None], sgu_w_out=sgu_w_out[0].astype(BF16),
        ffn_w_up=ffn_w_up.astype(BF16), ffn_conv_w=ffn_conv_w, ffn_conv_b=row(ffn_conv_b),
        ffn_w_down=ffn_w_down.astype(BF16), ple_w_gate=ple_w_gate.astype(BF16), ple_w_proj=ple_w_proj.astype(BF16),
    )
    return _trunk(x_prompt, p_prompt, w), _trunk(x_sample, p_sample, w)
```

```python
import functools
import math

import jax
import jax.numpy as jnp
from jax import lax
from jax.experimental import pallas as pl
from jax.experimental.pallas import tpu as pltpu

F32 = jnp.float32
BF16 = jnp.bfloat16

EPS = 1e-6
ROPE_THETA = 10000.0
HEAD_DIM = 64
CHUNK = 128
CONV_WIDTH = 3
F32_ROWS = 8
BF16_ROWS = 16
HALO = F32_ROWS
LANES = 128
MXU_COLS = 256
PIECE_COLS = MXU_COLS
ONES_ROWS = 16
Q_SCALE = HEAD_DIM ** -0.5 * math.log2(math.e)

VMEM_LIMIT = 56 * 1024 * 1024


def _rms(x, g):
    return x * lax.rsqrt(jnp.mean(x * x, axis=-1, keepdims=True) + EPS) * g


def _const_spec(shape):
    nd = len(shape)
    return pl.BlockSpec(shape, lambda *_: (0,) * nd, pipeline_mode=pl.Buffered(1))


def _params(n_axes):
    return pltpu.CompilerParams(dimension_semantics=("arbitrary",) * n_axes, vmem_limit_bytes=VMEM_LIMIT)


def _attn_pre_kernel(x_ref, g_ref, w_ref, cos_ref, sin_ref, qt_ref, k_ref, vt_ref):
    tm, d = x_ref.shape
    hd = 2 * HEAD_DIM
    h = _rms(x_ref[...], g_ref[...]).astype(BF16)
    qkv = jnp.dot(h, w_ref[...], preferred_element_type=F32)
    lane = lax.broadcasted_iota(jnp.int32, (tm, LANES), 1)
    first_half = (lane % HEAD_DIM) < (HEAD_DIM // 2)

    def rope(t, cos, sin):
        rot = jnp.where(first_half, pltpu.roll(t, LANES - HEAD_DIM // 2, 1), pltpu.roll(t, HEAD_DIM // 2, 1))
        return t * cos + rot * sin

    cos, sin = cos_ref[...], sin_ref[...]
    for j in range(d // hd):
        cols = slice(j * hd, (j + 1) * hd)
        qt_ref[j] = (rope(qkv[:, j * hd:(j + 1) * hd], cos, sin) * Q_SCALE).T.astype(BF16)
        k_ref[:, cols] = rope(qkv[:, d + j * hd:d + (j + 1) * hd], cos, sin).astype(BF16)
        vt_ref[j, 0:hd, :] = qkv[:, 2 * d + j * hd:2 * d + (j + 1) * hd].T.astype(BF16)
        vt_ref[j, hd:hd + ONES_ROWS, :] = jnp.ones((ONES_ROWS, tm), BF16)


def _attn_pre(x, g, w_qkv, tables, seq, tm):
    n, d = x.shape
    hd = 2 * HEAD_DIM
    heads = d // hd
    tps = seq // tm
    tab_spec = pl.BlockSpec((tm, LANES), lambda i: (i % tps, 0))
    return pl.pallas_call(
        _attn_pre_kernel,
        grid=(n // tm,),
        in_specs=[pl.BlockSpec((tm, d), lambda i: (i, 0)), _const_spec((1, d)), _const_spec((d, 3 * d)),
                  tab_spec, tab_spec],
        out_specs=[pl.BlockSpec((heads, hd, tm), lambda i: (0, 0, i)),
                   pl.BlockSpec((tm, d), lambda i: (i, 0)),
                   pl.BlockSpec((heads, hd + ONES_ROWS, tm), lambda i: (0, 0, i))],
        out_shape=[jax.ShapeDtypeStruct((heads, hd, n), BF16), jax.ShapeDtypeStruct((n, d), BF16),
                   jax.ShapeDtypeStruct((heads, hd + ONES_ROWS, n), BF16)],
        compiler_params=_params(1),
        name="attn_pre",
    )(x, g, w_qkv, *tables)


def _flash_kernel(qt_ref, k_ref, vt_ref, lq1_ref, lk1_ref, lq2_ref, lk2_ref, g_ref, o_ref,
                  q2t_s, m_s, acc_s, *bufs, tk, group, lambda_init):
    hd, tq = qt_ref.shape[1], qt_ref.shape[2]
    seq = k_ref.shape[1]
    qt = qt_ref[0]
    row = lax.broadcasted_iota(jnp.int32, qt.shape, 0)
    zero = jnp.zeros_like(qt)
    q2t_s[:, 0:tq] = jnp.where(row < HEAD_DIM, qt, zero)
    q2t_s[:, tq:2 * tq] = jnp.where(row >= HEAD_DIM, qt, zero)
    m_s[...] = jnp.full(m_s.shape, -jnp.inf, F32)
    acc_s[...] = jnp.zeros(acc_s.shape, F32)
    slots = 2 * group
    s_bufs, p_bufs, x_bufs, a_bufs = (bufs[i * slots:(i + 1) * slots] for i in range(4))
    n_groups = seq // (tk * group)
    slabs_per_piece = PIECE_COLS // LANES
    vrows = vt_ref.shape[1]

    def work(values=None, softmax_half=None, scores=None):
        if softmax_half is not None:
            slots_m = range(softmax_half * group, (softmax_half + 1) * group)
            m_old = m_s[...]
            m_new = functools.reduce(jnp.maximum, [x_bufs[slot][...] for slot in slots_m], m_old)
            a_bufs[softmax_half][...] = jnp.exp2(m_old - m_new)
            m_s[...] = m_new
        if values is not None:
            slots_v = range(values[1] * group, (values[1] + 1) * group)
            start_v = pl.multiple_of(values[0] * group * tk, group * tk)
        for c in range(2 * tq // PIECE_COLS):
            cols = slice(c * PIECE_COLS, (c + 1) * PIECE_COLS)
            slabs = range(c * slabs_per_piece, (c + 1) * slabs_per_piece)
            if values is not None:
                p = jnp.concatenate([jnp.concatenate([p_bufs[slot][j, 0:tk] for slot in slots_v], axis=0)
                                     for j in slabs], axis=1)
                pv = jnp.dot(vt_ref[0, :, pl.ds(start_v, group * tk)], p, preferred_element_type=F32)
                for i, j in enumerate(slabs):
                    alpha = a_bufs[values[1]][:, j * LANES:(j + 1) * LANES]
                    acc_s[j, 0:vrows] = alpha * acc_s[j, 0:vrows] + pv[:, i * LANES:(i + 1) * LANES]
            for g in range(group):
                if softmax_half is not None:
                    slot_m = softmax_half * group + g
                    for j in slabs:
                        lcols = slice(j * LANES, (j + 1) * LANES)
                        p_bufs[slot_m][j, 0:tk] = jnp.exp2(s_bufs[slot_m][j, 0:tk] - m_new[:, lcols]).astype(BF16)
                if scores is not None:
                    slot_s = scores[1] * group + g
                    start_s = pl.multiple_of((scores[0] * group + g) * tk, tk)
                    s = jnp.dot(k_ref[0, pl.ds(start_s, tk), :], q2t_s[:, cols], preferred_element_type=F32)
                    for i, j in enumerate(slabs):
                        s_bufs[slot_s][j, 0:tk] = s[:, i * LANES:(i + 1) * LANES]
                    x_bufs[slot_s][:, cols] = jnp.max(s, axis=0, keepdims=True)

    def step(b, half):
        work(values=(b, half), softmax_half=1 - half, scores=(b + 2, half))

    work(scores=(0, 0))
    work(softmax_half=0, scores=(1, 1))

    def body(b, carry):
        for half in range(2):
            pl.when(b % 2 == half)(functools.partial(step, b, half))
        return carry

    lax.fori_loop(0, n_groups - 2, body, 0)
    work(values=(n_groups - 2, n_groups % 2), softmax_half=(n_groups - 1) % 2)
    work(values=(n_groups - 1, (n_groups - 1) % 2))

    acc = jnp.concatenate([acc_s[j, 0:vrows] for j in range(2 * tq // LANES)], axis=1)
    o_maps = acc[0:hd] / acc[hd:hd + 1]
    lam = (jnp.exp(jnp.sum(lq1_ref[...] * lk1_ref[...], axis=-1, keepdims=True))
           - jnp.exp(jnp.sum(lq2_ref[...] * lk2_ref[...], axis=-1, keepdims=True)) + lambda_init)
    o = (o_maps[:, 0:tq] - lam * o_maps[:, tq:2 * tq]).T
    o_ref[0] = (_rms(o, g_ref[...]) * (1.0 - lambda_init)).astype(o_ref.dtype)


def _flash(qt, k, vt, lq1, lk1, lq2, lk2, subln_g, lambda_init, tq, tk, group):
    b, seq, d = k.shape
    heads, hd, _ = qt.shape
    vrows = vt.shape[1]
    qtiles = seq // tq
    slots = 2 * group
    assert seq % (tk * group) == 0 and seq // (tk * group) >= 2, (seq, tk, group)
    small = _const_spec((1, HEAD_DIM))
    return pl.pallas_call(
        functools.partial(_flash_kernel, tk=tk, group=group, lambda_init=lambda_init),
        grid=(b, heads, qtiles),
        in_specs=[pl.BlockSpec((1, hd, tq), lambda bi, h, qi: (h, 0, bi * qtiles + qi)),
                  pl.BlockSpec((1, seq, hd), lambda bi, h, qi: (bi, 0, h)),
                  pl.BlockSpec((1, vrows, seq), lambda bi, h, qi: (h, 0, bi)),
                  small, small, small, small, _const_spec((1, hd))],
        out_specs=pl.BlockSpec((1, tq, hd), lambda bi, h, qi: (bi, qi, h)),
        out_shape=jax.ShapeDtypeStruct((b, seq, d), BF16),
        scratch_shapes=[pltpu.VMEM((hd, 2 * tq), BF16), pltpu.VMEM((1, 2 * tq), F32),
                        pltpu.VMEM((2 * tq // LANES, vrows + F32_ROWS, LANES), F32)]
        + [pltpu.VMEM((2 * tq // LANES, tk + F32_ROWS, LANES), F32)] * slots
        + [pltpu.VMEM((2 * tq // LANES, tk + BF16_ROWS, LANES), BF16)] * slots
        + [pltpu.VMEM((1, 2 * tq), F32)] * slots
        + [pltpu.VMEM((1, 2 * tq), F32)] * 2,
        compiler_params=_params(3),
        name="flash_diff_attn",
    )(qt, k, vt, lq1, lk1, lq2, lk2, subln_g)


def _attn_post_kernel(x_ref, o_ref, w_ref, y_ref):
    y_ref[...] = x_ref[...] + jnp.dot(o_ref[...], w_ref[...], preferred_element_type=F32)


def _attn_post(x, o, w_o, tm):
    n, d = x.shape
    row = pl.BlockSpec((tm, d), lambda i: (i, 0))
    return pl.pallas_call(
        _attn_post_kernel,
        grid=(n // tm,),
        in_specs=[row, row, _const_spec((d, d))],
        out_specs=row,
        out_shape=jax.ShapeDtypeStruct((n, d), F32),
        compiler_params=_params(1),
        name="attn_post",
    )(x, o, w_o)


def _ffn_ple_kernel(x_ref, xp_ref, xn_ref, p_ref, gf_ref, wup_ref, cw_ref, cb_ref, wdn_ref,
                    gp_ref, wg_ref, wp_ref, gfin_ref, y_ref, a_s, act_s, *, tiles_per_seq, fc, final):
    tm, d = x_ref.shape
    f = wdn_ref.shape[0]
    i = pl.program_id(0)
    keep_prev = jnp.where(i % tiles_per_seq == 0, 0.0, 1.0).astype(F32)
    keep_next = jnp.where(i % tiles_per_seq == tiles_per_seq - 1, 0.0, 1.0).astype(F32)
    x = x_ref[...]
    gf = gf_ref[...]
    h = jnp.concatenate([_rms(xp_ref[...], gf) * keep_prev, _rms(x, gf), _rms(xn_ref[...], gf) * keep_next], axis=0)
    a_s[...] = jnp.dot(h.astype(BF16), wup_ref[...], preferred_element_type=F32)

    def conv(col0):
        cols = pl.ds(col0, fc)
        c = cb_ref[:, cols]
        for t in range(CONV_WIDTH):
            c = c + a_s[pl.ds(HALO - 1 + t, tm), cols] * cw_ref[pl.ds(t, 1), cols]
        return c

    for j in range(f // fc):
        val = conv(j * fc)
        gate = conv(f + j * fc)
        act_s[:, pl.ds(j * fc, fc)] = (gate * jax.nn.sigmoid(gate) * val).astype(BF16)

    x = x + jnp.dot(act_s[...], wdn_ref[...], preferred_element_type=F32)
    gate = jax.nn.sigmoid(jnp.dot(_rms(x, gp_ref[...]).astype(BF16), wg_ref[...], preferred_element_type=F32))
    x = x + gate * jnp.dot(p_ref[...].astype(BF16), wp_ref[...], preferred_element_type=F32)
    if final:
        x = _rms(x, gfin_ref[...])
    y_ref[...] = x


def _ffn_ple(x, p, g_ffn, w_up, conv_w, conv_b, w_down, g_ple, w_gate, w_proj, g_final, layer, seq, tm, final):
    n, d = x.shape
    f = w_down.shape[1]
    pd = p.shape[2]

    def layer_spec(*shape):
        return pl.BlockSpec((None,) + shape, lambda i: (layer,) + (0,) * len(shape), pipeline_mode=pl.Buffered(1))

    fc = 2 * LANES
    tiles_per_seq = seq // tm
    hb = tm // HALO
    last_blk = n // HALO - 1
    return pl.pallas_call(
        functools.partial(_ffn_ple_kernel, tiles_per_seq=tiles_per_seq, fc=fc, final=final),
        grid=(n // tm,),
        in_specs=[pl.BlockSpec((tm, d), lambda i: (i, 0)),
                  pl.BlockSpec((HALO, d), lambda i: (jnp.maximum(i * hb - 1, 0), 0)),
                  pl.BlockSpec((HALO, d), lambda i: (jnp.minimum((i + 1) * hb, last_blk), 0)),
                  pl.BlockSpec((None, tm, pd), lambda i: (layer, i, 0)),
                  layer_spec(1, d), layer_spec(d, 2 * f), layer_spec(CONV_WIDTH, 2 * f),
                  layer_spec(1, 2 * f), layer_spec(f, d), layer_spec(1, d), layer_spec(d, d),
                  layer_spec(pd, d), _const_spec((1, d))],
        out_specs=pl.BlockSpec((tm, d), lambda i: (i, 0)),
        out_shape=jax.ShapeDtypeStruct((n, d), F32),
        scratch_shapes=[pltpu.VMEM((tm + 2 * HALO, 2 * f), F32), pltpu.VMEM((tm, f), BF16)],
        compiler_params=_params(1),
        name="ffn_ple_final" if final else "ffn_ple",
    )(x, x, x, p, g_ffn, w_up, conv_w, conv_b, w_down, g_ple, w_gate, w_proj, g_final)


def _sgu_kernel(x_ref, g_ref, wuv_ref, lng_ref, lnb_ref, ws_ref, bs_ref, wout_ref, y_ref, um_s):
    tm, d = x_ref.shape
    w = wout_ref.shape[0]
    x = x_ref[...]
    z = jnp.dot(_rms(x, g_ref[...]).astype(BF16), wuv_ref[...], preferred_element_type=F32)
    z = 0.5 * z * (1.0 + lax.erf(z * (1.0 / math.sqrt(2.0))))
    v = z[:, w:]
    vc = v - jnp.mean(v, axis=-1, keepdims=True)
    v = vc * lax.rsqrt(jnp.mean(vc * vc, axis=-1, keepdims=True) + EPS) * lng_ref[...] + lnb_ref[...]
    vb = v.astype(BF16)
    for c in range(tm // CHUNK):
        rows = slice(c * CHUNK, (c + 1) * CHUNK)
        for grp in range(w // CHUNK):
            cols = slice(grp * CHUNK, (grp + 1) * CHUNK)
            mixed = jnp.dot(ws_ref[grp], vb[rows, cols], preferred_element_type=F32) + bs_ref[grp]
            um_s[rows, cols] = (z[rows, cols] * mixed).astype(BF16)
    y_ref[...] = x + jnp.dot(um_s[...], wout_ref[...], preferred_element_type=F32)


def _sgu(x, g, w_uv, ln_g, ln_b, w_s, b_s, w_out, tm):
    n, d = x.shape
    w = w_out.shape[0]
    ng = w // CHUNK
    row = pl.BlockSpec((tm, d), lambda i: (i, 0))
    return pl.pallas_call(
        _sgu_kernel,
        grid=(n // tm,),
        in_specs=[row, _const_spec((1, d)), _const_spec((d, 2 * w)), _const_spec((1, w)), _const_spec((1, w)),
                  _const_spec((ng, CHUNK, CHUNK)), _const_spec((ng, CHUNK, 1)), _const_spec((w, d))],
        out_specs=row,
        out_shape=jax.ShapeDtypeStruct((n, d), F32),
        scratch_shapes=[pltpu.VMEM((tm, w), BF16)],
        compiler_params=_params(1),
        name="sgu",
    )(x, g, w_uv, ln_g, ln_b, w_s, b_s, w_out)


def _rope_tables(seq):
    half = HEAD_DIM // 2
    inv = 1.0 / (ROPE_THETA ** (jnp.arange(0, HEAD_DIM, 2, dtype=F32) / HEAD_DIM))
    ang = jnp.arange(seq, dtype=F32)[:, None] * inv[None, :]
    cos, sin = jnp.cos(ang), jnp.sin(ang)
    sign = jnp.where((jnp.arange(LANES) % HEAD_DIM) < half, -1.0, 1.0).astype(F32)
    reps = LANES // half
    return jnp.concatenate([cos] * reps, axis=-1), jnp.concatenate([sin] * reps, axis=-1) * sign


def _flash_tiles(seq):
    group = 1
    tq = _tile(seq, 2048)
    tk = _tile(seq, max(LANES, min(512, seq // (4 * group))))
    return tq, tk, group


def _tile(n, want):
    t = min(n, want)
    assert n % t == 0, (n, t)
    return t


def _trunk(x, p, w):
    b, seq, d = x.shape
    n = b * seq
    x = x.reshape(n, d)
    tm = _tile(seq, 512)
    qt, k, vt = _attn_pre(x, w["norm_mix_g"][0], w["attn_w_qkv"], _rope_tables(seq), seq, tm)
    lambda_init = 0.8 - 0.6 * math.exp(-0.3 * 0)
    o = _flash(qt, k.reshape(b, seq, d), vt,
               w["attn_lq1"], w["attn_lk1"], w["attn_lq2"], w["attn_lk2"], w["attn_subln_g"],
               lambda_init, *_flash_tiles(seq))
    x = _attn_post(x, o.reshape(n, d), w["attn_w_o"], tm)
    p = p.reshape(p.shape[0], n, p.shape[-1])
    ffn = functools.partial(_ffn_ple, p=p, g_ffn=w["norm_ffn_g"], w_up=w["ffn_w_up"], conv_w=w["ffn_conv_w"],
                            conv_b=w["ffn_conv_b"], w_down=w["ffn_w_down"], g_ple=w["norm_ple_g"],
                            w_gate=w["ple_w_gate"], w_proj=w["ple_w_proj"], g_final=w["final_norm_g"], seq=seq, tm=tm)
    x = ffn(x, layer=0, final=False)
    x = _sgu(x, w["norm_mix_g"][1], w["sgu_w_uv"], w["sgu_ln_g"], w["sgu_ln_b"], w["sgu_w_s"], w["sgu_b_s"],
             w["sgu_w_out"], tm)
    x = ffn(x, layer=1, final=True)
    return x.reshape(b, seq, d)


def kernel(x_prompt, x_sample, p_prompt, p_sample, norm_mix_g, attn_w_qkv, attn_lq1, attn_lk1, attn_lq2, attn_lk2, attn_subln_g, attn_w_o, sgu_w_uv, sgu_ln_g, sgu_ln_b, sgu_w_s, sgu_b_s, sgu_w_out, norm_ffn_g, ffn_w_up, ffn_conv_w, ffn_conv_b, ffn_w_down, norm_ple_g, ple_w_gate, ple_w_proj, final_norm_g):
    depth = norm_mix_g.shape[0]
    assert depth == 2 and attn_w_qkv.shape[0] == 1 and sgu_w_uv.shape[0] == 1
    d = x_prompt.shape[-1]
    row = lambda a: a.reshape(a.shape[:-1] + (1, a.shape[-1]))
    w = dict(
        norm_mix_g=row(norm_mix_g), norm_ffn_g=row(norm_ffn_g), norm_ple_g=row(norm_ple_g),
        final_norm_g=final_norm_g.reshape(1, d),
        attn_w_qkv=attn_w_qkv[0].astype(BF16), attn_w_o=attn_w_o[0].astype(BF16),
        attn_lq1=attn_lq1, attn_lk1=attn_lk1, attn_lq2=attn_lq2, attn_lk2=attn_lk2, attn_subln_g=attn_subln_g,
        sgu_w_uv=sgu_w_uv[0].astype(BF16), sgu_ln_g=sgu_ln_g, sgu_ln_b=sgu_ln_b,
        sgu_w_s=sgu_w_s[0].astype(BF16), sgu_b_s=sgu_b_s[0][:, :, None], sgu_w_out=sgu_w_out[0].astype(BF16),
        ffn_w_up=ffn_w_up.astype(BF16), ffn_conv_w=ffn_conv_w, ffn_conv_b=row(ffn_conv_b),
        ffn_w_down=ffn_w_down.astype(BF16), ple_w_gate=ple_w_gate.astype(BF16), ple_w_proj=ple_w_proj.astype(BF16),
    )
    return _trunk(x_prompt, p_prompt, w), _trunk(x_sample, p_sample, w)
```

```python
import functools
import math

import jax
import jax.numpy as jnp
from jax import lax
from jax.experimental import pallas as pl
from jax.experimental.pallas import tpu as pltpu

F32 = jnp.float32
BF16 = jnp.bfloat16

EPS = 1e-6
ROPE_THETA = 10000.0
HEAD_DIM = 64
CHUNK = 128
CONV_WIDTH = 3
F32_ROWS = 8
BF16_ROWS = 16
HALO = F32_ROWS
LANES = 128
MXU_COLS = 256
PIECE_COLS = MXU_COLS
ONES_ROWS = 16
Q_SCALE = HEAD_DIM ** -0.5 * math.log2(math.e)

VMEM_LIMIT = 56 * 1024 * 1024


def _rms(x, g):
    return x * lax.rsqrt(jnp.mean(x * x, axis=-1, keepdims=True) + EPS) * g


def _const_spec(shape):
    nd = len(shape)
    return pl.BlockSpec(shape, lambda *_: (0,) * nd, pipeline_mode=pl.Buffered(1))


def _params(n_axes):
    return pltpu.CompilerParams(dimension_semantics=("arbitrary",) * n_axes, vmem_limit_bytes=VMEM_LIMIT)


def _attn_pre_kernel(x_ref, g_ref, w_ref, cos_ref, sin_ref, qt_ref, k_ref, vt_ref):
    tm, d = x_ref.shape
    hd = 2 * HEAD_DIM
    h = _rms(x_ref[...], g_ref[...]).astype(BF16)
    qkv = jnp.dot(h, w_ref[...], preferred_element_type=F32)
    lane = lax.broadcasted_iota(jnp.int32, (tm, LANES), 1)
    first_half = (lane % HEAD_DIM) < (HEAD_DIM // 2)

    def rope(t, cos, sin):
        rot = jnp.where(first_half, pltpu.roll(t, LANES - HEAD_DIM // 2, 1), pltpu.roll(t, HEAD_DIM // 2, 1))
        return t * cos + rot * sin

    cos, sin = cos_ref[...], sin_ref[...]
    for j in range(d // hd):
        cols = slice(j * hd, (j + 1) * hd)
        qt_ref[j] = (rope(qkv[:, j * hd:(j + 1) * hd], cos, sin) * Q_SCALE).T.astype(BF16)
        k_ref[:, cols] = rope(qkv[:, d + j * hd:d + (j + 1) * hd], cos, sin).astype(BF16)
        vt_ref[j, 0:hd, :] = qkv[:, 2 * d + j * hd:2 * d + (j + 1) * hd].T.astype(BF16)
        vt_ref[j, hd:hd + ONES_ROWS, :] = jnp.ones((ONES_ROWS, tm), BF16)


def _attn_pre(x, g, w_qkv, tables, seq, tm):
    n, d = x.shape
    hd = 2 * HEAD_DIM
    heads = d // hd
    tps = seq // tm
    tab_spec = pl.BlockSpec((tm, LANES), lambda i: (i % tps, 0))
    return pl.pallas_call(
        _attn_pre_kernel,
        grid=(n // tm,),
        in_specs=[pl.BlockSpec((tm, d), lambda i: (i, 0)), _const_spec((1, d)), _const_spec((d, 3 * d)),
                  tab_spec, tab_spec],
        out_specs=[pl.BlockSpec((heads, hd, tm), lambda i: (0, 0, i)),
                   pl.BlockSpec((tm, d), lambda i: (i, 0)),
                   pl.BlockSpec((heads, hd + ONES_ROWS, tm), lambda i: (0, 0, i))],
        out_shape=[jax.ShapeDtypeStruct((heads, hd, n), BF16), jax.ShapeDtypeStruct((n, d), BF16),
                   jax.ShapeDtypeStruct((heads, hd + ONES_ROWS, n), BF16)],
        compiler_params=_params(1),
        name="attn_pre",
    )(x, g, w_qkv, *tables)


def _flash_kernel(qt_ref, k_ref, vt_ref, lq1_ref, lk1_ref, lq2_ref, lk2_ref, g_ref, o_ref,
                  q2t_s, m_s, acc_s, *bufs, tk, group, lambda_init):
    hd, tq = qt_ref.shape[1], qt_ref.shape[2]
    seq = k_ref.shape[1]
    qt = qt_ref[0]
    row = lax.broadcasted_iota(jnp.int32, qt.shape, 0)
    zero = jnp.zeros_like(qt)
    q2t_s[:, 0:tq] = jnp.where(row < HEAD_DIM, qt, zero)
    q2t_s[:, tq:2 * tq] = jnp.where(row >= HEAD_DIM, qt, zero)
    m_s[...] = jnp.full(m_s.shape, -jnp.inf, F32)
    acc_s[...] = jnp.zeros(acc_s.shape, F32)
    slots = 2 * group
    s_bufs, p_bufs, x_bufs, a_bufs = (bufs[i * slots:(i + 1) * slots] for i in range(4))
    n_groups = seq // (tk * group)
    slabs_per_piece = PIECE_COLS // LANES
    vrows = vt_ref.shape[1]

    def work(values=None, softmax_half=None, scores=None):
        if softmax_half is not None:
            slots_m = range(softmax_half * group, (softmax_half + 1) * group)
            m_old = m_s[...]
            m_new = functools.reduce(jnp.maximum, [x_bufs[slot][...] for slot in slots_m], m_old)
            a_bufs[softmax_half][...] = jnp.exp2(m_old - m_new)
            m_s[...] = m_new
        if values is not None:
            slots_v = range(values[1] * group, (values[1] + 1) * group)
            start_v = pl.multiple_of(values[0] * group * tk, group * tk)
        for c in range(2 * tq // PIECE_COLS):
            cols = slice(c * PIECE_COLS, (c + 1) * PIECE_COLS)
            slabs = range(c * slabs_per_piece, (c + 1) * slabs_per_piece)
            if values is not None:
                p = jnp.concatenate([jnp.concatenate([p_bufs[slot][j, 0:tk] for slot in slots_v], axis=0)
                                     for j in slabs], axis=1)
                pv = jnp.dot(vt_ref[0, :, pl.ds(start_v, group * tk)], p, preferred_element_type=F32)
                for i, j in enumerate(slabs):
                    alpha = a_bufs[values[1]][:, j * LANES:(j + 1) * LANES]
                    acc_s[j, 0:vrows] = alpha * acc_s[j, 0:vrows] + pv[:, i * LANES:(i + 1) * LANES]
            for g in range(group):
                if softmax_half is not None:
                    slot_m = softmax_half * group + g
                    for j in slabs:
                        lcols = slice(j * LANES, (j + 1) * LANES)
                        p_bufs[slot_m][j, 0:tk] = jnp.exp2(s_bufs[slot_m][j, 0:tk] - m_new[:, lcols]).astype(BF16)
                if scores is not None:
                    slot_s = scores[1] * group + g
                    start_s = pl.multiple_of((scores[0] * group + g) * tk, tk)
                    s = jnp.dot(k_ref[0, pl.ds(start_s, tk), :], q2t_s[:, cols], preferred_element_type=F32)
                    for i, j in enumerate(slabs):
                        s_bufs[slot_s][j, 0:tk] = s[:, i * LANES:(i + 1) * LANES]
                    x_bufs[slot_s][:, cols] = jnp.max(s, axis=0, keepdims=True)

    def step(b, half):
        work(values=(b, half), softmax_half=1 - half, scores=(b + 2, half))

    work(scores=(0, 0))
    work(softmax_half=0, scores=(1, 1))

    def body(b, carry):
        for half in range(2):
            pl.when(b % 2 == half)(functools.partial(step, b, half))
        return carry

    lax.fori_loop(0, n_groups - 2, body, 0)
    work(values=(n_groups - 2, n_groups % 2), softmax_half=(n_groups - 1) % 2)
    work(values=(n_groups - 1, (n_groups - 1) % 2))

    acc = jnp.concatenate([acc_s[j, 0:vrows] for j in range(2 * tq // LANES)], axis=1)
    o_maps = acc[0:hd] / acc[hd:hd + 1]
    lam = (jnp.exp(jnp.sum(lq1_ref[...] * lk1_ref[...], axis=-1, keepdims=True))
           - jnp.exp(jnp.sum(lq2_ref[...] * lk2_ref[...], axis=-1, keepdims=True)) + lambda_init)
    o = (o_maps[:, 0:tq] - lam * o_maps[:, tq:2 * tq]).T
    o_ref[0] = (_rms(o, g_ref[...]) * (1.0 - lambda_init)).astype(o_ref.dtype)


def _flash(qt, k, vt, lq1, lk1, lq2, lk2, subln_g, lambda_init, tq, tk, group):
    b, seq, d = k.shape
    heads, hd, _ = qt.shape
    vrows = vt.shape[1]
    qtiles = seq // tq
    slots = 2 * group
    assert seq % (tk * group) == 0 and seq // (tk * group) >= 2, (seq, tk, group)
    small = _const_spec((1, HEAD_DIM))
    return pl.pallas_call(
        functools.partial(_flash_kernel, tk=tk, group=group, lambda_init=lambda_init),
        grid=(b, heads, qtiles),
        in_specs=[pl.BlockSpec((1, hd, tq), lambda bi, h, qi: (h, 0, bi * qtiles + qi)),
                  pl.BlockSpec((1, seq, hd), lambda bi, h, qi: (bi, 0, h)),
                  pl.BlockSpec((1, vrows, seq), lambda bi, h, qi: (h, 0, bi)),
                  small, small, small, small, _const_spec((1, hd))],
        out_specs=pl.BlockSpec((1, tq, hd), lambda bi, h, qi: (bi, qi, h)),
        out_shape=jax.ShapeDtypeStruct((b, seq, d), BF16),
        scratch_shapes=[pltpu.VMEM((hd, 2 * tq), BF16), pltpu.VMEM((1, 2 * tq), F32),
                        pltpu.VMEM((2 * tq // LANES, vrows + F32_ROWS, LANES), F32)]
        + [pltpu.VMEM((2 * tq // LANES, tk + F32_ROWS, LANES), F32)] * slots
        + [pltpu.VMEM((2 * tq // LANES, tk + BF16_ROWS, LANES), BF16)] * slots
        + [pltpu.VMEM((1, 2 * tq), F32)] * slots
        + [pltpu.VMEM((1, 2 * tq), F32)] * 2,
        compiler_params=_params(3),
        name="flash_diff_attn",
    )(qt, k, vt, lq1, lk1, lq2, lk2, subln_g)


def _attn_post_kernel(x_ref, o_ref, w_ref, y_ref):
    y_ref[...] = x_ref[...] + jnp.dot(o_ref[...], w_ref[...], preferred_element_type=F32)


def _attn_post(x, o, w_o, tm):
    n, d = x.shape
    row = pl.BlockSpec((tm, d), lambda i: (i, 0))
    return pl.pallas_call(
        _attn_post_kernel,
        grid=(n // tm,),
        in_specs=[row, row, _const_spec((d, d))],
        out_specs=row,
        out_shape=jax.ShapeDtypeStruct((n, d), F32),
        compiler_params=_params(1),
        name="attn_post",
    )(x, o, w_o)


def _ffn_ple_kernel(x_ref, xp_ref, xn_ref, p_ref, gf_ref, wup_ref, cw_ref, cb_ref, wdn_ref,
                    gp_ref, wg_ref, wp_ref, gfin_ref, y_ref, a_s, act_s, *, tiles_per_seq, fc, final):
    tm, d = x_ref.shape
    f = wdn_ref.shape[0]
    i = pl.program_id(0)
    keep_prev = jnp.where(i % tiles_per_seq == 0, 0.0, 1.0).astype(F32)
    keep_next = jnp.where(i % tiles_per_seq == tiles_per_seq - 1, 0.0, 1.0).astype(F32)
    x = x_ref[...]
    gf = gf_ref[...]
    h = jnp.concatenate([_rms(xp_ref[...], gf) * keep_prev, _rms(x, gf), _rms(xn_ref[...], gf) * keep_next], axis=0)
    a_s[...] = jnp.dot(h.astype(BF16), wup_ref[...], preferred_element_type=F32)

    def conv(col0):
        cols = pl.ds(col0, fc)
        c = cb_ref[:, cols]
        for t in range(CONV_WIDTH):
            c = c + a_s[pl.ds(HALO - 1 + t, tm), cols] * cw_ref[pl.ds(t, 1), cols]
        return c

    for j in range(f // fc):
        val = conv(j * fc)
        gate = conv(f + j * fc)
        act_s[:, pl.ds(j * fc, fc)] = (gate * jax.nn.sigmoid(gate) * val).astype(BF16)

    x = x + jnp.dot(act_s[...], wdn_ref[...], preferred_element_type=F32)
    gate = jax.nn.sigmoid(jnp.dot(_rms(x, gp_ref[...]).astype(BF16), wg_ref[...], preferred_element_type=F32))
    x = x + gate * jnp.dot(p_ref[...].astype(BF16), wp_ref[...], preferred_element_type=F32)
    if final:
        x = _rms(x, gfin_ref[...])
    y_ref[...] = x


def _ffn_ple(x, p, g_ffn, w_up, conv_w, conv_b, w_down, g_ple, w_gate, w_proj, g_final, layer, seq, tm, final):
    n, d = x.shape
    f = w_down.shape[1]
    pd = p.shape[2]

    def layer_spec(*shape):
        return pl.BlockSpec((None,) + shape, lambda i: (layer,) + (0,) * len(shape), pipeline_mode=pl.Buffered(1))

    fc = 2 * LANES
    tiles_per_seq = seq // tm
    hb = tm // HALO
    last_blk = n // HALO - 1
    return pl.pallas_call(
        functools.partial(_ffn_ple_kernel, tiles_per_seq=tiles_per_seq, fc=fc, final=final),
        grid=(n // tm,),
        in_specs=[pl.BlockSpec((tm, d), lambda i: (i, 0)),
                  pl.BlockSpec((HALO, d), lambda i: (jnp.maximum(i * hb - 1, 0), 0)),
                  pl.BlockSpec((HALO, d), lambda i: (jnp.minimum((i + 1) * hb, last_blk), 0)),
                  pl.BlockSpec((None, tm, pd), lambda i: (layer, i, 0)),
                  layer_spec(1, d), layer_spec(d, 2 * f), layer_spec(CONV_WIDTH, 2 * f),
                  layer_spec(1, 2 * f), layer_spec(f, d), layer_spec(1, d), layer_spec(d, d),
                  layer_spec(pd, d), _const_spec((1, d))],
        out_specs=pl.BlockSpec((tm, d), lambda i: (i, 0)),
        out_shape=jax.ShapeDtypeStruct((n, d), F32),
        scratch_shapes=[pltpu.VMEM((tm + 2 * HALO, 2 * f), F32), pltpu.VMEM((tm, f), BF16)],
        compiler_params=_params(1),
        name="ffn_ple_final" if final else "ffn_ple",
    )(x, x, x, p, g_ffn, w_up, conv_w, conv_b, w_down, g_ple, w_gate, w_proj, g_final)


def _sgu_kernel(x_ref, g_ref, wuv_ref, lng_ref, lnb_ref, ws_ref, bs_ref, wout_ref, y_ref, um_s):
    tm, d = x_ref.shape
    w = wout_ref.shape[0]
    x = x_ref[...]
    z = jnp.dot(_rms(x, g_ref[...]).astype(BF16), wuv_ref[...], preferred_element_type=F32)
    z = 0.5 * z * (1.0 + lax.erf(z * (1.0 / math.sqrt(2.0))))
    v = z[:, w:]
    vc = v - jnp.mean(v, axis=-1, keepdims=True)
    v = vc * lax.rsqrt(jnp.mean(vc * vc, axis=-1, keepdims=True) + EPS) * lng_ref[...] + lnb_ref[...]
    vb = v.astype(BF16)
    for c in range(tm // CHUNK):
        rows = slice(c * CHUNK, (c + 1) * CHUNK)
        for grp in range(w // CHUNK):
            cols = slice(grp * CHUNK, (grp + 1) * CHUNK)
            mixed = jnp.dot(ws_ref[grp], vb[rows, cols], preferred_element_type=F32) + bs_ref[grp]
            um_s[rows, cols] = (z[rows, cols] * mixed).astype(BF16)
    y_ref[...] = x + jnp.dot(um_s[...], wout_ref[...], preferred_element_type=F32)


def _sgu(x, g, w_uv, ln_g, ln_b, w_s, b_s, w_out, tm):
    n, d = x.shape
    w = w_out.shape[0]
    ng = w // CHUNK
    row = pl.BlockSpec((tm, d), lambda i: (i, 0))
    return pl.pallas_call(
        _sgu_kernel,
        grid=(n // tm,),
        in_specs=[row, _const_spec((1, d)), _const_spec((d, 2 * w)), _const_spec((1, w)), _const_spec((1, w)),
                  _const_spec((ng, CHUNK, CHUNK)), _const_spec((ng, CHUNK, 1)), _const_spec((w, d))],
        out_specs=row,
        out_shape=jax.ShapeDtypeStruct((n, d), F32),
        scratch_shapes=[pltpu.VMEM((tm, w), BF16)],
        compiler_params=_params(1),
        name="sgu",
    )(x, g, w_uv, ln_g, ln_b, w_s, b_s, w_out)


def _rope_tables(seq):
    half = HEAD_DIM // 2
    inv = 1.0 / (ROPE_THETA ** (jnp.arange(0, HEAD_DIM, 2, dtype=F32) / HEAD_DIM))
    ang = jnp.arange(seq, dtype=F32)[:, None] * inv[None, :]
    cos, sin = jnp.cos(ang), jnp.sin(ang)
    sign = jnp.where((jnp.arange(LANES) % HEAD_DIM) < half, -1.0, 1.0).astype(F32)
    reps = LANES // half
    return jnp.concatenate([cos] * reps, axis=-1), jnp.concatenate([sin] * reps, axis=-1) * sign


def _flash_tiles(seq):
    tq = _tile(seq, 2048)
    tk = _tile(seq, LANES)
    group = max(1, min(4, seq // tk // 4))
    return tq, tk, group


def _tile(n, want):
    t = min(n, want)
    assert n % t == 0, (n, t)
    return t


def _trunk(x, p, w):
    b, seq, d = x.shape
    n = b * seq
    x = x.reshape(n, d)
    tm = _tile(seq, 512)
    qt, k, vt = _attn_pre(x, w["norm_mix_g"][0], w["attn_w_qkv"], _rope_tables(seq), seq, tm)
    lambda_init = 0.8 - 0.6 * math.exp(-0.3 * 0)
    o = _flash(qt, k.reshape(b, seq, d), vt,
               w["attn_lq1"], w["attn_lk1"], w["attn_lq2"], w["attn_lk2"], w["attn_subln_g"],
               lambda_init, *_flash_tiles(seq))
    x = _attn_post(x, o.reshape(n, d), w["attn_w_o"], tm)
    p = p.reshape(p.shape[0], n, p.shape[-1])
    ffn = functools.partial(_ffn_ple, p=p, g_ffn=w["norm_ffn_g"], w_up=w["ffn_w_up"], conv_w=w["ffn_conv_w"],
                            conv_b=w["ffn_conv_b"], w_down=w["ffn_w_down"], g_ple=w["norm_ple_g"],
                            w_gate=w["ple_w_gate"], w_proj=w["ple_w_proj"], g_final=w["final_norm_g"], seq=seq, tm=tm)
    x = ffn(x, layer=0, final=False)
    x = _sgu(x, w["norm_mix_g"][1], w["sgu_w_uv"], w["sgu_ln_g"], w["sgu_ln_b"], w["sgu_w_s"], w["sgu_b_s"],
             w["sgu_w_out"], tm)
    x = ffn(x, layer=1, final=True)
    return x.reshape(b, seq, d)


def kernel(x_prompt, x_sample, p_prompt, p_sample, norm_mix_g, attn_w_qkv, attn_lq1, attn_lk1, attn_lq2, attn_lk2, attn_subln_g, attn_w_o, sgu_w_uv, sgu_ln_g, sgu_ln_b, sgu_w_s, sgu_b_s, sgu_w_out, norm_ffn_g, ffn_w_up, ffn_conv_w, ffn_conv_b, ffn_w_down, norm_ple_g, ple_w_gate, ple_w_proj, final_norm_g):
    depth = norm_mix_g.shape[0]
    assert depth == 2 and attn_w_qkv.shape[0] == 1 and sgu_w_uv.shape[0] == 1
    d = x_prompt.shape[-1]
    row = lambda a: a.reshape(a.shape[:-1] + (1, a.shape[-1]))
    w = dict(
        norm_mix_g=row(norm_mix_g), norm_ffn_g=row(norm_ffn_g), norm_ple_g=row(norm_ple_g),
        final_norm_g=final_norm_g.reshape(1, d),
        attn_w_qkv=attn_w_qkv[0].astype(BF16), attn_w_o=attn_w_o[0].astype(BF16),
        attn_lq1=attn_lq1, attn_lk1=attn_lk1, attn_lq2=attn_lq2, attn_lk2=attn_lk2, attn_subln_g=attn_subln_g,
        sgu_w_uv=sgu_w_uv[0].astype(BF16), sgu_ln_g=sgu_ln_g, sgu_ln_b=sgu_ln_b,
        sgu_w_s=sgu_w_s[0].astype(BF16), sgu_b_s=sgu_b_s[0][:, :, None], sgu_w_out=sgu_w_out[0].astype(BF16),
        ffn_w_up=ffn_w_up.astype(BF16), ffn_conv_w=ffn_conv_w, ffn_conv_b=row(ffn_conv_b),
        ffn_w_down=ffn_w_down.astype(BF16), ple_w_gate=ple_w_gate.astype(BF16), ple_w_proj=ple_w_proj.astype(BF16),
    )
    return _trunk(x_prompt, p_prompt, w), _trunk(x_sample, p_sample, w)
```

```python
import functools
import math

import jax
import jax.numpy as jnp
from jax import lax
from jax.experimental import pallas as pl
from jax.experimental.pallas import tpu as pltpu

F32 = jnp.float32
BF16 = jnp.bfloat16

EPS = 1e-6
ROPE_THETA = 10000.0
HEAD_DIM = 64
CHUNK = 128
CONV_WIDTH = 3
F32_ROWS = 8
BF16_ROWS = 16
HALO = F32_ROWS
LANES = 128
MXU_COLS = 256
PIECE_COLS = MXU_COLS
ONES_ROWS = 16
Q_SCALE = HEAD_DIM ** -0.5 * math.log2(math.e)

VMEM_LIMIT = 56 * 1024 * 1024


def _rms(x, g):
    return x * lax.rsqrt(jnp.mean(x * x, axis=-1, keepdims=True) + EPS) * g


def _const_spec(shape):
    nd = len(shape)
    return pl.BlockSpec(shape, lambda *_: (0,) * nd, pipeline_mode=pl.Buffered(1))


def _params(n_axes):
    return pltpu.CompilerParams(dimension_semantics=("arbitrary",) * n_axes, vmem_limit_bytes=VMEM_LIMIT)


def _attn_pre_kernel(x_ref, g_ref, w_ref, cos_ref, sin_ref, qt_ref, k_ref, vt_ref):
    tm, d = x_ref.shape
    hd = 2 * HEAD_DIM
    h = _rms(x_ref[...], g_ref[...]).astype(BF16)
    qkv = jnp.dot(h, w_ref[...], preferred_element_type=F32)
    lane = lax.broadcasted_iota(jnp.int32, (tm, LANES), 1)
    first_half = (lane % HEAD_DIM) < (HEAD_DIM // 2)

    def rope(t, cos, sin):
        rot = jnp.where(first_half, pltpu.roll(t, LANES - HEAD_DIM // 2, 1), pltpu.roll(t, HEAD_DIM // 2, 1))
        return t * cos + rot * sin

    cos, sin = cos_ref[...], sin_ref[...]
    for j in range(d // hd):
        cols = slice(j * hd, (j + 1) * hd)
        qt_ref[j] = (rope(qkv[:, j * hd:(j + 1) * hd], cos, sin) * Q_SCALE).T.astype(BF16)
        k_ref[:, cols] = rope(qkv[:, d + j * hd:d + (j + 1) * hd], cos, sin).astype(BF16)
        vt_ref[j, 0:hd, :] = qkv[:, 2 * d + j * hd:2 * d + (j + 1) * hd].T.astype(BF16)
        vt_ref[j, hd:hd + ONES_ROWS, :] = jnp.ones((ONES_ROWS, tm), BF16)


def _attn_pre(x, g, w_qkv, tables, seq, tm):
    n, d = x.shape
    hd = 2 * HEAD_DIM
    heads = d // hd
    tps = seq // tm
    tab_spec = pl.BlockSpec((tm, LANES), lambda i: (i % tps, 0))
    return pl.pallas_call(
        _attn_pre_kernel,
        grid=(n // tm,),
        in_specs=[pl.BlockSpec((tm, d), lambda i: (i, 0)), _const_spec((1, d)), _const_spec((d, 3 * d)),
                  tab_spec, tab_spec],
        out_specs=[pl.BlockSpec((heads, hd, tm), lambda i: (0, 0, i)),
                   pl.BlockSpec((tm, d), lambda i: (i, 0)),
                   pl.BlockSpec((heads, hd + ONES_ROWS, tm), lambda i: (0, 0, i))],
        out_shape=[jax.ShapeDtypeStruct((heads, hd, n), BF16), jax.ShapeDtypeStruct((n, d), BF16),
                   jax.ShapeDtypeStruct((heads, hd + ONES_ROWS, n), BF16)],
        compiler_params=_params(1),
        name="attn_pre",
    )(x, g, w_qkv, *tables)


def _flash_kernel(qt_ref, k_ref, vt_ref, lq1_ref, lk1_ref, lq2_ref, lk2_ref, g_ref, o_ref,
                  q2t_s, m_s, acc_s, *bufs, tk, group, lambda_init):
    hd, tq = qt_ref.shape[1], qt_ref.shape[2]
    seq = k_ref.shape[1]
    qt = qt_ref[0]
    row = lax.broadcasted_iota(jnp.int32, qt.shape, 0)
    zero = jnp.zeros_like(qt)
    q2t_s[:, 0:tq] = jnp.where(row < HEAD_DIM, qt, zero)
    q2t_s[:, tq:2 * tq] = jnp.where(row >= HEAD_DIM, qt, zero)
    m_s[...] = jnp.full(m_s.shape, -jnp.inf, F32)
    acc_s[...] = jnp.zeros(acc_s.shape, F32)
    slots = 2 * group
    s_bufs, p_bufs, x_bufs, a_bufs = (bufs[i * slots:(i + 1) * slots] for i in range(4))
    n_groups = seq // (tk * group)
    slabs_per_piece = PIECE_COLS // LANES
    vrows = vt_ref.shape[1]

    def work(values=None, softmax_half=None, scores=None):
        if softmax_half is not None:
            slots_m = range(softmax_half * group, (softmax_half + 1) * group)
            m_old = m_s[...]
            m_new = functools.reduce(jnp.maximum, [x_bufs[slot][...] for slot in slots_m], m_old)
            a_bufs[softmax_half][...] = jnp.exp2(m_old - m_new)
            m_s[...] = m_new
        if values is not None:
            slots_v = range(values[1] * group, (values[1] + 1) * group)
            start_v = pl.multiple_of(values[0] * group * tk, group * tk)
        for c in range(2 * tq // PIECE_COLS):
            cols = slice(c * PIECE_COLS, (c + 1) * PIECE_COLS)
            slabs = range(c * slabs_per_piece, (c + 1) * slabs_per_piece)
            if values is not None:
                p = jnp.concatenate([jnp.concatenate([p_bufs[slot][j, 0:tk] for slot in slots_v], axis=0)
                                     for j in slabs], axis=1)
                pv = jnp.dot(vt_ref[0, :, pl.ds(start_v, group * tk)], p, preferred_element_type=F32)
                for i, j in enumerate(slabs):
                    alpha = a_bufs[values[1]][:, j * LANES:(j + 1) * LANES]
                    acc_s[j, 0:vrows] = alpha * acc_s[j, 0:vrows] + pv[:, i * LANES:(i + 1) * LANES]
            for g in range(group):
                if softmax_half is not None:
                    slot_m = softmax_half * group + g
                    for j in slabs:
                        lcols = slice(j * LANES, (j + 1) * LANES)
                        p_bufs[slot_m][j, 0:tk] = jnp.exp2(s_bufs[slot_m][j, 0:tk] - m_new[:, lcols]).astype(BF16)
                if scores is not None:
                    slot_s = scores[1] * group + g
                    start_s = pl.multiple_of((scores[0] * group + g) * tk, tk)
                    s = jnp.dot(k_ref[0, pl.ds(start_s, tk), :], q2t_s[:, cols], preferred_element_type=F32)
                    for i, j in enumerate(slabs):
                        s_bufs[slot_s][j, 0:tk] = s[:, i * LANES:(i + 1) * LANES]
                    x_bufs[slot_s][:, cols] = jnp.max(s, axis=0, keepdims=True)

    def step(b, half):
        work(values=(b, half), softmax_half=1 - half, scores=(b + 2, half))

    work(scores=(0, 0))
    work(softmax_half=0, scores=(1, 1))

    def body(b, carry):
        for half in range(2):
            pl.when(b % 2 == half)(functools.partial(step, b, half))
        return carry

    lax.fori_loop(0, n_groups - 2, body, 0)
    work(values=(n_groups - 2, n_groups % 2), softmax_half=(n_groups - 1) % 2)
    work(values=(n_groups - 1, (n_groups - 1) % 2))

    acc = jnp.concatenate([acc_s[j, 0:vrows] for j in range(2 * tq // LANES)], axis=1)
    o_maps = acc[0:hd] / acc[hd:hd + 1]
    lam = (jnp.exp(jnp.sum(lq1_ref[...] * lk1_ref[...], axis=-1, keepdims=True))
           - jnp.exp(jnp.sum(lq2_ref[...] * lk2_ref[...], axis=-1, keepdims=True)) + lambda_init)
    o = (o_maps[:, 0:tq] - lam * o_maps[:, tq:2 * tq]).T
    o_ref[0] = (_rms(o, g_ref[...]) * (1.0 - lambda_init)).astype(o_ref.dtype)


def _flash(qt, k, vt, lq1, lk1, lq2, lk2, subln_g, lambda_init, tq, tk, group):
    b, seq, d = k.shape
    heads, hd, _ = qt.shape
    vrows = vt.shape[1]
    qtiles = seq // tq
    slots = 2 * group
    assert seq % (tk * group) == 0 and seq // (tk * group) >= 2, (seq, tk, group)
    small = _const_spec((1, HEAD_DIM))
    return pl.pallas_call(
        functools.partial(_flash_kernel, tk=tk, group=group, lambda_init=lambda_init),
        grid=(b, heads, qtiles),
        in_specs=[pl.BlockSpec((1, hd, tq), lambda bi, h, qi: (h, 0, bi * qtiles + qi)),
                  pl.BlockSpec((1, seq, hd), lambda bi, h, qi: (bi, 0, h)),
                  pl.BlockSpec((1, vrows, seq), lambda bi, h, qi: (h, 0, bi)),
                  small, small, small, small, _const_spec((1, hd))],
        out_specs=pl.BlockSpec((1, tq, hd), lambda bi, h, qi: (bi, qi, h)),
        out_shape=jax.ShapeDtypeStruct((b, seq, d), BF16),
        scratch_shapes=[pltpu.VMEM((hd, 2 * tq), BF16), pltpu.VMEM((1, 2 * tq), F32),
                        pltpu.VMEM((2 * tq // LANES, vrows + F32_ROWS, LANES), F32)]
        + [pltpu.VMEM((2 * tq // LANES, tk + F32_ROWS, LANES), F32)] * slots
        + [pltpu.VMEM((2 * tq // LANES, tk + BF16_ROWS, LANES), BF16)] * slots
        + [pltpu.VMEM((1, 2 * tq), F32)] * slots
        + [pltpu.VMEM((1, 2 * tq), F32)] * 2,
        compiler_params=_params(3),
        name="flash_diff_attn",
    )(qt, k, vt, lq1, lk1, lq2, lk2, subln_g)


def _attn_post_kernel(x_ref, o_ref, w_ref, y_ref):
    y_ref[...] = x_ref[...] + jnp.dot(o_ref[...], w_ref[...], preferred_element_type=F32)


def _attn_post(x, o, w_o, tm):
    n, d = x.shape
    row = pl.BlockSpec((tm, d), lambda i: (i, 0))
    return pl.pallas_call(
        _attn_post_kernel,
        grid=(n // tm,),
        in_specs=[row, row, _const_spec((d, d))],
        out_specs=row,
        out_shape=jax.ShapeDtypeStruct((n, d), F32),
        compiler_params=_params(1),
        name="attn_post",
    )(x, o, w_o)


def _ffn_ple_kernel(x_ref, xp_ref, xn_ref, p_ref, gf_ref, wup_ref, cw_ref, cb_ref, wdn_ref,
                    gp_ref, wg_ref, wp_ref, gfin_ref, y_ref, a_s, act_s, *, tiles_per_seq, fc, final):
    tm, d = x_ref.shape
    f = wdn_ref.shape[0]
    i = pl.program_id(0)
    keep_prev = jnp.where(i % tiles_per_seq == 0, 0.0, 1.0).astype(F32)
    keep_next = jnp.where(i % tiles_per_seq == tiles_per_seq - 1, 0.0, 1.0).astype(F32)
    x = x_ref[...]
    gf = gf_ref[...]
    h = jnp.concatenate([_rms(xp_ref[...], gf) * keep_prev, _rms(x, gf), _rms(xn_ref[...], gf) * keep_next], axis=0)
    a_s[...] = jnp.dot(h.astype(BF16), wup_ref[...], preferred_element_type=F32)

    def conv(col0):
        cols = pl.ds(col0, fc)
        c = cb_ref[:, cols]
        for t in range(CONV_WIDTH):
            c = c + a_s[pl.ds(HALO - 1 + t, tm), cols] * cw_ref[pl.ds(t, 1), cols]
        return c

    for j in range(f // fc):
        val = conv(j * fc)
        gate = conv(f + j * fc)
        act_s[:, pl.ds(j * fc, fc)] = (gate * jax.nn.sigmoid(gate) * val).astype(BF16)

    x = x + jnp.dot(act_s[...], wdn_ref[...], preferred_element_type=F32)
    gate = jax.nn.sigmoid(jnp.dot(_rms(x, gp_ref[...]).astype(BF16), wg_ref[...], preferred_element_type=F32))
    x = x + gate * jnp.dot(p_ref[...].astype(BF16), wp_ref[...], preferred_element_type=F32)
    if final:
        x = _rms(x, gfin_ref[...])
    y_ref[...] = x


def _ffn_ple(x, p, g_ffn, w_up, conv_w, conv_b, w_down, g_ple, w_gate, w_proj, g_final, layer, seq, tm, final):
    n, d = x.shape
    f = w_down.shape[1]
    pd = p.shape[2]

    def layer_spec(*shape):
        return pl.BlockSpec((None,) + shape, lambda i: (layer,) + (0,) * len(shape), pipeline_mode=pl.Buffered(1))

    fc = 2 * LANES
    tiles_per_seq = seq // tm
    hb = tm // HALO
    last_blk = n // HALO - 1
    return pl.pallas_call(
        functools.partial(_ffn_ple_kernel, tiles_per_seq=tiles_per_seq, fc=fc, final=final),
        grid=(n // tm,),
        in_specs=[pl.BlockSpec((tm, d), lambda i: (i, 0)),
                  pl.BlockSpec((HALO, d), lambda i: (jnp.maximum(i * hb - 1, 0), 0)),
                  pl.BlockSpec((HALO, d), lambda i: (jnp.minimum((i + 1) * hb, last_blk), 0)),
                  pl.BlockSpec((None, tm, pd), lambda i: (layer, i, 0)),
                  layer_spec(1, d), layer_spec(d, 2 * f), layer_spec(CONV_WIDTH, 2 * f),
                  layer_spec(1, 2 * f), layer_spec(f, d), layer_spec(1, d), layer_spec(d, d),
                  layer_spec(pd, d), _const_spec((1, d))],
        out_specs=pl.BlockSpec((tm, d), lambda i: (i, 0)),
        out_shape=jax.ShapeDtypeStruct((n, d), F32),
        scratch_shapes=[pltpu.VMEM((tm + 2 * HALO, 2 * f), F32), pltpu.VMEM((tm, f), BF16)],
        compiler_params=_params(1),
        name="ffn_ple_final" if final else "ffn_ple",
    )(x, x, x, p, g_ffn, w_up, conv_w, conv_b, w_down, g_ple, w_gate, w_proj, g_final)


def _sgu_kernel(x_ref, g_ref, wuv_ref, lng_ref, lnb_ref, ws_ref, bs_ref, wout_ref, y_ref, um_s):
    tm, d = x_ref.shape
    w = wout_ref.shape[0]
    x = x_ref[...]
    z = jnp.dot(_rms(x, g_ref[...]).astype(BF16), wuv_ref[...], preferred_element_type=F32)
    z = 0.5 * z * (1.0 + lax.erf(z * (1.0 / math.sqrt(2.0))))
    v = z[:, w:]
    vc = v - jnp.mean(v, axis=-1, keepdims=True)
    v = vc * lax.rsqrt(jnp.mean(vc * vc, axis=-1, keepdims=True) + EPS) * lng_ref[...] + lnb_ref[...]
    vb = v.astype(BF16)
    for c in range(tm // CHUNK):
        rows = slice(c * CHUNK, (c + 1) * CHUNK)
        for grp in range(w // CHUNK):
            cols = slice(grp * CHUNK, (grp + 1) * CHUNK)
            mixed = jnp.dot(ws_ref[grp], vb[rows, cols], preferred_element_type=F32) + bs_ref[grp]
            um_s[rows, cols] = (z[rows, cols] * mixed).astype(BF16)
    y_ref[...] = x + jnp.dot(um_s[...], wout_ref[...], preferred_element_type=F32)


def _sgu(x, g, w_uv, ln_g, ln_b, w_s, b_s, w_out, tm):
    n, d = x.shape
    w = w_out.shape[0]
    ng = w // CHUNK
    row = pl.BlockSpec((tm, d), lambda i: (i, 0))
    return pl.pallas_call(
        _sgu_kernel,
        grid=(n // tm,),
        in_specs=[row, _const_spec((1, d)), _const_spec((d, 2 * w)), _const_spec((1, w)), _const_spec((1, w)),
                  _const_spec((ng, CHUNK, CHUNK)), _const_spec((ng, CHUNK, 1)), _const_spec((w, d))],
        out_specs=row,
        out_shape=jax.ShapeDtypeStruct((n, d), F32),
        scratch_shapes=[pltpu.VMEM((tm, w), BF16)],
        compiler_params=_params(1),
        name="sgu",
    )(x, g, w_uv, ln_g, ln_b, w_s, b_s, w_out)


def _rope_tables(seq):
    half = HEAD_DIM // 2
    inv = 1.0 / (ROPE_THETA ** (jnp.arange(0, HEAD_DIM, 2, dtype=F32) / HEAD_DIM))
    ang = jnp.arange(seq, dtype=F32)[:, None] * inv[None, :]
    cos, sin = jnp.cos(ang), jnp.sin(ang)
    sign = jnp.where((jnp.arange(LANES) % HEAD_DIM) < half, -1.0, 1.0).astype(F32)
    reps = LANES // half
    return jnp.concatenate([cos] * reps, axis=-1), jnp.concatenate([sin] * reps, axis=-1) * sign


def _flash_tiles(seq):
    tq = _tile(seq, 2048)
    tk = _tile(seq, LANES)
    group = max(1, min(4, seq // tk // 4))
    return tq, tk, group


def _tile(n, want):
    t = min(n, want)
    assert n % t == 0, (n, t)
    return t


def _trunk(x, p, w, rope):
    b, seq, d = x.shape
    n = b * seq
    x = x.reshape(n, d)
    tm = _tile(seq, 512)
    tm_wide = _tile(seq, 1024)
    qt, k, vt = _attn_pre(x, w["norm_mix_g"][0], w["attn_w_qkv"], rope, seq, tm_wide)
    lambda_init = 0.8 - 0.6 * math.exp(-0.3 * 0)
    o = _flash(qt, k.reshape(b, seq, d), vt,
               w["attn_lq1"], w["attn_lk1"], w["attn_lq2"], w["attn_lk2"], w["attn_subln_g"],
               lambda_init, *_flash_tiles(seq))
    x = _attn_post(x, o.reshape(n, d), w["attn_w_o"], tm_wide)
    p = p.reshape(p.shape[0], n, p.shape[-1])
    ffn = functools.partial(_ffn_ple, p=p, g_ffn=w["norm_ffn_g"], w_up=w["ffn_w_up"], conv_w=w["ffn_conv_w"],
                            conv_b=w["ffn_conv_b"], w_down=w["ffn_w_down"], g_ple=w["norm_ple_g"],
                            w_gate=w["ple_w_gate"], w_proj=w["ple_w_proj"], g_final=w["final_norm_g"], seq=seq, tm=tm)
    x = ffn(x, layer=0, final=False)
    x = _sgu(x, w["norm_mix_g"][1], w["sgu_w_uv"], w["sgu_ln_g"], w["sgu_ln_b"], w["sgu_w_s"], w["sgu_b_s"],
             w["sgu_w_out"], tm_wide)
    x = ffn(x, layer=1, final=True)
    return x.reshape(b, seq, d)


def kernel(x_prompt, x_sample, p_prompt, p_sample, norm_mix_g, attn_w_qkv, attn_lq1, attn_lk1, attn_lq2, attn_lk2, attn_subln_g, attn_w_o, sgu_w_uv, sgu_ln_g, sgu_ln_b, sgu_w_s, sgu_b_s, sgu_w_out, norm_ffn_g, ffn_w_up, ffn_conv_w, ffn_conv_b, ffn_w_down, norm_ple_g, ple_w_gate, ple_w_proj, final_norm_g):
    depth = norm_mix_g.shape[0]
    assert depth == 2 and attn_w_qkv.shape[0] == 1 and sgu_w_uv.shape[0] == 1
    d = x_prompt.shape[-1]
    row = lambda a: a.reshape(a.shape[:-1] + (1, a.shape[-1]))
    w = dict(
        norm_mix_g=row(norm_mix_g), norm_ffn_g=row(norm_ffn_g), norm_ple_g=row(norm_ple_g),
        final_norm_g=final_norm_g.reshape(1, d),
        attn_w_qkv=attn_w_qkv[0].astype(BF16), attn_w_o=attn_w_o[0].astype(BF16),
        attn_lq1=attn_lq1, attn_lk1=attn_lk1, attn_lq2=attn_lq2, attn_lk2=attn_lk2, attn_subln_g=attn_subln_g,
        sgu_w_uv=sgu_w_uv[0].astype(BF16), sgu_ln_g=sgu_ln_g, sgu_ln_b=sgu_ln_b,
        sgu_w_s=sgu_w_s[0].astype(BF16), sgu_b_s=sgu_b_s[0][:, :, None], sgu_w_out=sgu_w_out[0].astype(BF16),
        ffn_w_up=ffn_w_up.astype(BF16), ffn_conv_w=ffn_conv_w, ffn_conv_b=row(ffn_conv_b),
        ffn_w_down=ffn_w_down.astype(BF16), ple_w_gate=ple_w_gate.astype(BF16), ple_w_proj=ple_w_proj.astype(BF16),
    )
    rope = _rope_tables(max(x_prompt.shape[1], x_sample.shape[1]))
    return _trunk(x_prompt, p_prompt, w, rope), _trunk(x_sample, p_sample, w, rope)
```

```python
import functools
import math

import jax
import jax.numpy as jnp
from jax import lax
from jax.experimental import pallas as pl
from jax.experimental.pallas import tpu as pltpu

F32 = jnp.float32
BF16 = jnp.bfloat16

EPS = 1e-6
ROPE_THETA = 10000.0
HEAD_DIM = 64
CHUNK = 128
CONV_WIDTH = 3
F32_ROWS = 8
BF16_ROWS = 16
HALO = F32_ROWS
LANES = 128
MXU_COLS = 256
PIECE_COLS = MXU_COLS
ONES_ROWS = 16
Q_SCALE = HEAD_DIM ** -0.5 * math.log2(math.e)

VMEM_LIMIT = 56 * 1024 * 1024


def _rms(x, g):
    return x * lax.rsqrt(jnp.mean(x * x, axis=-1, keepdims=True) + EPS) * g


def _const_spec(shape):
    nd = len(shape)
    return pl.BlockSpec(shape, lambda *_: (0,) * nd, pipeline_mode=pl.Buffered(1))


def _params(n_axes):
    return pltpu.CompilerParams(dimension_semantics=("arbitrary",) * n_axes, vmem_limit_bytes=VMEM_LIMIT)


def _attn_pre_kernel(x_ref, g_ref, w_ref, cos_ref, sin_ref, qt_ref, k_ref, vt_ref):
    tm, d = x_ref.shape
    hd = 2 * HEAD_DIM
    h = _rms(x_ref[...], g_ref[...]).astype(BF16)
    qkv = jnp.dot(h, w_ref[...], preferred_element_type=F32)
    lane = lax.broadcasted_iota(jnp.int32, (tm, LANES), 1)
    first_half = (lane % HEAD_DIM) < (HEAD_DIM // 2)

    def rope(t, cos, sin):
        rot = jnp.where(first_half, pltpu.roll(t, LANES - HEAD_DIM // 2, 1), pltpu.roll(t, HEAD_DIM // 2, 1))
        return t * cos + rot * sin

    cos, sin = cos_ref[...], sin_ref[...]
    for j in range(d // hd):
        cols = slice(j * hd, (j + 1) * hd)
        qt_ref[j] = (rope(qkv[:, j * hd:(j + 1) * hd], cos, sin) * Q_SCALE).T.astype(BF16)
        k_ref[:, cols] = rope(qkv[:, d + j * hd:d + (j + 1) * hd], cos, sin).astype(BF16)
        vt_ref[j, 0:hd, :] = qkv[:, 2 * d + j * hd:2 * d + (j + 1) * hd].T.astype(BF16)
        vt_ref[j, hd:hd + ONES_ROWS, :] = jnp.ones((ONES_ROWS, tm), BF16)


def _attn_pre(x, g, w_qkv, tables, seq, tm):
    n, d = x.shape
    hd = 2 * HEAD_DIM
    heads = d // hd
    tps = seq // tm
    tab_spec = pl.BlockSpec((tm, LANES), lambda i: (i % tps, 0))
    return pl.pallas_call(
        _attn_pre_kernel,
        grid=(n // tm,),
        in_specs=[pl.BlockSpec((tm, d), lambda i: (i, 0)), _const_spec((1, d)), _const_spec((d, 3 * d)),
                  tab_spec, tab_spec],
        out_specs=[pl.BlockSpec((heads, hd, tm), lambda i: (0, 0, i)),
                   pl.BlockSpec((tm, d), lambda i: (i, 0)),
                   pl.BlockSpec((heads, hd + ONES_ROWS, tm), lambda i: (0, 0, i))],
        out_shape=[jax.ShapeDtypeStruct((heads, hd, n), BF16), jax.ShapeDtypeStruct((n, d), BF16),
                   jax.ShapeDtypeStruct((heads, hd + ONES_ROWS, n), BF16)],
        compiler_params=_params(1),
        name="attn_pre",
    )(x, g, w_qkv, *tables)


def _flash_kernel(qt_ref, k_ref, vt_ref, lq1_ref, lk1_ref, lq2_ref, lk2_ref, g_ref, o_ref,
                  q2t_s, m_s, acc_s, *bufs, tk, group, lambda_init):
    hd, tq = qt_ref.shape[1], qt_ref.shape[2]
    seq = k_ref.shape[1]
    qt = qt_ref[0]
    row = lax.broadcasted_iota(jnp.int32, qt.shape, 0)
    zero = jnp.zeros_like(qt)
    q2t_s[:, 0:tq] = jnp.where(row < HEAD_DIM, qt, zero)
    q2t_s[:, tq:2 * tq] = jnp.where(row >= HEAD_DIM, qt, zero)
    m_s[...] = jnp.full(m_s.shape, -jnp.inf, F32)
    acc_s[...] = jnp.zeros(acc_s.shape, F32)
    slots = 2 * group
    s_bufs, p_bufs, x_bufs, a_bufs = (bufs[i * slots:(i + 1) * slots] for i in range(4))
    n_groups = seq // (tk * group)
    slabs_per_piece = PIECE_COLS // LANES
    vrows = vt_ref.shape[1]

    def work(values=None, softmax_half=None, scores=None):
        if softmax_half is not None:
            slots_m = range(softmax_half * group, (softmax_half + 1) * group)
            m_old = m_s[...]
            m_new = functools.reduce(jnp.maximum, [x_bufs[slot][...] for slot in slots_m], m_old)
            a_bufs[softmax_half][...] = jnp.exp2(m_old - m_new)
            m_s[...] = m_new
        if values is not None:
            slots_v = range(values[1] * group, (values[1] + 1) * group)
            start_v = pl.multiple_of(values[0] * group * tk, group * tk)
        for c in range(2 * tq // PIECE_COLS):
            cols = slice(c * PIECE_COLS, (c + 1) * PIECE_COLS)
            slabs = range(c * slabs_per_piece, (c + 1) * slabs_per_piece)
            if values is not None:
                p = jnp.concatenate([jnp.concatenate([p_bufs[slot][j, 0:tk] for slot in slots_v], axis=0)
                                     for j in slabs], axis=1)
                pv = jnp.dot(vt_ref[0, :, pl.ds(start_v, group * tk)], p, preferred_element_type=F32)
                for i, j in enumerate(slabs):
                    alpha = a_bufs[values[1]][:, j * LANES:(j + 1) * LANES]
                    acc_s[j, 0:vrows] = alpha * acc_s[j, 0:vrows] + pv[:, i * LANES:(i + 1) * LANES]
            for g in range(group):
                if softmax_half is not None:
                    slot_m = softmax_half * group + g
                    for j in slabs:
                        lcols = slice(j * LANES, (j + 1) * LANES)
                        p_bufs[slot_m][j, 0:tk] = jnp.exp2(s_bufs[slot_m][j, 0:tk] - m_new[:, lcols]).astype(BF16)
                if scores is not None:
                    slot_s = scores[1] * group + g
                    start_s = pl.multiple_of((scores[0] * group + g) * tk, tk)
                    s = jnp.dot(k_ref[0, pl.ds(start_s, tk), :], q2t_s[:, cols], preferred_element_type=F32)
                    for i, j in enumerate(slabs):
                        s_bufs[slot_s][j, 0:tk] = s[:, i * LANES:(i + 1) * LANES]
                    x_bufs[slot_s][:, cols] = jnp.max(s, axis=0, keepdims=True)

    def step(b, half):
        work(values=(b, half), softmax_half=1 - half, scores=(b + 2, half))

    work(scores=(0, 0))
    work(softmax_half=0, scores=(1, 1))

    def body(b, carry):
        for half in range(2):
            pl.when(b % 2 == half)(functools.partial(step, b, half))
        return carry

    lax.fori_loop(0, n_groups - 2, body, 0)
    work(values=(n_groups - 2, n_groups % 2), softmax_half=(n_groups - 1) % 2)
    work(values=(n_groups - 1, (n_groups - 1) % 2))

    acc = jnp.concatenate([acc_s[j, 0:vrows] for j in range(2 * tq // LANES)], axis=1)
    o_maps = acc[0:hd] / acc[hd:hd + 1]
    lam = (jnp.exp(jnp.sum(lq1_ref[...] * lk1_ref[...], axis=-1, keepdims=True))
           - jnp.exp(jnp.sum(lq2_ref[...] * lk2_ref[...], axis=-1, keepdims=True)) + lambda_init)
    o = (o_maps[:, 0:tq] - lam * o_maps[:, tq:2 * tq]).T
    o_ref[0] = (_rms(o, g_ref[...]) * (1.0 - lambda_init)).astype(o_ref.dtype)


def _flash(qt, k, vt, lq1, lk1, lq2, lk2, subln_g, lambda_init, tq, tk, group):
    b, seq, d = k.shape
    heads, hd, _ = qt.shape
    vrows = vt.shape[1]
    qtiles = seq // tq
    slots = 2 * group
    assert seq % (tk * group) == 0 and seq // (tk * group) >= 2, (seq, tk, group)
    small = _const_spec((1, HEAD_DIM))
    return pl.pallas_call(
        functools.partial(_flash_kernel, tk=tk, group=group, lambda_init=lambda_init),
        grid=(b, heads, qtiles),
        in_specs=[pl.BlockSpec((1, hd, tq), lambda bi, h, qi: (h, 0, bi * qtiles + qi)),
                  pl.BlockSpec((1, seq, hd), lambda bi, h, qi: (bi, 0, h)),
                  pl.BlockSpec((1, vrows, seq), lambda bi, h, qi: (h, 0, bi)),
                  small, small, small, small, _const_spec((1, hd))],
        out_specs=pl.BlockSpec((1, tq, hd), lambda bi, h, qi: (bi, qi, h)),
        out_shape=jax.ShapeDtypeStruct((b, seq, d), BF16),
        scratch_shapes=[pltpu.VMEM((hd, 2 * tq), BF16), pltpu.VMEM((1, 2 * tq), F32),
                        pltpu.VMEM((2 * tq // LANES, vrows + F32_ROWS, LANES), F32)]
        + [pltpu.VMEM((2 * tq // LANES, tk + F32_ROWS, LANES), F32)] * slots
        + [pltpu.VMEM((2 * tq // LANES, tk + BF16_ROWS, LANES), BF16)] * slots
        + [pltpu.VMEM((1, 2 * tq), F32)] * slots
        + [pltpu.VMEM((1, 2 * tq), F32)] * 2,
        compiler_params=_params(3),
        name="flash_diff_attn",
    )(qt, k, vt, lq1, lk1, lq2, lk2, subln_g)


def _attn_post_kernel(x_ref, o_ref, w_ref, y_ref):
    y_ref[...] = x_ref[...] + jnp.dot(o_ref[...], w_ref[...], preferred_element_type=F32)


def _attn_post(x, o, w_o, tm):
    n, d = x.shape
    row = pl.BlockSpec((tm, d), lambda i: (i, 0))
    return pl.pallas_call(
        _attn_post_kernel,
        grid=(n // tm,),
        in_specs=[row, row, _const_spec((d, d))],
        out_specs=row,
        out_shape=jax.ShapeDtypeStruct((n, d), F32),
        compiler_params=_params(1),
        name="attn_post",
    )(x, o, w_o)


def _ffn_ple_kernel(x_ref, xp_ref, xn_ref, p_ref, gf_ref, wup_ref, cw_ref, cb_ref, wdn_ref,
                    gp_ref, wg_ref, wp_ref, gfin_ref, y_ref, a_s, act_s, *, tiles_per_seq, fc, final):
    tm, d = x_ref.shape
    f = wdn_ref.shape[0]
    i = pl.program_id(0)
    keep_prev = jnp.where(i % tiles_per_seq == 0, 0.0, 1.0).astype(F32)
    keep_next = jnp.where(i % tiles_per_seq == tiles_per_seq - 1, 0.0, 1.0).astype(F32)
    x = x_ref[...]
    gf = gf_ref[...]
    h = jnp.concatenate([_rms(xp_ref[...], gf) * keep_prev, _rms(x, gf), _rms(xn_ref[...], gf) * keep_next], axis=0)
    a_s[...] = jnp.dot(h.astype(BF16), wup_ref[...], preferred_element_type=F32)

    def conv(col0):
        cols = pl.ds(col0, fc)
        c = cb_ref[:, cols]
        for t in range(CONV_WIDTH):
            c = c + a_s[pl.ds(HALO - 1 + t, tm), cols] * cw_ref[pl.ds(t, 1), cols]
        return c

    for j in range(f // fc):
        val = conv(j * fc)
        gate = conv(f + j * fc)
        act_s[:, pl.ds(j * fc, fc)] = (gate * jax.nn.sigmoid(gate) * val).astype(BF16)

    x = x + jnp.dot(act_s[...], wdn_ref[...], preferred_element_type=F32)
    gate = jax.nn.sigmoid(jnp.dot(_rms(x, gp_ref[...]).astype(BF16), wg_ref[...], preferred_element_type=F32))
    x = x + gate * jnp.dot(p_ref[...].astype(BF16), wp_ref[...], preferred_element_type=F32)
    if final:
        x = _rms(x, gfin_ref[...])
    y_ref[...] = x


def _ffn_ple(x, p, g_ffn, w_up, conv_w, conv_b, w_down, g_ple, w_gate, w_proj, g_final, layer, seq, tm, final):
    n, d = x.shape
    f = w_down.shape[1]
    pd = p.shape[2]

    def layer_spec(*shape):
        return pl.BlockSpec((None,) + shape, lambda i: (layer,) + (0,) * len(shape), pipeline_mode=pl.Buffered(1))

    fc = 2 * LANES
    tiles_per_seq = seq // tm
    hb = tm // HALO
    last_blk = n // HALO - 1
    return pl.pallas_call(
        functools.partial(_ffn_ple_kernel, tiles_per_seq=tiles_per_seq, fc=fc, final=final),
        grid=(n // tm,),
        in_specs=[pl.BlockSpec((tm, d), lambda i: (i, 0)),
                  pl.BlockSpec((HALO, d), lambda i: (jnp.maximum(i * hb - 1, 0), 0)),
                  pl.BlockSpec((HALO, d), lambda i: (jnp.minimum((i + 1) * hb, last_blk), 0)),
                  pl.BlockSpec((None, tm, pd), lambda i: (layer, i, 0)),
                  layer_spec(1, d), layer_spec(d, 2 * f), layer_spec(CONV_WIDTH, 2 * f),
                  layer_spec(1, 2 * f), layer_spec(f, d), layer_spec(1, d), layer_spec(d, d),
                  layer_spec(pd, d), _const_spec((1, d))],
        out_specs=pl.BlockSpec((tm, d), lambda i: (i, 0)),
        out_shape=jax.ShapeDtypeStruct((n, d), F32),
        scratch_shapes=[pltpu.VMEM((tm + 2 * HALO, 2 * f), F32), pltpu.VMEM((tm, f), BF16)],
        compiler_params=_params(1),
        name="ffn_ple_final" if final else "ffn_ple",
    )(x, x, x, p, g_ffn, w_up, conv_w, conv_b, w_down, g_ple, w_gate, w_proj, g_final)


def _sgu_kernel(x_ref, g_ref, wuv_ref, lng_ref, lnb_ref, ws_ref, bs_ref, wout_ref, y_ref, um_s):
    tm, d = x_ref.shape
    w = wout_ref.shape[0]
    x = x_ref[...]
    z = jnp.dot(_rms(x, g_ref[...]).astype(BF16), wuv_ref[...], preferred_element_type=F32)
    z = 0.5 * z * (1.0 + lax.erf(z * (1.0 / math.sqrt(2.0))))
    v = z[:, w:]
    vc = v - jnp.mean(v, axis=-1, keepdims=True)
    v = vc * lax.rsqrt(jnp.mean(vc * vc, axis=-1, keepdims=True) + EPS) * lng_ref[...] + lnb_ref[...]
    vb = v.astype(BF16)
    for c in range(tm // CHUNK):
        rows = slice(c * CHUNK, (c + 1) * CHUNK)
        for grp in range(w // CHUNK):
            cols = slice(grp * CHUNK, (grp + 1) * CHUNK)
            mixed = jnp.dot(ws_ref[grp], vb[rows, cols], preferred_element_type=F32) + bs_ref[grp]
            um_s[rows, cols] = (z[rows, cols] * mixed).astype(BF16)
    y_ref[...] = x + jnp.dot(um_s[...], wout_ref[...], preferred_element_type=F32)


def _sgu(x, g, w_uv, ln_g, ln_b, w_s, b_s, w_out, tm):
    n, d = x.shape
    w = w_out.shape[0]
    ng = w // CHUNK
    row = pl.BlockSpec((tm, d), lambda i: (i, 0))
    return pl.pallas_call(
        _sgu_kernel,
        grid=(n // tm,),
        in_specs=[row, _const_spec((1, d)), _const_spec((d, 2 * w)), _const_spec((1, w)), _const_spec((1, w)),
                  _const_spec((ng, CHUNK, CHUNK)), _const_spec((ng, CHUNK, 1)), _const_spec((w, d))],
        out_specs=row,
        out_shape=jax.ShapeDtypeStruct((n, d), F32),
        scratch_shapes=[pltpu.VMEM((tm, w), BF16)],
        compiler_params=_params(1),
        name="sgu",
    )(x, g, w_uv, ln_g, ln_b, w_s, b_s, w_out)


def _rope_tables(seq):
    half = HEAD_DIM // 2
    inv = 1.0 / (ROPE_THETA ** (jnp.arange(0, HEAD_DIM, 2, dtype=F32) / HEAD_DIM))
    ang = jnp.arange(seq, dtype=F32)[:, None] * inv[None, :]
    cos, sin = jnp.cos(ang), jnp.sin(ang)
    sign = jnp.where((jnp.arange(LANES) % HEAD_DIM) < half, -1.0, 1.0).astype(F32)
    reps = LANES // half
    return jnp.concatenate([cos] * reps, axis=-1), jnp.concatenate([sin] * reps, axis=-1) * sign


def _flash_tiles(seq):
    tq = _tile(seq, 2048)
    tk = _tile(seq, LANES)
    group = max(1, min(4, seq // tk // 4))
    return tq, tk, group


def _tile(n, want):
    t = min(n, want)
    assert n % t == 0, (n, t)
    return t


def _trunk(x, p, w, rope):
    b, seq, d = x.shape
    n = b * seq
    x = x.reshape(n, d)
    tm = _tile(seq, 512)
    tm_wide = _tile(seq, 1024)
    qt, k, vt = _attn_pre(x, w["norm_mix_g"][0], w["attn_w_qkv"], rope, seq, tm_wide)
    lambda_init = 0.8 - 0.6 * math.exp(-0.3 * 0)
    o = _flash(qt, k.reshape(b, seq, d), vt,
               w["attn_lq1"], w["attn_lk1"], w["attn_lq2"], w["attn_lk2"], w["attn_subln_g"],
               lambda_init, *_flash_tiles(seq))
    x = _attn_post(x, o.reshape(n, d), w["attn_w_o"], _tile(seq, 2048))
    p = p.reshape(p.shape[0], n, p.shape[-1])
    ffn = functools.partial(_ffn_ple, p=p, g_ffn=w["norm_ffn_g"], w_up=w["ffn_w_up"], conv_w=w["ffn_conv_w"],
                            conv_b=w["ffn_conv_b"], w_down=w["ffn_w_down"], g_ple=w["norm_ple_g"],
                            w_gate=w["ple_w_gate"], w_proj=w["ple_w_proj"], g_final=w["final_norm_g"], seq=seq, tm=tm)
    x = ffn(x, layer=0, final=False)
    x = _sgu(x, w["norm_mix_g"][1], w["sgu_w_uv"], w["sgu_ln_g"], w["sgu_ln_b"], w["sgu_w_s"], w["sgu_b_s"],
             w["sgu_w_out"], tm_wide)
    x = ffn(x, layer=1, final=True)
    return x.reshape(b, seq, d)


def kernel(x_prompt, x_sample, p_prompt, p_sample, norm_mix_g, attn_w_qkv, attn_lq1, attn_lk1, attn_lq2, attn_lk2, attn_subln_g, attn_w_o, sgu_w_uv, sgu_ln_g, sgu_ln_b, sgu_w_s, sgu_b_s, sgu_w_out, norm_ffn_g, ffn_w_up, ffn_conv_w, ffn_conv_b, ffn_w_down, norm_ple_g, ple_w_gate, ple_w_proj, final_norm_g):
    depth = norm_mix_g.shape[0]
    assert depth == 2 and attn_w_qkv.shape[0] == 1 and sgu_w_uv.shape[0] == 1
    d = x_prompt.shape[-1]
    row = lambda a: a.reshape(a.shape[:-1] + (1, a.shape[-1]))
    w = dict(
        norm_mix_g=row(norm_mix_g), norm_ffn_g=row(norm_ffn_g), norm_ple_g=row(norm_ple_g),
        final_norm_g=final_norm_g.reshape(1, d),
        attn_w_qkv=attn_w_qkv[0].astype(BF16), attn_w_o=attn_w_o[0].astype(BF16),
        attn_lq1=attn_lq1, attn_lk1=attn_lk1, attn_lq2=attn_lq2, attn_lk2=attn_lk2, attn_subln_g=attn_subln_g,
        sgu_w_uv=sgu_w_uv[0].astype(BF16), sgu_ln_g=sgu_ln_g, sgu_ln_b=sgu_ln_b,
        sgu_w_s=sgu_w_s[0].astype(BF16), sgu_b_s=sgu_b_s[0][:, :, None], sgu_w_out=sgu_w_out[0].astype(BF16),
        ffn_w_up=ffn_w_up.astype(BF16), ffn_conv_w=ffn_conv_w, ffn_conv_b=row(ffn_conv_b),
        ffn_w_down=ffn_w_down.astype(BF16), ple_w_gate=ple_w_gate.astype(BF16), ple_w_proj=ple_w_proj.astype(BF16),
    )
    rope = _rope_tables(max(x_prompt.shape[1], x_sample.shape[1]))
    return _trunk(x_prompt, p_prompt, w, rope), _trunk(x_sample, p_sample, w, rope)
```

```python
import functools
import math

import jax
import jax.numpy as jnp
from jax import lax
from jax.experimental import pallas as pl
from jax.experimental.pallas import tpu as pltpu

F32 = jnp.float32
BF16 = jnp.bfloat16

EPS = 1e-6
ROPE_THETA = 10000.0
HEAD_DIM = 64
CHUNK = 128
CONV_WIDTH = 3
F32_ROWS = 8
BF16_ROWS = 16
HALO = F32_ROWS
LANES = 128
MXU_COLS = 256
PIECE_COLS = MXU_COLS
SHORT_SEQ = 2048
VALUE_CHUNKS = 4
ONES_ROWS = 16
Q_SCALE = HEAD_DIM ** -0.5 * math.log2(math.e)

VMEM_LIMIT = 56 * 1024 * 1024


def _rms(x, g):
    return x * lax.rsqrt(jnp.mean(x * x, axis=-1, keepdims=True) + EPS) * g


def _const_spec(shape):
    nd = len(shape)
    return pl.BlockSpec(shape, lambda *_: (0,) * nd, pipeline_mode=pl.Buffered(1))


def _params(n_axes):
    return pltpu.CompilerParams(dimension_semantics=("arbitrary",) * n_axes, vmem_limit_bytes=VMEM_LIMIT)


def _attn_pre_kernel(x_ref, g_ref, w_ref, cos_ref, sin_ref, qt_ref, k_ref, vt_ref):
    tm, d = x_ref.shape
    hd = 2 * HEAD_DIM
    h = _rms(x_ref[...], g_ref[...]).astype(BF16)
    qkv = jnp.dot(h, w_ref[...], preferred_element_type=F32)
    lane = lax.broadcasted_iota(jnp.int32, (tm, LANES), 1)
    first_half = (lane % HEAD_DIM) < (HEAD_DIM // 2)

    def rope(t, cos, sin):
        rot = jnp.where(first_half, pltpu.roll(t, LANES - HEAD_DIM // 2, 1), pltpu.roll(t, HEAD_DIM // 2, 1))
        return t * cos + rot * sin

    cos, sin = cos_ref[...], sin_ref[...]
    for j in range(d // hd):
        cols = slice(j * hd, (j + 1) * hd)
        qt_ref[j] = (rope(qkv[:, j * hd:(j + 1) * hd], cos, sin) * Q_SCALE).T.astype(BF16)
        k_ref[:, cols] = rope(qkv[:, d + j * hd:d + (j + 1) * hd], cos, sin).astype(BF16)
        vt_ref[j, 0:hd, :] = qkv[:, 2 * d + j * hd:2 * d + (j + 1) * hd].T.astype(BF16)
        vt_ref[j, hd:hd + ONES_ROWS, :] = jnp.ones((ONES_ROWS, tm), BF16)


def _attn_pre(x, g, w_qkv, tables, seq, tm):
    n, d = x.shape
    hd = 2 * HEAD_DIM
    heads = d // hd
    tps = seq // tm
    tab_spec = pl.BlockSpec((tm, LANES), lambda i: (i % tps, 0))
    return pl.pallas_call(
        _attn_pre_kernel,
        grid=(n // tm,),
        in_specs=[pl.BlockSpec((tm, d), lambda i: (i, 0)), _const_spec((1, d)), _const_spec((d, 3 * d)),
                  tab_spec, tab_spec],
        out_specs=[pl.BlockSpec((heads, hd, tm), lambda i: (0, 0, i)),
                   pl.BlockSpec((tm, d), lambda i: (i, 0)),
                   pl.BlockSpec((heads, hd + ONES_ROWS, tm), lambda i: (0, 0, i))],
        out_shape=[jax.ShapeDtypeStruct((heads, hd, n), BF16), jax.ShapeDtypeStruct((n, d), BF16),
                   jax.ShapeDtypeStruct((heads, hd + ONES_ROWS, n), BF16)],
        compiler_params=_params(1),
        name="attn_pre",
    )(x, g, w_qkv, *tables)


def _flash_kernel(qt_ref, k_ref, vt_ref, lq1_ref, lk1_ref, lq2_ref, lk2_ref, g_ref, o_ref,
                  q2t_s, m_s, acc_s, *bufs, tk, group, lambda_init):
    hd, tq = qt_ref.shape[1], qt_ref.shape[2]
    seq = k_ref.shape[1]
    qt = qt_ref[0]
    row = lax.broadcasted_iota(jnp.int32, qt.shape, 0)
    zero = jnp.zeros_like(qt)
    q2t_s[:, 0:tq] = jnp.where(row < HEAD_DIM, qt, zero)
    q2t_s[:, tq:2 * tq] = jnp.where(row >= HEAD_DIM, qt, zero)
    m_s[...] = jnp.full(m_s.shape, -jnp.inf, F32)
    acc_s[...] = jnp.zeros(acc_s.shape, F32)
    slots = 2 * group
    s_bufs, p_bufs, x_bufs, a_bufs = (bufs[i * slots:(i + 1) * slots] for i in range(4))
    n_groups = seq // (tk * group)
    slabs_per_piece = PIECE_COLS // LANES
    vrows = vt_ref.shape[1]

    def work(values=None, softmax_half=None, scores=None):
        if softmax_half is not None:
            slots_m = range(softmax_half * group, (softmax_half + 1) * group)
            m_old = m_s[...]
            m_new = functools.reduce(jnp.maximum, [x_bufs[slot][...] for slot in slots_m], m_old)
            a_bufs[softmax_half][...] = jnp.exp2(m_old - m_new)
            m_s[...] = m_new
        if values is not None:
            slots_v = range(values[1] * group, (values[1] + 1) * group)
            start_v = pl.multiple_of(values[0] * group * tk, group * tk)
        for c in range(2 * tq // PIECE_COLS):
            cols = slice(c * PIECE_COLS, (c + 1) * PIECE_COLS)
            slabs = range(c * slabs_per_piece, (c + 1) * slabs_per_piece)
            if values is not None:
                p = jnp.concatenate([jnp.concatenate([p_bufs[slot][j, 0:tk] for slot in slots_v], axis=0)
                                     for j in slabs], axis=1)
                pv = jnp.dot(vt_ref[0, :, pl.ds(start_v, group * tk)], p, preferred_element_type=F32)
                for i, j in enumerate(slabs):
                    alpha = a_bufs[values[1]][:, j * LANES:(j + 1) * LANES]
                    acc_s[j, 0:vrows] = alpha * acc_s[j, 0:vrows] + pv[:, i * LANES:(i + 1) * LANES]
            for g in range(group):
                if softmax_half is not None:
                    slot_m = softmax_half * group + g
                    for j in slabs:
                        lcols = slice(j * LANES, (j + 1) * LANES)
                        p_bufs[slot_m][j, 0:tk] = jnp.exp2(s_bufs[slot_m][j, 0:tk] - m_new[:, lcols]).astype(BF16)
                if scores is not None:
                    slot_s = scores[1] * group + g
                    start_s = pl.multiple_of((scores[0] * group + g) * tk, tk)
                    s = jnp.dot(k_ref[0, pl.ds(start_s, tk), :], q2t_s[:, cols], preferred_element_type=F32)
                    for i, j in enumerate(slabs):
                        s_bufs[slot_s][j, 0:tk] = s[:, i * LANES:(i + 1) * LANES]
                    x_bufs[slot_s][:, cols] = jnp.max(s, axis=0, keepdims=True)

    def step(b, half):
        work(values=(b, half), softmax_half=1 - half, scores=(b + 2, half))

    work(scores=(0, 0))
    work(softmax_half=0, scores=(1, 1))

    def body(b, carry):
        for half in range(2):
            pl.when(b % 2 == half)(functools.partial(step, b, half))
        return carry

    lax.fori_loop(0, n_groups - 2, body, 0)
    work(values=(n_groups - 2, n_groups % 2), softmax_half=(n_groups - 1) % 2)
    work(values=(n_groups - 1, (n_groups - 1) % 2))

    acc = jnp.concatenate([acc_s[j, 0:vrows] for j in range(2 * tq // LANES)], axis=1)
    o_maps = acc[0:hd] / acc[hd:hd + 1]
    lam = (jnp.exp(jnp.sum(lq1_ref[...] * lk1_ref[...], axis=-1, keepdims=True))
           - jnp.exp(jnp.sum(lq2_ref[...] * lk2_ref[...], axis=-1, keepdims=True)) + lambda_init)
    o = (o_maps[:, 0:tq] - lam * o_maps[:, tq:2 * tq]).T
    o_ref[0] = (_rms(o, g_ref[...]) * (1.0 - lambda_init)).astype(o_ref.dtype)


def _flash(qt, k, vt, lq1, lk1, lq2, lk2, subln_g, lambda_init, tq, tk, group):
    b, seq, d = k.shape
    heads, hd, _ = qt.shape
    vrows = vt.shape[1]
    qtiles = seq // tq
    slots = 2 * group
    assert seq % (tk * group) == 0 and seq // (tk * group) >= 2, (seq, tk, group)
    small = _const_spec((1, HEAD_DIM))
    return pl.pallas_call(
        functools.partial(_flash_kernel, tk=tk, group=group, lambda_init=lambda_init),
        grid=(b, heads, qtiles),
        in_specs=[pl.BlockSpec((1, hd, tq), lambda bi, h, qi: (h, 0, bi * qtiles + qi)),
                  pl.BlockSpec((1, seq, hd), lambda bi, h, qi: (bi, 0, h)),
                  pl.BlockSpec((1, vrows, seq), lambda bi, h, qi: (h, 0, bi)),
                  small, small, small, small, _const_spec((1, hd))],
        out_specs=pl.BlockSpec((1, tq, hd), lambda bi, h, qi: (bi, qi, h)),
        out_shape=jax.ShapeDtypeStruct((b, seq, d), BF16),
        scratch_shapes=[pltpu.VMEM((hd, 2 * tq), BF16), pltpu.VMEM((1, 2 * tq), F32),
                        pltpu.VMEM((2 * tq // LANES, vrows + F32_ROWS, LANES), F32)]
        + [pltpu.VMEM((2 * tq // LANES, tk + F32_ROWS, LANES), F32)] * slots
        + [pltpu.VMEM((2 * tq // LANES, tk + BF16_ROWS, LANES), BF16)] * slots
        + [pltpu.VMEM((1, 2 * tq), F32)] * slots
        + [pltpu.VMEM((1, 2 * tq), F32)] * 2,
        compiler_params=_params(3),
        name="flash_diff_attn",
    )(qt, k, vt, lq1, lk1, lq2, lk2, subln_g)


def _flash_short_kernel(qt_ref, k_ref, vt_ref, lq1_ref, lk1_ref, lq2_ref, lk2_ref, g_ref, o_ref,
                        q2t_s, *bufs, tk, lambda_init):
    hd, tq = qt_ref.shape[1], qt_ref.shape[2]
    seq = k_ref.shape[1]
    slabs_per_piece = PIECE_COLS // LANES
    s_bufs, p_bufs, x_bufs, acc_bufs = (bufs[2 * i:2 * i + 2] for i in range(4))
    step_id = pl.program_id(0)

    @pl.when(step_id == 0)
    def _():
        for buf in s_bufs + p_bufs + x_bufs:
            buf[...] = jnp.zeros(buf.shape, buf.dtype)
        for buf in acc_bufs:
            buf[...] = jnp.ones(buf.shape, buf.dtype)

    def step(par):
        acc = acc_bufs[1 - par][...]
        o_maps = acc[0:hd] / acc[hd:hd + 1]
        lam = (jnp.exp(jnp.sum(lq1_ref[...] * lk1_ref[...], axis=-1, keepdims=True))
               - jnp.exp(jnp.sum(lq2_ref[...] * lk2_ref[...], axis=-1, keepdims=True)) + lambda_init)
        o = (o_maps[:, 0:tq] - lam * o_maps[:, tq:2 * tq]).T
        o_ref[0] = (_rms(o, g_ref[...]) * (1.0 - lambda_init)).astype(o_ref.dtype)

        qt = qt_ref[0]
        row = lax.broadcasted_iota(jnp.int32, qt.shape, 0)
        zero = jnp.zeros_like(qt)
        q2t_s[:, 0:tq] = jnp.where(row < HEAD_DIM, qt, zero)
        q2t_s[:, tq:2 * tq] = jnp.where(row >= HEAD_DIM, qt, zero)

        m_prev = x_bufs[1 - par][...]
        vk = VALUE_CHUNKS * tk
        for c in range(2 * tq // PIECE_COLS):
            cols = slice(c * PIECE_COLS, (c + 1) * PIECE_COLS)
            slabs = range(c * slabs_per_piece, (c + 1) * slabs_per_piece)
            pv, col_max = None, None
            for g in range(seq // vk):
                krows = slice(g * vk, (g + 1) * vk)
                p = jnp.concatenate([p_bufs[par][j, krows] for j in slabs], axis=1)
                part = jnp.dot(vt_ref[0, :, krows], p, preferred_element_type=F32)
                pv = part if pv is None else pv + part
                for r in range(g * VALUE_CHUNKS, (g + 1) * VALUE_CHUNKS):
                    rows = slice(r * tk, (r + 1) * tk)
                    for j in slabs:
                        lcols = slice(j * LANES, (j + 1) * LANES)
                        p_bufs[1 - par][j, rows] = jnp.exp2(s_bufs[1 - par][j, rows] - m_prev[:, lcols]).astype(BF16)
                    s = jnp.dot(k_ref[0, rows, :], q2t_s[:, cols], preferred_element_type=F32)
                    for i, j in enumerate(slabs):
                        s_bufs[par][j, rows] = s[:, i * LANES:(i + 1) * LANES]
                    chunk_max = jnp.max(s, axis=0, keepdims=True)
                    col_max = chunk_max if col_max is None else jnp.maximum(col_max, chunk_max)
            acc_bufs[par][:, cols] = pv
            x_bufs[par][:, cols] = col_max

    for par in range(2):
        pl.when(step_id % 2 == par)(functools.partial(step, par))


def _flash_short(qt, k, vt, lq1, lk1, lq2, lk2, subln_g, lambda_init, tq, tk):
    b, seq, d = k.shape
    heads, hd, _ = qt.shape
    vrows = vt.shape[1]
    qtiles = seq // tq
    n_items = b * heads * qtiles
    assert seq % (VALUE_CHUNKS * tk) == 0, (seq, tk)

    def item(i, lag):
        i = jnp.clip(i - lag, 0, n_items - 1)
        return i // (qtiles * heads), (i // qtiles) % heads, i % qtiles

    small = _const_spec((1, HEAD_DIM))
    n_slabs = 2 * tq // LANES
    return pl.pallas_call(
        functools.partial(_flash_short_kernel, tk=tk, lambda_init=lambda_init),
        grid=(n_items + 3,),
        in_specs=[pl.BlockSpec((1, hd, tq), lambda i: (item(i, 0)[1], 0, item(i, 0)[0] * qtiles + item(i, 0)[2])),
                  pl.BlockSpec((1, seq, hd), lambda i: (item(i, 0)[0], 0, item(i, 0)[1])),
                  pl.BlockSpec((1, vrows, seq), lambda i: (item(i, 2)[1], 0, item(i, 2)[0])),
                  small, small, small, small, _const_spec((1, hd))],
        out_specs=pl.BlockSpec((1, tq, hd), lambda i: (item(i, 3)[0], item(i, 3)[2], item(i, 3)[1])),
        out_shape=jax.ShapeDtypeStruct((b, seq, d), BF16),
        scratch_shapes=[pltpu.VMEM((hd, 2 * tq), BF16)]
        + [pltpu.VMEM((n_slabs, seq + F32_ROWS, LANES), F32)] * 2
        + [pltpu.VMEM((n_slabs, seq + BF16_ROWS, LANES), BF16)] * 2
        + [pltpu.VMEM((1, 2 * tq), F32)] * 2
        + [pltpu.VMEM((vrows, 2 * tq), F32)] * 2,
        compiler_params=_params(1),
        name="flash_diff_attn_short",
    )(qt, k, vt, lq1, lk1, lq2, lk2, subln_g)


def _attn_post_kernel(x_ref, o_ref, w_ref, y_ref):
    y_ref[...] = x_ref[...] + jnp.dot(o_ref[...], w_ref[...], preferred_element_type=F32)


def _attn_post(x, o, w_o, tm):
    n, d = x.shape
    row = pl.BlockSpec((tm, d), lambda i: (i, 0))
    return pl.pallas_call(
        _attn_post_kernel,
        grid=(n // tm,),
        in_specs=[row, row, _const_spec((d, d))],
        out_specs=row,
        out_shape=jax.ShapeDtypeStruct((n, d), F32),
        compiler_params=_params(1),
        name="attn_post",
    )(x, o, w_o)


def _ffn_ple_kernel(x_ref, xp_ref, xn_ref, p_ref, gf_ref, wup_ref, cw_ref, cb_ref, wdn_ref,
                    gp_ref, wg_ref, wp_ref, gfin_ref, y_ref, a_s, act_s, *, tiles_per_seq, fc, final):
    tm, d = x_ref.shape
    f = wdn_ref.shape[0]
    i = pl.program_id(0)
    keep_prev = jnp.where(i % tiles_per_seq == 0, 0.0, 1.0).astype(F32)
    keep_next = jnp.where(i % tiles_per_seq == tiles_per_seq - 1, 0.0, 1.0).astype(F32)
    x = x_ref[...]
    gf = gf_ref[...]
    h = jnp.concatenate([_rms(xp_ref[...], gf) * keep_prev, _rms(x, gf), _rms(xn_ref[...], gf) * keep_next], axis=0)
    a_s[...] = jnp.dot(h.astype(BF16), wup_ref[...], preferred_element_type=F32)

    def conv(col0):
        cols = pl.ds(col0, fc)
        c = cb_ref[:, cols]
        for t in range(CONV_WIDTH):
            c = c + a_s[pl.ds(HALO - 1 + t, tm), cols] * cw_ref[pl.ds(t, 1), cols]
        return c

    for j in range(f // fc):
        val = conv(j * fc)
        gate = conv(f + j * fc)
        act_s[:, pl.ds(j * fc, fc)] = (gate * jax.nn.sigmoid(gate) * val).astype(BF16)

    x = x + jnp.dot(act_s[...], wdn_ref[...], preferred_element_type=F32)
    gate = jax.nn.sigmoid(jnp.dot(_rms(x, gp_ref[...]).astype(BF16), wg_ref[...], preferred_element_type=F32))
    x = x + gate * jnp.dot(p_ref[...].astype(BF16), wp_ref[...], preferred_element_type=F32)
    if final:
        x = _rms(x, gfin_ref[...])
    y_ref[...] = x


def _ffn_ple(x, p, g_ffn, w_up, conv_w, conv_b, w_down, g_ple, w_gate, w_proj, g_final, layer, seq, tm, final):
    n, d = x.shape
    f = w_down.shape[1]
    pd = p.shape[2]

    def layer_spec(*shape):
        return pl.BlockSpec((None,) + shape, lambda i: (layer,) + (0,) * len(shape), pipeline_mode=pl.Buffered(1))

    fc = 2 * LANES
    tiles_per_seq = seq // tm
    hb = tm // HALO
    last_blk = n // HALO - 1
    return pl.pallas_call(
        functools.partial(_ffn_ple_kernel, tiles_per_seq=tiles_per_seq, fc=fc, final=final),
        grid=(n // tm,),
        in_specs=[pl.BlockSpec((tm, d), lambda i: (i, 0)),
                  pl.BlockSpec((HALO, d), lambda i: (jnp.maximum(i * hb - 1, 0), 0)),
                  pl.BlockSpec((HALO, d), lambda i: (jnp.minimum((i + 1) * hb, last_blk), 0)),
                  pl.BlockSpec((None, tm, pd), lambda i: (layer, i, 0)),
                  layer_spec(1, d), layer_spec(d, 2 * f), layer_spec(CONV_WIDTH, 2 * f),
                  layer_spec(1, 2 * f), layer_spec(f, d), layer_spec(1, d), layer_spec(d, d),
                  layer_spec(pd, d), _const_spec((1, d))],
        out_specs=pl.BlockSpec((tm, d), lambda i: (i, 0)),
        out_shape=jax.ShapeDtypeStruct((n, d), F32),
        scratch_shapes=[pltpu.VMEM((tm + 2 * HALO, 2 * f), F32), pltpu.VMEM((tm, f), BF16)],
        compiler_params=_params(1),
        name="ffn_ple_final" if final else "ffn_ple",
    )(x, x, x, p, g_ffn, w_up, conv_w, conv_b, w_down, g_ple, w_gate, w_proj, g_final)


def _sgu_kernel(x_ref, g_ref, wuv_ref, lng_ref, lnb_ref, ws_ref, bs_ref, wout_ref, y_ref, um_s):
    tm, d = x_ref.shape
    w = wout_ref.shape[0]
    x = x_ref[...]
    z = jnp.dot(_rms(x, g_ref[...]).astype(BF16), wuv_ref[...], preferred_element_type=F32)
    z = 0.5 * z * (1.0 + lax.erf(z * (1.0 / math.sqrt(2.0))))
    v = z[:, w:]
    vc = v - jnp.mean(v, axis=-1, keepdims=True)
    v = vc * lax.rsqrt(jnp.mean(vc * vc, axis=-1, keepdims=True) + EPS) * lng_ref[...] + lnb_ref[...]
    vb = v.astype(BF16)
    for c in range(tm // CHUNK):
        rows = slice(c * CHUNK, (c + 1) * CHUNK)
        for grp in range(w // CHUNK):
            cols = slice(grp * CHUNK, (grp + 1) * CHUNK)
            mixed = jnp.dot(ws_ref[grp], vb[rows, cols], preferred_element_type=F32) + bs_ref[grp]
            um_s[rows, cols] = (z[rows, cols] * mixed).astype(BF16)
    y_ref[...] = x + jnp.dot(um_s[...], wout_ref[...], preferred_element_type=F32)


def _sgu(x, g, w_uv, ln_g, ln_b, w_s, b_s, w_out, tm):
    n, d = x.shape
    w = w_out.shape[0]
    ng = w // CHUNK
    row = pl.BlockSpec((tm, d), lambda i: (i, 0))
    return pl.pallas_call(
        _sgu_kernel,
        grid=(n // tm,),
        in_specs=[row, _const_spec((1, d)), _const_spec((d, 2 * w)), _const_spec((1, w)), _const_spec((1, w)),
                  _const_spec((ng, CHUNK, CHUNK)), _const_spec((ng, CHUNK, 1)), _const_spec((w, d))],
        out_specs=row,
        out_shape=jax.ShapeDtypeStruct((n, d), F32),
        scratch_shapes=[pltpu.VMEM((tm, w), BF16)],
        compiler_params=_params(1),
        name="sgu",
    )(x, g, w_uv, ln_g, ln_b, w_s, b_s, w_out)


def _rope_tables(seq):
    half = HEAD_DIM // 2
    inv = 1.0 / (ROPE_THETA ** (jnp.arange(0, HEAD_DIM, 2, dtype=F32) / HEAD_DIM))
    ang = jnp.arange(seq, dtype=F32)[:, None] * inv[None, :]
    cos, sin = jnp.cos(ang), jnp.sin(ang)
    sign = jnp.where((jnp.arange(LANES) % HEAD_DIM) < half, -1.0, 1.0).astype(F32)
    reps = LANES // half
    return jnp.concatenate([cos] * reps, axis=-1), jnp.concatenate([sin] * reps, axis=-1) * sign


def _flash_tiles(seq):
    tq = _tile(seq, 2048)
    tk = _tile(seq, LANES)
    group = max(1, min(4, seq // tk // 4))
    return tq, tk, group


def _tile(n, want):
    t = min(n, want)
    assert n % t == 0, (n, t)
    return t


def _trunk(x, p, w, rope):
    b, seq, d = x.shape
    n = b * seq
    x = x.reshape(n, d)
    tm = _tile(seq, 512)
    tm_wide = _tile(seq, 1024)
    qt, k, vt = _attn_pre(x, w["norm_mix_g"][0], w["attn_w_qkv"], rope, seq, tm_wide)
    lambda_init = 0.8 - 0.6 * math.exp(-0.3 * 0)
    lam_args = (w["attn_lq1"], w["attn_lk1"], w["attn_lq2"], w["attn_lk2"], w["attn_subln_g"], lambda_init)
    if seq <= SHORT_SEQ:
        o = _flash_short(qt, k.reshape(b, seq, d), vt, *lam_args, _tile(seq, 512), LANES)
    else:
        o = _flash(qt, k.reshape(b, seq, d), vt, *lam_args, *_flash_tiles(seq))
    x = _attn_post(x, o.reshape(n, d), w["attn_w_o"], _tile(seq, 2048))
    p = p.reshape(p.shape[0], n, p.shape[-1])
    ffn = functools.partial(_ffn_ple, p=p, g_ffn=w["norm_ffn_g"], w_up=w["ffn_w_up"], conv_w=w["ffn_conv_w"],
                            conv_b=w["ffn_conv_b"], w_down=w["ffn_w_down"], g_ple=w["norm_ple_g"],
                            w_gate=w["ple_w_gate"], w_proj=w["ple_w_proj"], g_final=w["final_norm_g"], seq=seq, tm=tm)
    x = ffn(x, layer=0, final=False)
    x = _sgu(x, w["norm_mix_g"][1], w["sgu_w_uv"], w["sgu_ln_g"], w["sgu_ln_b"], w["sgu_w_s"], w["sgu_b_s"],
             w["sgu_w_out"], tm_wide)
    x = ffn(x, layer=1, final=True)
    return x.reshape(b, seq, d)


def kernel(x_prompt, x_sample, p_prompt, p_sample, norm_mix_g, attn_w_qkv, attn_lq1, attn_lk1, attn_lq2, attn_lk2, attn_subln_g, attn_w_o, sgu_w_uv, sgu_ln_g, sgu_ln_b, sgu_w_s, sgu_b_s, sgu_w_out, norm_ffn_g, ffn_w_up, ffn_conv_w, ffn_conv_b, ffn_w_down, norm_ple_g, ple_w_gate, ple_w_proj, final_norm_g):
    depth = norm_mix_g.shape[0]
    assert depth == 2 and attn_w_qkv.shape[0] == 1 and sgu_w_uv.shape[0] == 1
    d = x_prompt.shape[-1]
    row = lambda a: a.reshape(a.shape[:-1] + (1, a.shape[-1]))
    w = dict(
        norm_mix_g=row(norm_mix_g), norm_ffn_g=row(norm_ffn_g), norm_ple_g=row(norm_ple_g),
        final_norm_g=final_norm_g.reshape(1, d),
        attn_w_qkv=attn_w_qkv[0].astype(BF16), attn_w_o=attn_w_o[0].astype(BF16),
        attn_lq1=attn_lq1, attn_lk1=attn_lk1, attn_lq2=attn_lq2, attn_lk2=attn_lk2, attn_subln_g=attn_subln_g,
        sgu_w_uv=sgu_w_uv[0].astype(BF16), sgu_ln_g=sgu_ln_g, sgu_ln_b=sgu_ln_b,
        sgu_w_s=sgu_w_s[0].astype(BF16), sgu_b_s=sgu_b_s[0][:, :, None], sgu_w_out=sgu_w_out[0].astype(BF16),
        ffn_w_up=ffn_w_up.astype(BF16), ffn_conv_w=ffn_conv_w, ffn_conv_b=row(ffn_conv_b),
        ffn_w_down=ffn_w_down.astype(BF16), ple_w_gate=ple_w_gate.astype(BF16), ple_w_proj=ple_w_proj.astype(BF16),
    )
    rope = _rope_tables(max(x_prompt.shape[1], x_sample.shape[1]))
    return _trunk(x_prompt, p_prompt, w, rope), _trunk(x_sample, p_sample, w, rope)
```

```python
import functools
import math

import jax
import jax.numpy as jnp
from jax import lax
from jax.experimental import pallas as pl
from jax.experimental.pallas import tpu as pltpu

F32 = jnp.float32
BF16 = jnp.bfloat16

EPS = 1e-6
ROPE_THETA = 10000.0
HEAD_DIM = 64
CHUNK = 128
CONV_WIDTH = 3
F32_ROWS = 8
BF16_ROWS = 16
HALO = F32_ROWS
LANES = 128
MXU_COLS = 256
PIECE_COLS = MXU_COLS
SHORT_SEQ = 2048
VALUE_CHUNKS = 2
ONES_ROWS = 16
Q_SCALE = HEAD_DIM ** -0.5 * math.log2(math.e)

VMEM_LIMIT = 56 * 1024 * 1024


def _rms(x, g):
    return x * lax.rsqrt(jnp.mean(x * x, axis=-1, keepdims=True) + EPS) * g


def _const_spec(shape):
    nd = len(shape)
    return pl.BlockSpec(shape, lambda *_: (0,) * nd, pipeline_mode=pl.Buffered(1))


def _params(n_axes):
    return pltpu.CompilerParams(dimension_semantics=("arbitrary",) * n_axes, vmem_limit_bytes=VMEM_LIMIT)


def _attn_pre_kernel(x_ref, g_ref, w_ref, cos_ref, sin_ref, qt_ref, k_ref, vt_ref):
    tm, d = x_ref.shape
    hd = 2 * HEAD_DIM
    h = _rms(x_ref[...], g_ref[...]).astype(BF16)
    qkv = jnp.dot(h, w_ref[...], preferred_element_type=F32)
    lane = lax.broadcasted_iota(jnp.int32, (tm, LANES), 1)
    first_half = (lane % HEAD_DIM) < (HEAD_DIM // 2)

    def rope(t, cos, sin):
        rot = jnp.where(first_half, pltpu.roll(t, LANES - HEAD_DIM // 2, 1), pltpu.roll(t, HEAD_DIM // 2, 1))
        return t * cos + rot * sin

    cos, sin = cos_ref[...], sin_ref[...]
    for j in range(d // hd):
        cols = slice(j * hd, (j + 1) * hd)
        qt_ref[j] = (rope(qkv[:, j * hd:(j + 1) * hd], cos, sin) * Q_SCALE).T.astype(BF16)
        k_ref[:, cols] = rope(qkv[:, d + j * hd:d + (j + 1) * hd], cos, sin).astype(BF16)
        vt_ref[j, 0:hd, :] = qkv[:, 2 * d + j * hd:2 * d + (j + 1) * hd].T.astype(BF16)
        vt_ref[j, hd:hd + ONES_ROWS, :] = jnp.ones((ONES_ROWS, tm), BF16)


def _attn_pre(x, g, w_qkv, tables, seq, tm):
    n, d = x.shape
    hd = 2 * HEAD_DIM
    heads = d // hd
    tps = seq // tm
    tab_spec = pl.BlockSpec((tm, LANES), lambda i: (i % tps, 0))
    return pl.pallas_call(
        _attn_pre_kernel,
        grid=(n // tm,),
        in_specs=[pl.BlockSpec((tm, d), lambda i: (i, 0)), _const_spec((1, d)), _const_spec((d, 3 * d)),
                  tab_spec, tab_spec],
        out_specs=[pl.BlockSpec((heads, hd, tm), lambda i: (0, 0, i)),
                   pl.BlockSpec((tm, d), lambda i: (i, 0)),
                   pl.BlockSpec((heads, hd + ONES_ROWS, tm), lambda i: (0, 0, i))],
        out_shape=[jax.ShapeDtypeStruct((heads, hd, n), BF16), jax.ShapeDtypeStruct((n, d), BF16),
                   jax.ShapeDtypeStruct((heads, hd + ONES_ROWS, n), BF16)],
        compiler_params=_params(1),
        name="attn_pre",
    )(x, g, w_qkv, *tables)


def _flash_kernel(qt_ref, k_ref, vt_ref, lq1_ref, lk1_ref, lq2_ref, lk2_ref, g_ref, o_ref,
                  q2t_s, m_s, acc_s, *bufs, tk, group, lambda_init):
    hd, tq = qt_ref.shape[1], qt_ref.shape[2]
    seq = k_ref.shape[1]
    qt = qt_ref[0]
    row = lax.broadcasted_iota(jnp.int32, qt.shape, 0)
    zero = jnp.zeros_like(qt)
    q2t_s[:, 0:tq] = jnp.where(row < HEAD_DIM, qt, zero)
    q2t_s[:, tq:2 * tq] = jnp.where(row >= HEAD_DIM, qt, zero)
    m_s[...] = jnp.full(m_s.shape, -jnp.inf, F32)
    acc_s[...] = jnp.zeros(acc_s.shape, F32)
    slots = 2 * group
    s_bufs, p_bufs, x_bufs, a_bufs = (bufs[i * slots:(i + 1) * slots] for i in range(4))
    n_groups = seq // (tk * group)
    slabs_per_piece = PIECE_COLS // LANES
    vrows = vt_ref.shape[1]

    def work(values=None, softmax_half=None, scores=None):
        if softmax_half is not None:
            slots_m = range(softmax_half * group, (softmax_half + 1) * group)
            m_old = m_s[...]
            m_new = functools.reduce(jnp.maximum, [x_bufs[slot][...] for slot in slots_m], m_old)
            a_bufs[softmax_half][...] = jnp.exp2(m_old - m_new)
            m_s[...] = m_new
        if values is not None:
            slots_v = range(values[1] * group, (values[1] + 1) * group)
            start_v = pl.multiple_of(values[0] * group * tk, group * tk)
        for c in range(2 * tq // PIECE_COLS):
            cols = slice(c * PIECE_COLS, (c + 1) * PIECE_COLS)
            slabs = range(c * slabs_per_piece, (c + 1) * slabs_per_piece)
            if values is not None:
                p = jnp.concatenate([jnp.concatenate([p_bufs[slot][j, 0:tk] for slot in slots_v], axis=0)
                                     for j in slabs], axis=1)
                pv = jnp.dot(vt_ref[0, :, pl.ds(start_v, group * tk)], p, preferred_element_type=F32)
                for i, j in enumerate(slabs):
                    alpha = a_bufs[values[1]][:, j * LANES:(j + 1) * LANES]
                    acc_s[j, 0:vrows] = alpha * acc_s[j, 0:vrows] + pv[:, i * LANES:(i + 1) * LANES]
            for g in range(group):
                if softmax_half is not None:
                    slot_m = softmax_half * group + g
                    for j in slabs:
                        lcols = slice(j * LANES, (j + 1) * LANES)
                        p_bufs[slot_m][j, 0:tk] = jnp.exp2(s_bufs[slot_m][j, 0:tk] - m_new[:, lcols]).astype(BF16)
                if scores is not None:
                    slot_s = scores[1] * group + g
                    start_s = pl.multiple_of((scores[0] * group + g) * tk, tk)
                    s = jnp.dot(k_ref[0, pl.ds(start_s, tk), :], q2t_s[:, cols], preferred_element_type=F32)
                    for i, j in enumerate(slabs):
                        s_bufs[slot_s][j, 0:tk] = s[:, i * LANES:(i + 1) * LANES]
                    x_bufs[slot_s][:, cols] = jnp.max(s, axis=0, keepdims=True)

    def step(b, half):
        work(values=(b, half), softmax_half=1 - half, scores=(b + 2, half))

    work(scores=(0, 0))
    work(softmax_half=0, scores=(1, 1))

    def body(b, carry):
        for half in range(2):
            pl.when(b % 2 == half)(functools.partial(step, b, half))
        return carry

    lax.fori_loop(0, n_groups - 2, body, 0)
    work(values=(n_groups - 2, n_groups % 2), softmax_half=(n_groups - 1) % 2)
    work(values=(n_groups - 1, (n_groups - 1) % 2))

    acc = jnp.concatenate([acc_s[j, 0:vrows] for j in range(2 * tq // LANES)], axis=1)
    o_maps = acc[0:hd] / acc[hd:hd + 1]
    lam = (jnp.exp(jnp.sum(lq1_ref[...] * lk1_ref[...], axis=-1, keepdims=True))
           - jnp.exp(jnp.sum(lq2_ref[...] * lk2_ref[...], axis=-1, keepdims=True)) + lambda_init)
    o = (o_maps[:, 0:tq] - lam * o_maps[:, tq:2 * tq]).T
    o_ref[0] = (_rms(o, g_ref[...]) * (1.0 - lambda_init)).astype(o_ref.dtype)


def _flash(qt, k, vt, lq1, lk1, lq2, lk2, subln_g, lambda_init, tq, tk, group):
    b, seq, d = k.shape
    heads, hd, _ = qt.shape
    vrows = vt.shape[1]
    qtiles = seq // tq
    slots = 2 * group
    assert seq % (tk * group) == 0 and seq // (tk * group) >= 2, (seq, tk, group)
    small = _const_spec((1, HEAD_DIM))
    return pl.pallas_call(
        functools.partial(_flash_kernel, tk=tk, group=group, lambda_init=lambda_init),
        grid=(b, heads, qtiles),
        in_specs=[pl.BlockSpec((1, hd, tq), lambda bi, h, qi: (h, 0, bi * qtiles + qi)),
                  pl.BlockSpec((1, seq, hd), lambda bi, h, qi: (bi, 0, h)),
                  pl.BlockSpec((1, vrows, seq), lambda bi, h, qi: (h, 0, bi)),
                  small, small, small, small, _const_spec((1, hd))],
        out_specs=pl.BlockSpec((1, tq, hd), lambda bi, h, qi: (bi, qi, h)),
        out_shape=jax.ShapeDtypeStruct((b, seq, d), BF16),
        scratch_shapes=[pltpu.VMEM((hd, 2 * tq), BF16), pltpu.VMEM((1, 2 * tq), F32),
                        pltpu.VMEM((2 * tq // LANES, vrows + F32_ROWS, LANES), F32)]
        + [pltpu.VMEM((2 * tq // LANES, tk + F32_ROWS, LANES), F32)] * slots
        + [pltpu.VMEM((2 * tq // LANES, tk + BF16_ROWS, LANES), BF16)] * slots
        + [pltpu.VMEM((1, 2 * tq), F32)] * slots
        + [pltpu.VMEM((1, 2 * tq), F32)] * 2,
        compiler_params=_params(3),
        name="flash_diff_attn",
    )(qt, k, vt, lq1, lk1, lq2, lk2, subln_g)


def _flash_short_kernel(qt_ref, k_ref, vt_ref, lq1_ref, lk1_ref, lq2_ref, lk2_ref, g_ref, o_ref,
                        q2t_s, *bufs, tk, lambda_init):
    hd, tq = qt_ref.shape[1], qt_ref.shape[2]
    seq = k_ref.shape[1]
    slabs_per_piece = PIECE_COLS // LANES
    s_bufs, p_bufs, x_bufs, acc_bufs = (bufs[2 * i:2 * i + 2] for i in range(4))
    step_id = pl.program_id(0)

    @pl.when(step_id == 0)
    def _():
        for buf in s_bufs + p_bufs + x_bufs:
            buf[...] = jnp.zeros(buf.shape, buf.dtype)
        for buf in acc_bufs:
            buf[...] = jnp.ones(buf.shape, buf.dtype)

    def step(par):
        acc = acc_bufs[1 - par][...]
        o_maps = acc[0:hd] / acc[hd:hd + 1]
        lam = (jnp.exp(jnp.sum(lq1_ref[...] * lk1_ref[...], axis=-1, keepdims=True))
               - jnp.exp(jnp.sum(lq2_ref[...] * lk2_ref[...], axis=-1, keepdims=True)) + lambda_init)
        o = (o_maps[:, 0:tq] - lam * o_maps[:, tq:2 * tq]).T
        o_ref[0] = (_rms(o, g_ref[...]) * (1.0 - lambda_init)).astype(o_ref.dtype)

        qt = qt_ref[0]
        row = lax.broadcasted_iota(jnp.int32, qt.shape, 0)
        zero = jnp.zeros_like(qt)
        q2t_s[:, 0:tq] = jnp.where(row < HEAD_DIM, qt, zero)
        q2t_s[:, tq:2 * tq] = jnp.where(row >= HEAD_DIM, qt, zero)

        m_prev = x_bufs[1 - par][...]
        vk = VALUE_CHUNKS * tk
        for c in range(2 * tq // PIECE_COLS):
            cols = slice(c * PIECE_COLS, (c + 1) * PIECE_COLS)
            slabs = range(c * slabs_per_piece, (c + 1) * slabs_per_piece)
            pv, col_max = None, None
            for g in range(seq // vk):
                krows = slice(g * vk, (g + 1) * vk)
                p = jnp.concatenate([p_bufs[par][j, krows] for j in slabs], axis=1)
                part = jnp.dot(vt_ref[0, :, krows], p, preferred_element_type=F32)
                pv = part if pv is None else pv + part
                for r in range(g * VALUE_CHUNKS, (g + 1) * VALUE_CHUNKS):
                    rows = slice(r * tk, (r + 1) * tk)
                    for j in slabs:
                        lcols = slice(j * LANES, (j + 1) * LANES)
                        p_bufs[1 - par][j, rows] = jnp.exp2(s_bufs[1 - par][j, rows] - m_prev[:, lcols]).astype(BF16)
                    s = jnp.dot(k_ref[0, rows, :], q2t_s[:, cols], preferred_element_type=F32)
                    for i, j in enumerate(slabs):
                        s_bufs[par][j, rows] = s[:, i * LANES:(i + 1) * LANES]
                    chunk_max = jnp.max(s, axis=0, keepdims=True)
                    col_max = chunk_max if col_max is None else jnp.maximum(col_max, chunk_max)
            acc_bufs[par][:, cols] = pv
            x_bufs[par][:, cols] = col_max

    for par in range(2):
        pl.when(step_id % 2 == par)(functools.partial(step, par))


def _flash_short(qt, k, vt, lq1, lk1, lq2, lk2, subln_g, lambda_init, tq, tk):
    b, seq, d = k.shape
    heads, hd, _ = qt.shape
    vrows = vt.shape[1]
    qtiles = seq // tq
    n_items = b * heads * qtiles
    assert seq % (VALUE_CHUNKS * tk) == 0, (seq, tk)

    def item(i, lag):
        i = jnp.clip(i - lag, 0, n_items - 1)
        return i // (qtiles * heads), (i // qtiles) % heads, i % qtiles

    small = _const_spec((1, HEAD_DIM))
    n_slabs = 2 * tq // LANES
    return pl.pallas_call(
        functools.partial(_flash_short_kernel, tk=tk, lambda_init=lambda_init),
        grid=(n_items + 3,),
        in_specs=[pl.BlockSpec((1, hd, tq), lambda i: (item(i, 0)[1], 0, item(i, 0)[0] * qtiles + item(i, 0)[2])),
                  pl.BlockSpec((1, seq, hd), lambda i: (item(i, 0)[0], 0, item(i, 0)[1])),
                  pl.BlockSpec((1, vrows, seq), lambda i: (item(i, 2)[1], 0, item(i, 2)[0])),
                  small, small, small, small, _const_spec((1, hd))],
        out_specs=pl.BlockSpec((1, tq, hd), lambda i: (item(i, 3)[0], item(i, 3)[2], item(i, 3)[1])),
        out_shape=jax.ShapeDtypeStruct((b, seq, d), BF16),
        scratch_shapes=[pltpu.VMEM((hd, 2 * tq), BF16)]
        + [pltpu.VMEM((n_slabs, seq + F32_ROWS, LANES), F32)] * 2
        + [pltpu.VMEM((n_slabs, seq + BF16_ROWS, LANES), BF16)] * 2
        + [pltpu.VMEM((1, 2 * tq), F32)] * 2
        + [pltpu.VMEM((vrows, 2 * tq), F32)] * 2,
        compiler_params=_params(1),
        name="flash_diff_attn_short",
    )(qt, k, vt, lq1, lk1, lq2, lk2, subln_g)


def _attn_post_kernel(x_ref, o_ref, w_ref, y_ref):
    y_ref[...] = x_ref[...] + jnp.dot(o_ref[...], w_ref[...], preferred_element_type=F32)


def _attn_post(x, o, w_o, tm):
    n, d = x.shape
    row = pl.BlockSpec((tm, d), lambda i: (i, 0))
    return pl.pallas_call(
        _attn_post_kernel,
        grid=(n // tm,),
        in_specs=[row, row, _const_spec((d, d))],
        out_specs=row,
        out_shape=jax.ShapeDtypeStruct((n, d), F32),
        compiler_params=_params(1),
        name="attn_post",
    )(x, o, w_o)


def _ffn_ple_kernel(x_ref, xp_ref, xn_ref, p_ref, gf_ref, wup_ref, cw_ref, cb_ref, wdn_ref,
                    gp_ref, wg_ref, wp_ref, gfin_ref, y_ref, a_s, act_s, *, tiles_per_seq, fc, final):
    tm, d = x_ref.shape
    f = wdn_ref.shape[0]
    i = pl.program_id(0)
    keep_prev = jnp.where(i % tiles_per_seq == 0, 0.0, 1.0).astype(F32)
    keep_next = jnp.where(i % tiles_per_seq == tiles_per_seq - 1, 0.0, 1.0).astype(F32)
    x = x_ref[...]
    gf = gf_ref[...]
    h = jnp.concatenate([_rms(xp_ref[...], gf) * keep_prev, _rms(x, gf), _rms(xn_ref[...], gf) * keep_next], axis=0)
    a_s[...] = jnp.dot(h.astype(BF16), wup_ref[...], preferred_element_type=F32)

    def conv(col0):
        cols = pl.ds(col0, fc)
        c = cb_ref[:, cols]
        for t in range(CONV_WIDTH):
            c = c + a_s[pl.ds(HALO - 1 + t, tm), cols] * cw_ref[pl.ds(t, 1), cols]
        return c

    for j in range(f // fc):
        val = conv(j * fc)
        gate = conv(f + j * fc)
        act_s[:, pl.ds(j * fc, fc)] = (gate * jax.nn.sigmoid(gate) * val).astype(BF16)

    x = x + jnp.dot(act_s[...], wdn_ref[...], preferred_element_type=F32)
    gate = jax.nn.sigmoid(jnp.dot(_rms(x, gp_ref[...]).astype(BF16), wg_ref[...], preferred_element_type=F32))
    x = x + gate * jnp.dot(p_ref[...].astype(BF16), wp_ref[...], preferred_element_type=F32)
    if final:
        x = _rms(x, gfin_ref[...])
    y_ref[...] = x


def _ffn_ple(x, p, g_ffn, w_up, conv_w, conv_b, w_down, g_ple, w_gate, w_proj, g_final, layer, seq, tm, final):
    n, d = x.shape
    f = w_down.shape[1]
    pd = p.shape[2]

    def layer_spec(*shape):
        return pl.BlockSpec((None,) + shape, lambda i: (layer,) + (0,) * len(shape), pipeline_mode=pl.Buffered(1))

    fc = 2 * LANES
    tiles_per_seq = seq // tm
    hb = tm // HALO
    last_blk = n // HALO - 1
    return pl.pallas_call(
        functools.partial(_ffn_ple_kernel, tiles_per_seq=tiles_per_seq, fc=fc, final=final),
        grid=(n // tm,),
        in_specs=[pl.BlockSpec((tm, d), lambda i: (i, 0)),
                  pl.BlockSpec((HALO, d), lambda i: (jnp.maximum(i * hb - 1, 0), 0)),
                  pl.BlockSpec((HALO, d), lambda i: (jnp.minimum((i + 1) * hb, last_blk), 0)),
                  pl.BlockSpec((None, tm, pd), lambda i: (layer, i, 0)),
                  layer_spec(1, d), layer_spec(d, 2 * f), layer_spec(CONV_WIDTH, 2 * f),
                  layer_spec(1, 2 * f), layer_spec(f, d), layer_spec(1, d), layer_spec(d, d),
                  layer_spec(pd, d), _const_spec((1, d))],
        out_specs=pl.BlockSpec((tm, d), lambda i: (i, 0)),
        out_shape=jax.ShapeDtypeStruct((n, d), F32),
        scratch_shapes=[pltpu.VMEM((tm + 2 * HALO, 2 * f), F32), pltpu.VMEM((tm, f), BF16)],
        compiler_params=_params(1),
        name="ffn_ple_final" if final else "ffn_ple",
    )(x, x, x, p, g_ffn, w_up, conv_w, conv_b, w_down, g_ple, w_gate, w_proj, g_final)


def _sgu_kernel(x_ref, g_ref, wuv_ref, lng_ref, lnb_ref, ws_ref, bs_ref, wout_ref, y_ref, um_s):
    tm, d = x_ref.shape
    w = wout_ref.shape[0]
    x = x_ref[...]
    z = jnp.dot(_rms(x, g_ref[...]).astype(BF16), wuv_ref[...], preferred_element_type=F32)
    z = 0.5 * z * (1.0 + lax.erf(z * (1.0 / math.sqrt(2.0))))
    v = z[:, w:]
    vc = v - jnp.mean(v, axis=-1, keepdims=True)
    v = vc * lax.rsqrt(jnp.mean(vc * vc, axis=-1, keepdims=True) + EPS) * lng_ref[...] + lnb_ref[...]
    vb = v.astype(BF16)
    for c in range(tm // CHUNK):
        rows = slice(c * CHUNK, (c + 1) * CHUNK)
        for grp in range(w // CHUNK):
            cols = slice(grp * CHUNK, (grp + 1) * CHUNK)
            mixed = jnp.dot(ws_ref[grp], vb[rows, cols], preferred_element_type=F32) + bs_ref[grp]
            um_s[rows, cols] = (z[rows, cols] * mixed).astype(BF16)
    y_ref[...] = x + jnp.dot(um_s[...], wout_ref[...], preferred_element_type=F32)


def _sgu(x, g, w_uv, ln_g, ln_b, w_s, b_s, w_out, tm):
    n, d = x.shape
    w = w_out.shape[0]
    ng = w // CHUNK
    row = pl.BlockSpec((tm, d), lambda i: (i, 0))
    return pl.pallas_call(
        _sgu_kernel,
        grid=(n // tm,),
        in_specs=[row, _const_spec((1, d)), _const_spec((d, 2 * w)), _const_spec((1, w)), _const_spec((1, w)),
                  _const_spec((ng, CHUNK, CHUNK)), _const_spec((ng, CHUNK, 1)), _const_spec((w, d))],
        out_specs=row,
        out_shape=jax.ShapeDtypeStruct((n, d), F32),
        scratch_shapes=[pltpu.VMEM((tm, w), BF16)],
        compiler_params=_params(1),
        name="sgu",
    )(x, g, w_uv, ln_g, ln_b, w_s, b_s, w_out)


def _rope_tables(seq):
    half = HEAD_DIM // 2
    inv = 1.0 / (ROPE_THETA ** (jnp.arange(0, HEAD_DIM, 2, dtype=F32) / HEAD_DIM))
    ang = jnp.arange(seq, dtype=F32)[:, None] * inv[None, :]
    cos, sin = jnp.cos(ang), jnp.sin(ang)
    sign = jnp.where((jnp.arange(LANES) % HEAD_DIM) < half, -1.0, 1.0).astype(F32)
    reps = LANES // half
    return jnp.concatenate([cos] * reps, axis=-1), jnp.concatenate([sin] * reps, axis=-1) * sign


def _flash_tiles(seq):
    tq = _tile(seq, 2048)
    tk = _tile(seq, LANES)
    group = max(1, min(4, seq // tk // 4))
    return tq, tk, group


def _tile(n, want):
    t = min(n, want)
    assert n % t == 0, (n, t)
    return t


def _trunk(x, p, w, rope):
    b, seq, d = x.shape
    n = b * seq
    x = x.reshape(n, d)
    tm = _tile(seq, 512)
    tm_wide = _tile(seq, 1024)
    qt, k, vt = _attn_pre(x, w["norm_mix_g"][0], w["attn_w_qkv"], rope, seq, tm_wide)
    lambda_init = 0.8 - 0.6 * math.exp(-0.3 * 0)
    lam_args = (w["attn_lq1"], w["attn_lk1"], w["attn_lq2"], w["attn_lk2"], w["attn_subln_g"], lambda_init)
    if seq <= SHORT_SEQ:
        o = _flash_short(qt, k.reshape(b, seq, d), vt, *lam_args, _tile(seq, 512), 2 * LANES)
    else:
        o = _flash(qt, k.reshape(b, seq, d), vt, *lam_args, *_flash_tiles(seq))
    x = _attn_post(x, o.reshape(n, d), w["attn_w_o"], _tile(seq, 2048))
    p = p.reshape(p.shape[0], n, p.shape[-1])
    ffn = functools.partial(_ffn_ple, p=p, g_ffn=w["norm_ffn_g"], w_up=w["ffn_w_up"], conv_w=w["ffn_conv_w"],
                            conv_b=w["ffn_conv_b"], w_down=w["ffn_w_down"], g_ple=w["norm_ple_g"],
                            w_gate=w["ple_w_gate"], w_proj=w["ple_w_proj"], g_final=w["final_norm_g"], seq=seq, tm=tm)
    x = ffn(x, layer=0, final=False)
    x = _sgu(x, w["norm_mix_g"][1], w["sgu_w_uv"], w["sgu_ln_g"], w["sgu_ln_b"], w["sgu_w_s"], w["sgu_b_s"],
             w["sgu_w_out"], tm_wide)
    x = ffn(x, layer=1, final=True)
    return x.reshape(b, seq, d)


def kernel(x_prompt, x_sample, p_prompt, p_sample, norm_mix_g, attn_w_qkv, attn_lq1, attn_lk1, attn_lq2, attn_lk2, attn_subln_g, attn_w_o, sgu_w_uv, sgu_ln_g, sgu_ln_b, sgu_w_s, sgu_b_s, sgu_w_out, norm_ffn_g, ffn_w_up, ffn_conv_w, ffn_conv_b, ffn_w_down, norm_ple_g, ple_w_gate, ple_w_proj, final_norm_g):
    depth = norm_mix_g.shape[0]
    assert depth == 2 and attn_w_qkv.shape[0] == 1 and sgu_w_uv.shape[0] == 1
    d = x_prompt.shape[-1]
    row = lambda a: a.reshape(a.shape[:-1] + (1, a.shape[-1]))
    w = dict(
        norm_mix_g=row(norm_mix_g), norm_ffn_g=row(norm_ffn_g), norm_ple_g=row(norm_ple_g),
        final_norm_g=final_norm_g.reshape(1, d),
        attn_w_qkv=attn_w_qkv[0].astype(BF16), attn_w_o=attn_w_o[0].astype(BF16),
        attn_lq1=attn_lq1, attn_lk1=attn_lk1, attn_lq2=attn_lq2, attn_lk2=attn_lk2, attn_subln_g=attn_subln_g,
        sgu_w_uv=sgu_w_uv[0].astype(BF16), sgu_ln_g=sgu_ln_g, sgu_ln_b=sgu_ln_b,
        sgu_w_s=sgu_w_s[0].astype(BF16), sgu_b_s=sgu_b_s[0][:, :, None], sgu_w_out=sgu_w_out[0].astype(BF16),
        ffn_w_up=ffn_w_up.astype(BF16), ffn_conv_w=ffn_conv_w, ffn_conv_b=row(ffn_conv_b),
        ffn_w_down=ffn_w_down.astype(BF16), ple_w_gate=ple_w_gate.astype(BF16), ple_w_proj=ple_w_proj.astype(BF16),
    )
    rope = _rope_tables(max(x_prompt.shape[1], x_sample.shape[1]))
    return _trunk(x_prompt, p_prompt, w, rope), _trunk(x_sample, p_sample, w, rope)
```

```python
import functools
import math

import jax
import jax.numpy as jnp
from jax import lax
from jax.experimental import pallas as pl
from jax.experimental.pallas import tpu as pltpu

F32 = jnp.float32
BF16 = jnp.bfloat16

EPS = 1e-6
ROPE_THETA = 10000.0
HEAD_DIM = 64
CHUNK = 128
CONV_WIDTH = 3
F32_ROWS = 8
BF16_ROWS = 16
HALO = F32_ROWS
LANES = 128
MXU_COLS = 256
PIECE_COLS = MXU_COLS
SHORT_SEQ = 2048
VALUE_CHUNKS = 1
ONES_ROWS = 16
Q_SCALE = HEAD_DIM ** -0.5 * math.log2(math.e)

VMEM_LIMIT = 56 * 1024 * 1024


def _rms(x, g):
    return x * lax.rsqrt(jnp.mean(x * x, axis=-1, keepdims=True) + EPS) * g


def _const_spec(shape):
    nd = len(shape)
    return pl.BlockSpec(shape, lambda *_: (0,) * nd, pipeline_mode=pl.Buffered(1))


def _params(n_axes):
    return pltpu.CompilerParams(dimension_semantics=("arbitrary",) * n_axes, vmem_limit_bytes=VMEM_LIMIT)


def _attn_pre_kernel(x_ref, g_ref, w_ref, cos_ref, sin_ref, qt_ref, k_ref, vt_ref):
    tm, d = x_ref.shape
    hd = 2 * HEAD_DIM
    h = _rms(x_ref[...], g_ref[...]).astype(BF16)
    qkv = jnp.dot(h, w_ref[...], preferred_element_type=F32)
    lane = lax.broadcasted_iota(jnp.int32, (tm, LANES), 1)
    first_half = (lane % HEAD_DIM) < (HEAD_DIM // 2)

    def rope(t, cos, sin):
        rot = jnp.where(first_half, pltpu.roll(t, LANES - HEAD_DIM // 2, 1), pltpu.roll(t, HEAD_DIM // 2, 1))
        return t * cos + rot * sin

    cos, sin = cos_ref[...], sin_ref[...]
    for j in range(d // hd):
        cols = slice(j * hd, (j + 1) * hd)
        qt_ref[j] = (rope(qkv[:, j * hd:(j + 1) * hd], cos, sin) * Q_SCALE).T.astype(BF16)
        k_ref[:, cols] = rope(qkv[:, d + j * hd:d + (j + 1) * hd], cos, sin).astype(BF16)
        vt_ref[j, 0:hd, :] = qkv[:, 2 * d + j * hd:2 * d + (j + 1) * hd].T.astype(BF16)
        vt_ref[j, hd:hd + ONES_ROWS, :] = jnp.ones((ONES_ROWS, tm), BF16)


def _attn_pre(x, g, w_qkv, tables, seq, tm):
    n, d = x.shape
    hd = 2 * HEAD_DIM
    heads = d // hd
    tps = seq // tm
    tab_spec = pl.BlockSpec((tm, LANES), lambda i: (i % tps, 0))
    return pl.pallas_call(
        _attn_pre_kernel,
        grid=(n // tm,),
        in_specs=[pl.BlockSpec((tm, d), lambda i: (i, 0)), _const_spec((1, d)), _const_spec((d, 3 * d)),
                  tab_spec, tab_spec],
        out_specs=[pl.BlockSpec((heads, hd, tm), lambda i: (0, 0, i)),
                   pl.BlockSpec((tm, d), lambda i: (i, 0)),
                   pl.BlockSpec((heads, hd + ONES_ROWS, tm), lambda i: (0, 0, i))],
        out_shape=[jax.ShapeDtypeStruct((heads, hd, n), BF16), jax.ShapeDtypeStruct((n, d), BF16),
                   jax.ShapeDtypeStruct((heads, hd + ONES_ROWS, n), BF16)],
        compiler_params=_params(1),
        name="attn_pre",
    )(x, g, w_qkv, *tables)


def _flash_kernel(qt_ref, k_ref, vt_ref, lq1_ref, lk1_ref, lq2_ref, lk2_ref, g_ref, o_ref,
                  q2t_s, m_s, acc_s, *bufs, tk, group, lambda_init):
    hd, tq = qt_ref.shape[1], qt_ref.shape[2]
    seq = k_ref.shape[1]
    qt = qt_ref[0]
    row = lax.broadcasted_iota(jnp.int32, qt.shape, 0)
    zero = jnp.zeros_like(qt)
    q2t_s[:, 0:tq] = jnp.where(row < HEAD_DIM, qt, zero)
    q2t_s[:, tq:2 * tq] = jnp.where(row >= HEAD_DIM, qt, zero)
    m_s[...] = jnp.full(m_s.shape, -jnp.inf, F32)
    acc_s[...] = jnp.zeros(acc_s.shape, F32)
    slots = 2 * group
    s_bufs, p_bufs, x_bufs, a_bufs = (bufs[i * slots:(i + 1) * slots] for i in range(4))
    n_groups = seq // (tk * group)
    slabs_per_piece = PIECE_COLS // LANES
    vrows = vt_ref.shape[1]

    def work(values=None, softmax_half=None, scores=None):
        if softmax_half is not None:
            slots_m = range(softmax_half * group, (softmax_half + 1) * group)
            m_old = m_s[...]
            m_new = functools.reduce(jnp.maximum, [x_bufs[slot][...] for slot in slots_m], m_old)
            a_bufs[softmax_half][...] = jnp.exp2(m_old - m_new)
            m_s[...] = m_new
        if values is not None:
            slots_v = range(values[1] * group, (values[1] + 1) * group)
            start_v = pl.multiple_of(values[0] * group * tk, group * tk)
        for c in range(2 * tq // PIECE_COLS):
            cols = slice(c * PIECE_COLS, (c + 1) * PIECE_COLS)
            slabs = range(c * slabs_per_piece, (c + 1) * slabs_per_piece)
            if values is not None:
                p = jnp.concatenate([jnp.concatenate([p_bufs[slot][j, 0:tk] for slot in slots_v], axis=0)
                                     for j in slabs], axis=1)
                pv = jnp.dot(vt_ref[0, :, pl.ds(start_v, group * tk)], p, preferred_element_type=F32)
                for i, j in enumerate(slabs):
                    alpha = a_bufs[values[1]][:, j * LANES:(j + 1) * LANES]
                    acc_s[j, 0:vrows] = alpha * acc_s[j, 0:vrows] + pv[:, i * LANES:(i + 1) * LANES]
            for g in range(group):
                if softmax_half is not None:
                    slot_m = softmax_half * group + g
                    for j in slabs:
                        lcols = slice(j * LANES, (j + 1) * LANES)
                        p_bufs[slot_m][j, 0:tk] = jnp.exp2(s_bufs[slot_m][j, 0:tk] - m_new[:, lcols]).astype(BF16)
                if scores is not None:
                    slot_s = scores[1] * group + g
                    start_s = pl.multiple_of((scores[0] * group + g) * tk, tk)
                    s = jnp.dot(k_ref[0, pl.ds(start_s, tk), :], q2t_s[:, cols], preferred_element_type=F32)
                    for i, j in enumerate(slabs):
                        s_bufs[slot_s][j, 0:tk] = s[:, i * LANES:(i + 1) * LANES]
                    x_bufs[slot_s][:, cols] = jnp.max(s, axis=0, keepdims=True)

    def step(b, half):
        work(values=(b, half), softmax_half=1 - half, scores=(b + 2, half))

    work(scores=(0, 0))
    work(softmax_half=0, scores=(1, 1))

    def body(b, carry):
        for half in range(2):
            pl.when(b % 2 == half)(functools.partial(step, b, half))
        return carry

    lax.fori_loop(0, n_groups - 2, body, 0)
    work(values=(n_groups - 2, n_groups % 2), softmax_half=(n_groups - 1) % 2)
    work(values=(n_groups - 1, (n_groups - 1) % 2))

    acc = jnp.concatenate([acc_s[j, 0:vrows] for j in range(2 * tq // LANES)], axis=1)
    o_maps = acc[0:hd] / acc[hd:hd + 1]
    lam = (jnp.exp(jnp.sum(lq1_ref[...] * lk1_ref[...], axis=-1, keepdims=True))
           - jnp.exp(jnp.sum(lq2_ref[...] * lk2_ref[...], axis=-1, keepdims=True)) + lambda_init)
    o = (o_maps[:, 0:tq] - lam * o_maps[:, tq:2 * tq]).T
    o_ref[0] = (_rms(o, g_ref[...]) * (1.0 - lambda_init)).astype(o_ref.dtype)


def _flash(qt, k, vt, lq1, lk1, lq2, lk2, subln_g, lambda_init, tq, tk, group):
    b, seq, d = k.shape
    heads, hd, _ = qt.shape
    vrows = vt.shape[1]
    qtiles = seq // tq
    slots = 2 * group
    assert seq % (tk * group) == 0 and seq // (tk * group) >= 2, (seq, tk, group)
    small = _const_spec((1, HEAD_DIM))
    return pl.pallas_call(
        functools.partial(_flash_kernel, tk=tk, group=group, lambda_init=lambda_init),
        grid=(b, heads, qtiles),
        in_specs=[pl.BlockSpec((1, hd, tq), lambda bi, h, qi: (h, 0, bi * qtiles + qi)),
                  pl.BlockSpec((1, seq, hd), lambda bi, h, qi: (bi, 0, h)),
                  pl.BlockSpec((1, vrows, seq), lambda bi, h, qi: (h, 0, bi)),
                  small, small, small, small, _const_spec((1, hd))],
        out_specs=pl.BlockSpec((1, tq, hd), lambda bi, h, qi: (bi, qi, h)),
        out_shape=jax.ShapeDtypeStruct((b, seq, d), BF16),
        scratch_shapes=[pltpu.VMEM((hd, 2 * tq), BF16), pltpu.VMEM((1, 2 * tq), F32),
                        pltpu.VMEM((2 * tq // LANES, vrows + F32_ROWS, LANES), F32)]
        + [pltpu.VMEM((2 * tq // LANES, tk + F32_ROWS, LANES), F32)] * slots
        + [pltpu.VMEM((2 * tq // LANES, tk + BF16_ROWS, LANES), BF16)] * slots
        + [pltpu.VMEM((1, 2 * tq), F32)] * slots
        + [pltpu.VMEM((1, 2 * tq), F32)] * 2,
        compiler_params=_params(3),
        name="flash_diff_attn",
    )(qt, k, vt, lq1, lk1, lq2, lk2, subln_g)


def _flash_short_kernel(qt_ref, k_ref, vt_ref, lq1_ref, lk1_ref, lq2_ref, lk2_ref, g_ref, o_ref,
                        q2t_s, *bufs, tk, lambda_init):
    hd, tq = qt_ref.shape[1], qt_ref.shape[2]
    seq = k_ref.shape[1]
    slabs_per_piece = PIECE_COLS // LANES
    s_bufs, p_bufs, x_bufs, acc_bufs = (bufs[2 * i:2 * i + 2] for i in range(4))
    step_id = pl.program_id(0)

    @pl.when(step_id == 0)
    def _():
        for buf in s_bufs + p_bufs + x_bufs:
            buf[...] = jnp.zeros(buf.shape, buf.dtype)
        for buf in acc_bufs:
            buf[...] = jnp.ones(buf.shape, buf.dtype)

    def step(par):
        acc = acc_bufs[1 - par][...]
        o_maps = acc[0:hd] / acc[hd:hd + 1]
        lam = (jnp.exp(jnp.sum(lq1_ref[...] * lk1_ref[...], axis=-1, keepdims=True))
               - jnp.exp(jnp.sum(lq2_ref[...] * lk2_ref[...], axis=-1, keepdims=True)) + lambda_init)
        o = (o_maps[:, 0:tq] - lam * o_maps[:, tq:2 * tq]).T
        o_ref[0] = (_rms(o, g_ref[...]) * (1.0 - lambda_init)).astype(o_ref.dtype)

        qt = qt_ref[0]
        row = lax.broadcasted_iota(jnp.int32, qt.shape, 0)
        zero = jnp.zeros_like(qt)
        q2t_s[:, 0:tq] = jnp.where(row < HEAD_DIM, qt, zero)
        q2t_s[:, tq:2 * tq] = jnp.where(row >= HEAD_DIM, qt, zero)

        m_prev = x_bufs[1 - par][...]
        vk = VALUE_CHUNKS * tk
        for c in range(2 * tq // PIECE_COLS):
            cols = slice(c * PIECE_COLS, (c + 1) * PIECE_COLS)
            slabs = range(c * slabs_per_piece, (c + 1) * slabs_per_piece)
            pv, col_max = None, None
            for g in range(seq // vk):
                krows = slice(g * vk, (g + 1) * vk)
                p = jnp.concatenate([p_bufs[par][j, krows] for j in slabs], axis=1)
                part = jnp.dot(vt_ref[0, :, krows], p, preferred_element_type=F32)
                pv = part if pv is None else pv + part
                for r in range(g * VALUE_CHUNKS, (g + 1) * VALUE_CHUNKS):
                    rows = slice(r * tk, (r + 1) * tk)
                    for j in slabs:
                        lcols = slice(j * LANES, (j + 1) * LANES)
                        p_bufs[1 - par][j, rows] = jnp.exp2(s_bufs[1 - par][j, rows] - m_prev[:, lcols]).astype(BF16)
                    s = jnp.dot(k_ref[0, rows, :], q2t_s[:, cols], preferred_element_type=F32)
                    for i, j in enumerate(slabs):
                        s_bufs[par][j, rows] = s[:, i * LANES:(i + 1) * LANES]
                    chunk_max = jnp.max(s, axis=0, keepdims=True)
                    col_max = chunk_max if col_max is None else jnp.maximum(col_max, chunk_max)
            acc_bufs[par][:, cols] = pv
            x_bufs[par][:, cols] = col_max

    for par in range(2):
        pl.when(step_id % 2 == par)(functools.partial(step, par))


def _flash_short(qt, k, vt, lq1, lk1, lq2, lk2, subln_g, lambda_init, tq, tk):
    b, seq, d = k.shape
    heads, hd, _ = qt.shape
    vrows = vt.shape[1]
    qtiles = seq // tq
    n_items = b * heads * qtiles
    assert seq % (VALUE_CHUNKS * tk) == 0, (seq, tk)

    def item(i, lag):
        i = jnp.clip(i - lag, 0, n_items - 1)
        return i // (qtiles * heads), (i // qtiles) % heads, i % qtiles

    small = _const_spec((1, HEAD_DIM))
    n_slabs = 2 * tq // LANES
    return pl.pallas_call(
        functools.partial(_flash_short_kernel, tk=tk, lambda_init=lambda_init),
        grid=(n_items + 3,),
        in_specs=[pl.BlockSpec((1, hd, tq), lambda i: (item(i, 0)[1], 0, item(i, 0)[0] * qtiles + item(i, 0)[2])),
                  pl.BlockSpec((1, seq, hd), lambda i: (item(i, 0)[0], 0, item(i, 0)[1])),
                  pl.BlockSpec((1, vrows, seq), lambda i: (item(i, 2)[1], 0, item(i, 2)[0])),
                  small, small, small, small, _const_spec((1, hd))],
        out_specs=pl.BlockSpec((1, tq, hd), lambda i: (item(i, 3)[0], item(i, 3)[2], item(i, 3)[1])),
        out_shape=jax.ShapeDtypeStruct((b, seq, d), BF16),
        scratch_shapes=[pltpu.VMEM((hd, 2 * tq), BF16)]
        + [pltpu.VMEM((n_slabs, seq + F32_ROWS, LANES), F32)] * 2
        + [pltpu.VMEM((n_slabs, seq + BF16_ROWS, LANES), BF16)] * 2
        + [pltpu.VMEM((1, 2 * tq), F32)] * 2
        + [pltpu.VMEM((vrows, 2 * tq), F32)] * 2,
        compiler_params=_params(1),
        name="flash_diff_attn_short",
    )(qt, k, vt, lq1, lk1, lq2, lk2, subln_g)


def _attn_post_kernel(x_ref, o_ref, w_ref, y_ref):
    y_ref[...] = x_ref[...] + jnp.dot(o_ref[...], w_ref[...], preferred_element_type=F32)


def _attn_post(x, o, w_o, tm):
    n, d = x.shape
    row = pl.BlockSpec((tm, d), lambda i: (i, 0))
    return pl.pallas_call(
        _attn_post_kernel,
        grid=(n // tm,),
        in_specs=[row, row, _const_spec((d, d))],
        out_specs=row,
        out_shape=jax.ShapeDtypeStruct((n, d), F32),
        compiler_params=_params(1),
        name="attn_post",
    )(x, o, w_o)


def _ffn_ple_kernel(x_ref, xp_ref, xn_ref, p_ref, gf_ref, wup_ref, cw_ref, cb_ref, wdn_ref,
                    gp_ref, wg_ref, wp_ref, gfin_ref, y_ref, a_s, act_s, *, tiles_per_seq, fc, final):
    tm, d = x_ref.shape
    f = wdn_ref.shape[0]
    i = pl.program_id(0)
    keep_prev = jnp.where(i % tiles_per_seq == 0, 0.0, 1.0).astype(F32)
    keep_next = jnp.where(i % tiles_per_seq == tiles_per_seq - 1, 0.0, 1.0).astype(F32)
    x = x_ref[...]
    gf = gf_ref[...]
    h = jnp.concatenate([_rms(xp_ref[...], gf) * keep_prev, _rms(x, gf), _rms(xn_ref[...], gf) * keep_next], axis=0)
    a_s[...] = jnp.dot(h.astype(BF16), wup_ref[...], preferred_element_type=F32)

    def conv(col0):
        cols = pl.ds(col0, fc)
        c = cb_ref[:, cols]
        for t in range(CONV_WIDTH):
            c = c + a_s[pl.ds(HALO - 1 + t, tm), cols] * cw_ref[pl.ds(t, 1), cols]
        return c

    for j in range(f // fc):
        val = conv(j * fc)
        gate = conv(f + j * fc)
        act_s[:, pl.ds(j * fc, fc)] = (gate * jax.nn.sigmoid(gate) * val).astype(BF16)

    x = x + jnp.dot(act_s[...], wdn_ref[...], preferred_element_type=F32)
    gate = jax.nn.sigmoid(jnp.dot(_rms(x, gp_ref[...]).astype(BF16), wg_ref[...], preferred_element_type=F32))
    x = x + gate * jnp.dot(p_ref[...].astype(BF16), wp_ref[...], preferred_element_type=F32)
    if final:
        x = _rms(x, gfin_ref[...])
    y_ref[...] = x


def _ffn_ple(x, p, g_ffn, w_up, conv_w, conv_b, w_down, g_ple, w_gate, w_proj, g_final, layer, seq, tm, final):
    n, d = x.shape
    f = w_down.shape[1]
    pd = p.shape[2]

    def layer_spec(*shape):
        return pl.BlockSpec((None,) + shape, lambda i: (layer,) + (0,) * len(shape), pipeline_mode=pl.Buffered(1))

    fc = 2 * LANES
    tiles_per_seq = seq // tm
    hb = tm // HALO
    last_blk = n // HALO - 1
    return pl.pallas_call(
        functools.partial(_ffn_ple_kernel, tiles_per_seq=tiles_per_seq, fc=fc, final=final),
        grid=(n // tm,),
        in_specs=[pl.BlockSpec((tm, d), lambda i: (i, 0)),
                  pl.BlockSpec((HALO, d), lambda i: (jnp.maximum(i * hb - 1, 0), 0)),
                  pl.BlockSpec((HALO, d), lambda i: (jnp.minimum((i + 1) * hb, last_blk), 0)),
                  pl.BlockSpec((None, tm, pd), lambda i: (layer, i, 0)),
                  layer_spec(1, d), layer_spec(d, 2 * f), layer_spec(CONV_WIDTH, 2 * f),
                  layer_spec(1, 2 * f), layer_spec(f, d), layer_spec(1, d), layer_spec(d, d),
                  layer_spec(pd, d), _const_spec((1, d))],
        out_specs=pl.BlockSpec((tm, d), lambda i: (i, 0)),
        out_shape=jax.ShapeDtypeStruct((n, d), F32),
        scratch_shapes=[pltpu.VMEM((tm + 2 * HALO, 2 * f), F32), pltpu.VMEM((tm, f), BF16)],
        compiler_params=_params(1),
        name="ffn_ple_final" if final else "ffn_ple",
    )(x, x, x, p, g_ffn, w_up, conv_w, conv_b, w_down, g_ple, w_gate, w_proj, g_final)


def _sgu_kernel(x_ref, g_ref, wuv_ref, lng_ref, lnb_ref, ws_ref, bs_ref, wout_ref, y_ref, um_s):
    tm, d = x_ref.shape
    w = wout_ref.shape[0]
    x = x_ref[...]
    z = jnp.dot(_rms(x, g_ref[...]).astype(BF16), wuv_ref[...], preferred_element_type=F32)
    z = 0.5 * z * (1.0 + lax.erf(z * (1.0 / math.sqrt(2.0))))
    v = z[:, w:]
    vc = v - jnp.mean(v, axis=-1, keepdims=True)
    v = vc * lax.rsqrt(jnp.mean(vc * vc, axis=-1, keepdims=True) + EPS) * lng_ref[...] + lnb_ref[...]
    vb = v.astype(BF16)
    for c in range(tm // CHUNK):
        rows = slice(c * CHUNK, (c + 1) * CHUNK)
        for grp in range(w // CHUNK):
            cols = slice(grp * CHUNK, (grp + 1) * CHUNK)
            mixed = jnp.dot(ws_ref[grp], vb[rows, cols], preferred_element_type=F32) + bs_ref[grp]
            um_s[rows, cols] = (z[rows, cols] * mixed).astype(BF16)
    y_ref[...] = x + jnp.dot(um_s[...], wout_ref[...], preferred_element_type=F32)


def _sgu(x, g, w_uv, ln_g, ln_b, w_s, b_s, w_out, tm):
    n, d = x.shape
    w = w_out.shape[0]
    ng = w // CHUNK
    row = pl.BlockSpec((tm, d), lambda i: (i, 0))
    return pl.pallas_call(
        _sgu_kernel,
        grid=(n // tm,),
        in_specs=[row, _const_spec((1, d)), _const_spec((d, 2 * w)), _const_spec((1, w)), _const_spec((1, w)),
                  _const_spec((ng, CHUNK, CHUNK)), _const_spec((ng, CHUNK, 1)), _const_spec((w, d))],
        out_specs=row,
        out_shape=jax.ShapeDtypeStruct((n, d), F32),
        scratch_shapes=[pltpu.VMEM((tm, w), BF16)],
        compiler_params=_params(1),
        name="sgu",
    )(x, g, w_uv, ln_g, ln_b, w_s, b_s, w_out)


def _rope_tables(seq):
    half = HEAD_DIM // 2
    inv = 1.0 / (ROPE_THETA ** (jnp.arange(0, HEAD_DIM, 2, dtype=F32) / HEAD_DIM))
    ang = jnp.arange(seq, dtype=F32)[:, None] * inv[None, :]
    cos, sin = jnp.cos(ang), jnp.sin(ang)
    sign = jnp.where((jnp.arange(LANES) % HEAD_DIM) < half, -1.0, 1.0).astype(F32)
    reps = LANES // half
    return jnp.concatenate([cos] * reps, axis=-1), jnp.concatenate([sin] * reps, axis=-1) * sign


def _flash_tiles(seq):
    tq = _tile(seq, 2048)
    tk = _tile(seq, LANES)
    group = max(1, min(4, seq // tk // 4))
    return tq, tk, group


def _tile(n, want):
    t = min(n, want)
    assert n % t == 0, (n, t)
    return t


def _trunk(x, p, w, rope):
    b, seq, d = x.shape
    n = b * seq
    x = x.reshape(n, d)
    tm = _tile(seq, 512)
    tm_wide = _tile(seq, 1024)
    qt, k, vt = _attn_pre(x, w["norm_mix_g"][0], w["attn_w_qkv"], rope, seq, tm_wide)
    lambda_init = 0.8 - 0.6 * math.exp(-0.3 * 0)
    lam_args = (w["attn_lq1"], w["attn_lk1"], w["attn_lq2"], w["attn_lk2"], w["attn_subln_g"], lambda_init)
    if seq <= SHORT_SEQ:
        o = _flash_short(qt, k.reshape(b, seq, d), vt, *lam_args, _tile(seq, 512), 4 * LANES)
    else:
        o = _flash(qt, k.reshape(b, seq, d), vt, *lam_args, *_flash_tiles(seq))
    x = _attn_post(x, o.reshape(n, d), w["attn_w_o"], _tile(seq, 2048))
    p = p.reshape(p.shape[0], n, p.shape[-1])
    ffn = functools.partial(_ffn_ple, p=p, g_ffn=w["norm_ffn_g"], w_up=w["ffn_w_up"], conv_w=w["ffn_conv_w"],
                            conv_b=w["ffn_conv_b"], w_down=w["ffn_w_down"], g_ple=w["norm_ple_g"],
                            w_gate=w["ple_w_gate"], w_proj=w["ple_w_proj"], g_final=w["final_norm_g"], seq=seq, tm=tm)
    x = ffn(x, layer=0, final=False)
    x = _sgu(x, w["norm_mix_g"][1], w["sgu_w_uv"], w["sgu_ln_g"], w["sgu_ln_b"], w["sgu_w_s"], w["sgu_b_s"],
             w["sgu_w_out"], tm_wide)
    x = ffn(x, layer=1, final=True)
    return x.reshape(b, seq, d)


def kernel(x_prompt, x_sample, p_prompt, p_sample, norm_mix_g, attn_w_qkv, attn_lq1, attn_lk1, attn_lq2, attn_lk2, attn_subln_g, attn_w_o, sgu_w_uv, sgu_ln_g, sgu_ln_b, sgu_w_s, sgu_b_s, sgu_w_out, norm_ffn_g, ffn_w_up, ffn_conv_w, ffn_conv_b, ffn_w_down, norm_ple_g, ple_w_gate, ple_w_proj, final_norm_g):
    depth = norm_mix_g.shape[0]
    assert depth == 2 and attn_w_qkv.shape[0] == 1 and sgu_w_uv.shape[0] == 1
    d = x_prompt.shape[-1]
    row = lambda a: a.reshape(a.shape[:-1] + (1, a.shape[-1]))
    w = dict(
        norm_mix_g=row(norm_mix_g), norm_ffn_g=row(norm_ffn_g), norm_ple_g=row(norm_ple_g),
        final_norm_g=final_norm_g.reshape(1, d),
        attn_w_qkv=attn_w_qkv[0].astype(BF16), attn_w_o=attn_w_o[0].astype(BF16),
        attn_lq1=attn_lq1, attn_lk1=attn_lk1, attn_lq2=attn_lq2, attn_lk2=attn_lk2, attn_subln_g=attn_subln_g,
        sgu_w_uv=sgu_w_uv[0].astype(BF16), sgu_ln_g=sgu_ln_g, sgu_ln_b=sgu_ln_b,
        sgu_w_s=sgu_w_s[0].astype(BF16), sgu_b_s=sgu_b_s[0][:, :, None], sgu_w_out=sgu_w_out[0].astype(BF16),
        ffn_w_up=ffn_w_up.astype(BF16), ffn_conv_w=ffn_conv_w, ffn_conv_b=row(ffn_conv_b),
        ffn_w_down=ffn_w_down.astype(BF16), ple_w_gate=ple_w_gate.astype(BF16), ple_w_proj=ple_w_proj.astype(BF16),
    )
    rope = _rope_tables(max(x_prompt.shape[1], x_sample.shape[1]))
    return _trunk(x_prompt, p_prompt, w, rope), _trunk(x_sample, p_sample, w, rope)
```

```python
import functools
import math

import jax
import jax.numpy as jnp
from jax import lax
from jax.experimental import pallas as pl
from jax.experimental.pallas import tpu as pltpu

F32 = jnp.float32
BF16 = jnp.bfloat16

EPS = 1e-6
ROPE_THETA = 10000.0
HEAD_DIM = 64
CHUNK = 128
CONV_WIDTH = 3
F32_ROWS = 8
BF16_ROWS = 16
HALO = F32_ROWS
LANES = 128
MXU_COLS = 256
PIECE_COLS = MXU_COLS
SHORT_SEQ = 2048
VALUE_CHUNKS = 2
ONES_ROWS = 16
Q_SCALE = HEAD_DIM ** -0.5 * math.log2(math.e)

VMEM_LIMIT = 56 * 1024 * 1024


def _rms(x, g):
    return x * lax.rsqrt(jnp.mean(x * x, axis=-1, keepdims=True) + EPS) * g


def _const_spec(shape):
    nd = len(shape)
    return pl.BlockSpec(shape, lambda *_: (0,) * nd, pipeline_mode=pl.Buffered(1))


def _params(n_axes):
    return pltpu.CompilerParams(dimension_semantics=("arbitrary",) * n_axes, vmem_limit_bytes=VMEM_LIMIT)


def _attn_pre_kernel(x_ref, g_ref, w_ref, cos_ref, sin_ref, qt_ref, k_ref, vt_ref):
    tm, d = x_ref.shape
    hd = 2 * HEAD_DIM
    h = _rms(x_ref[...], g_ref[...]).astype(BF16)
    qkv = jnp.dot(h, w_ref[...], preferred_element_type=F32)
    lane = lax.broadcasted_iota(jnp.int32, (tm, LANES), 1)
    first_half = (lane % HEAD_DIM) < (HEAD_DIM // 2)

    def rope(t, cos, sin):
        rot = jnp.where(first_half, pltpu.roll(t, LANES - HEAD_DIM // 2, 1), pltpu.roll(t, HEAD_DIM // 2, 1))
        return t * cos + rot * sin

    cos, sin = cos_ref[...], sin_ref[...]
    for j in range(d // hd):
        cols = slice(j * hd, (j + 1) * hd)
        qt_ref[j] = (rope(qkv[:, j * hd:(j + 1) * hd], cos, sin) * Q_SCALE).T.astype(BF16)
        k_ref[:, cols] = rope(qkv[:, d + j * hd:d + (j + 1) * hd], cos, sin).astype(BF16)
        vt_ref[j, 0:hd, :] = qkv[:, 2 * d + j * hd:2 * d + (j + 1) * hd].T.astype(BF16)
        vt_ref[j, hd:hd + ONES_ROWS, :] = jnp.ones((ONES_ROWS, tm), BF16)


def _attn_pre(x, g, w_qkv, tables, seq, tm):
    n, d = x.shape
    hd = 2 * HEAD_DIM
    heads = d // hd
    tps = seq // tm
    tab_spec = pl.BlockSpec((tm, LANES), lambda i: (i % tps, 0))
    return pl.pallas_call(
        _attn_pre_kernel,
        grid=(n // tm,),
        in_specs=[pl.BlockSpec((tm, d), lambda i: (i, 0)), _const_spec((1, d)), _const_spec((d, 3 * d)),
                  tab_spec, tab_spec],
        out_specs=[pl.BlockSpec((heads, hd, tm), lambda i: (0, 0, i)),
                   pl.BlockSpec((tm, d), lambda i: (i, 0)),
                   pl.BlockSpec((heads, hd + ONES_ROWS, tm), lambda i: (0, 0, i))],
        out_shape=[jax.ShapeDtypeStruct((heads, hd, n), BF16), jax.ShapeDtypeStruct((n, d), BF16),
                   jax.ShapeDtypeStruct((heads, hd + ONES_ROWS, n), BF16)],
        compiler_params=_params(1),
        name="attn_pre",
    )(x, g, w_qkv, *tables)


def _flash_kernel(qt_ref, k_ref, vt_ref, lq1_ref, lk1_ref, lq2_ref, lk2_ref, g_ref, o_ref,
                  q2t_s, m_s, acc_s, *bufs, tq, tk, group, lambda_init):
    hd = qt_ref.shape[1]
    seq = k_ref.shape[1]
    slots = 2 * group
    s_bufs, p_bufs, x_bufs, a_bufs = (bufs[i * slots:(i + 1) * slots] for i in range(4))
    gpt = seq // (tk * group)
    n_steps = (seq // tq) * gpt
    slabs_per_piece = PIECE_COLS // LANES
    vrows = vt_ref.shape[1]

    def load_queries(tile):
        qt = qt_ref[0, :, pl.ds(pl.multiple_of(tile * tq, tq), tq)]
        row = lax.broadcasted_iota(jnp.int32, qt.shape, 0)
        zero = jnp.zeros_like(qt)
        q2t_s[:, 0:tq] = jnp.where(row < HEAD_DIM, qt, zero)
        q2t_s[:, tq:2 * tq] = jnp.where(row >= HEAD_DIM, qt, zero)

    def finish(tile):
        acc = jnp.concatenate([acc_s[j, 0:vrows] for j in range(2 * tq // LANES)], axis=1)
        o_maps = acc[0:hd] / acc[hd:hd + 1]
        lam = (jnp.exp(jnp.sum(lq1_ref[...] * lk1_ref[...], axis=-1, keepdims=True))
               - jnp.exp(jnp.sum(lq2_ref[...] * lk2_ref[...], axis=-1, keepdims=True)) + lambda_init)
        o = (o_maps[:, 0:tq] - lam * o_maps[:, tq:2 * tq]).T
        o_ref[0, pl.ds(pl.multiple_of(tile * tq, tq), tq), :] = (
            _rms(o, g_ref[...]) * (1.0 - lambda_init)).astype(o_ref.dtype)
        acc_s[...] = jnp.zeros(acc_s.shape, F32)

    def work(values=None, softmax=None, scores=None):
        if softmax is not None:
            slots_m = range(softmax[1] * group, (softmax[1] + 1) * group)
            m_old = jnp.where(softmax[0] % gpt == 0, -jnp.inf, m_s[...])
            m_new = functools.reduce(jnp.maximum, [x_bufs[slot][...] for slot in slots_m], m_old)
            a_bufs[softmax[1]][...] = jnp.exp2(m_old - m_new)
            m_s[...] = m_new
        if values is not None:
            slots_v = range(values[1] * group, (values[1] + 1) * group)
            start_v = pl.multiple_of((values[0] % gpt) * group * tk, group * tk)
        for c in range(2 * tq // PIECE_COLS):
            cols = slice(c * PIECE_COLS, (c + 1) * PIECE_COLS)
            slabs = range(c * slabs_per_piece, (c + 1) * slabs_per_piece)
            if values is not None:
                p = jnp.concatenate([jnp.concatenate([p_bufs[slot][j, 0:tk] for slot in slots_v], axis=0)
                                     for j in slabs], axis=1)
                pv = jnp.dot(vt_ref[0, :, pl.ds(start_v, group * tk)], p, preferred_element_type=F32)
                for i, j in enumerate(slabs):
                    alpha = a_bufs[values[1]][:, j * LANES:(j + 1) * LANES]
                    acc_s[j, 0:vrows] = alpha * acc_s[j, 0:vrows] + pv[:, i * LANES:(i + 1) * LANES]
            for g in range(group):
                if softmax is not None:
                    slot_m = softmax[1] * group + g
                    for j in slabs:
                        lcols = slice(j * LANES, (j + 1) * LANES)
                        p_bufs[slot_m][j, 0:tk] = jnp.exp2(s_bufs[slot_m][j, 0:tk] - m_new[:, lcols]).astype(BF16)
                if scores is not None:
                    slot_s = scores[1] * group + g
                    start_s = pl.multiple_of(((scores[0] % gpt) * group + g) * tk, tk)
                    s = jnp.dot(k_ref[0, pl.ds(start_s, tk), :], q2t_s[:, cols], preferred_element_type=F32)
                    for i, j in enumerate(slabs):
                        s_bufs[slot_s][j, 0:tk] = s[:, i * LANES:(i + 1) * LANES]
                    x_bufs[slot_s][:, cols] = jnp.max(s, axis=0, keepdims=True)

    m_s[...] = jnp.full(m_s.shape, -jnp.inf, F32)
    acc_s[...] = jnp.zeros(acc_s.shape, F32)
    load_queries(0)
    work(scores=(0, 0))
    work(softmax=(0, 0), scores=(1, 1))

    def body(u, carry):
        pl.when((u + 2) % gpt == 0)(lambda: load_queries((u + 2) // gpt))
        for half in range(2):
            pl.when(u % 2 == half)(
                lambda half=half: work(values=(u, half), softmax=(u + 1, 1 - half), scores=(u + 2, half)))
        pl.when(u % gpt == gpt - 1)(lambda: finish(u // gpt))
        return carry

    lax.fori_loop(0, n_steps - 2, body, 0)
    work(values=(n_steps - 2, n_steps % 2), softmax=(n_steps - 1, (n_steps - 1) % 2))
    work(values=(n_steps - 1, (n_steps - 1) % 2))
    finish(n_steps // gpt - 1)


def _flash(qt, k, vt, lq1, lk1, lq2, lk2, subln_g, lambda_init, tq, tk, group):
    b, seq, d = k.shape
    heads, hd, _ = qt.shape
    vrows = vt.shape[1]
    slots = 2 * group
    gpt = seq // (tk * group)
    assert seq % (tk * group) == 0 and gpt >= 2 and gpt % 2 == 0, (seq, tk, group)

    def once(shape, index_map):
        return pl.BlockSpec(shape, index_map, pipeline_mode=pl.Buffered(1))

    small = _const_spec((1, HEAD_DIM))
    return pl.pallas_call(
        functools.partial(_flash_kernel, tq=tq, tk=tk, group=group, lambda_init=lambda_init),
        grid=(b, heads),
        in_specs=[once((1, hd, seq), lambda bi, h: (h, 0, bi)),
                  once((1, seq, hd), lambda bi, h: (bi, 0, h)),
                  once((1, vrows, seq), lambda bi, h: (h, 0, bi)),
                  small, small, small, small, _const_spec((1, hd))],
        out_specs=once((1, seq, hd), lambda bi, h: (bi, 0, h)),
        out_shape=jax.ShapeDtypeStruct((b, seq, d), BF16),
        scratch_shapes=[pltpu.VMEM((hd, 2 * tq), BF16), pltpu.VMEM((1, 2 * tq), F32),
                        pltpu.VMEM((2 * tq // LANES, vrows + F32_ROWS, LANES), F32)]
        + [pltpu.VMEM((2 * tq // LANES, tk + F32_ROWS, LANES), F32)] * slots
        + [pltpu.VMEM((2 * tq // LANES, tk + BF16_ROWS, LANES), BF16)] * slots
        + [pltpu.VMEM((1, 2 * tq), F32)] * slots
        + [pltpu.VMEM((1, 2 * tq), F32)] * 2,
        compiler_params=_params(2),
        name="flash_diff_attn",
    )(qt, k, vt, lq1, lk1, lq2, lk2, subln_g)


def _flash_short_kernel(qt_ref, k_ref, vt_ref, lq1_ref, lk1_ref, lq2_ref, lk2_ref, g_ref, o_ref,
                        q2t_s, *bufs, tk, lambda_init):
    hd, tq = qt_ref.shape[1], qt_ref.shape[2]
    seq = k_ref.shape[1]
    slabs_per_piece = PIECE_COLS // LANES
    s_bufs, p_bufs, x_bufs, acc_bufs = (bufs[2 * i:2 * i + 2] for i in range(4))
    step_id = pl.program_id(0)

    @pl.when(step_id == 0)
    def _():
        for buf in s_bufs + p_bufs + x_bufs:
            buf[...] = jnp.zeros(buf.shape, buf.dtype)
        for buf in acc_bufs:
            buf[...] = jnp.ones(buf.shape, buf.dtype)

    def step(par):
        acc = acc_bufs[1 - par][...]
        o_maps = acc[0:hd] / acc[hd:hd + 1]
        lam = (jnp.exp(jnp.sum(lq1_ref[...] * lk1_ref[...], axis=-1, keepdims=True))
               - jnp.exp(jnp.sum(lq2_ref[...] * lk2_ref[...], axis=-1, keepdims=True)) + lambda_init)
        o = (o_maps[:, 0:tq] - lam * o_maps[:, tq:2 * tq]).T
        o_ref[0] = (_rms(o, g_ref[...]) * (1.0 - lambda_init)).astype(o_ref.dtype)

        qt = qt_ref[0]
        row = lax.broadcasted_iota(jnp.int32, qt.shape, 0)
        zero = jnp.zeros_like(qt)
        q2t_s[:, 0:tq] = jnp.where(row < HEAD_DIM, qt, zero)
        q2t_s[:, tq:2 * tq] = jnp.where(row >= HEAD_DIM, qt, zero)

        m_prev = x_bufs[1 - par][...]
        vk = VALUE_CHUNKS * tk
        for c in range(2 * tq // PIECE_COLS):
            cols = slice(c * PIECE_COLS, (c + 1) * PIECE_COLS)
            slabs = range(c * slabs_per_piece, (c + 1) * slabs_per_piece)
            pv, col_max = None, None
            for g in range(seq // vk):
                krows = slice(g * vk, (g + 1) * vk)
                p = jnp.concatenate([p_bufs[par][j, krows] for j in slabs], axis=1)
                part = jnp.dot(vt_ref[0, :, krows], p, preferred_element_type=F32)
                pv = part if pv is None else pv + part
                for r in range(g * VALUE_CHUNKS, (g + 1) * VALUE_CHUNKS):
                    rows = slice(r * tk, (r + 1) * tk)
                    for j in slabs:
                        lcols = slice(j * LANES, (j + 1) * LANES)
                        p_bufs[1 - par][j, rows] = jnp.exp2(s_bufs[1 - par][j, rows] - m_prev[:, lcols]).astype(BF16)
                    s = jnp.dot(k_ref[0, rows, :], q2t_s[:, cols], preferred_element_type=F32)
                    for i, j in enumerate(slabs):
                        s_bufs[par][j, rows] = s[:, i * LANES:(i + 1) * LANES]
                    chunk_max = jnp.max(s, axis=0, keepdims=True)
                    col_max = chunk_max if col_max is None else jnp.maximum(col_max, chunk_max)
            acc_bufs[par][:, cols] = pv
            x_bufs[par][:, cols] = col_max

    for par in range(2):
        pl.when(step_id % 2 == par)(functools.partial(step, par))


def _flash_short(qt, k, vt, lq1, lk1, lq2, lk2, subln_g, lambda_init, tq, tk):
    b, seq, d = k.shape
    heads, hd, _ = qt.shape
    vrows = vt.shape[1]
    qtiles = seq // tq
    n_items = b * heads * qtiles
    assert seq % (VALUE_CHUNKS * tk) == 0, (seq, tk)

    def item(i, lag):
        i = jnp.clip(i - lag, 0, n_items - 1)
        return i // (qtiles * heads), (i // qtiles) % heads, i % qtiles

    small = _const_spec((1, HEAD_DIM))
    n_slabs = 2 * tq // LANES
    return pl.pallas_call(
        functools.partial(_flash_short_kernel, tk=tk, lambda_init=lambda_init),
        grid=(n_items + 3,),
        in_specs=[pl.BlockSpec((1, hd, tq), lambda i: (item(i, 0)[1], 0, item(i, 0)[0] * qtiles + item(i, 0)[2])),
                  pl.BlockSpec((1, seq, hd), lambda i: (item(i, 0)[0], 0, item(i, 0)[1])),
                  pl.BlockSpec((1, vrows, seq), lambda i: (item(i, 2)[1], 0, item(i, 2)[0])),
                  small, small, small, small, _const_spec((1, hd))],
        out_specs=pl.BlockSpec((1, tq, hd), lambda i: (item(i, 3)[0], item(i, 3)[2], item(i, 3)[1])),
        out_shape=jax.ShapeDtypeStruct((b, seq, d), BF16),
        scratch_shapes=[pltpu.VMEM((hd, 2 * tq), BF16)]
        + [pltpu.VMEM((n_slabs, seq + F32_ROWS, LANES), F32)] * 2
        + [pltpu.VMEM((n_slabs, seq + BF16_ROWS, LANES), BF16)] * 2
        + [pltpu.VMEM((1, 2 * tq), F32)] * 2
        + [pltpu.VMEM((vrows, 2 * tq), F32)] * 2,
        compiler_params=_params(1),
        name="flash_diff_attn_short",
    )(qt, k, vt, lq1, lk1, lq2, lk2, subln_g)


def _attn_post_kernel(x_ref, o_ref, w_ref, y_ref):
    y_ref[...] = x_ref[...] + jnp.dot(o_ref[...], w_ref[...], preferred_element_type=F32)


def _attn_post(x, o, w_o, tm):
    n, d = x.shape
    row = pl.BlockSpec((tm, d), lambda i: (i, 0))
    return pl.pallas_call(
        _attn_post_kernel,
        grid=(n // tm,),
        in_specs=[row, row, _const_spec((d, d))],
        out_specs=row,
        out_shape=jax.ShapeDtypeStruct((n, d), F32),
        compiler_params=_params(1),
        name="attn_post",
    )(x, o, w_o)


def _ffn_ple_kernel(x_ref, xp_ref, xn_ref, p_ref, gf_ref, wup_ref, cw_ref, cb_ref, wdn_ref,
                    gp_ref, wg_ref, wp_ref, gfin_ref, y_ref, a_s, act_s, *, tiles_per_seq, fc, final):
    tm, d = x_ref.shape
    f = wdn_ref.shape[0]
    i = pl.program_id(0)
    keep_prev = jnp.where(i % tiles_per_seq == 0, 0.0, 1.0).astype(F32)
    keep_next = jnp.where(i % tiles_per_seq == tiles_per_seq - 1, 0.0, 1.0).astype(F32)
    x = x_ref[...]
    gf = gf_ref[...]
    h = jnp.concatenate([_rms(xp_ref[...], gf) * keep_prev, _rms(x, gf), _rms(xn_ref[...], gf) * keep_next], axis=0)
    a_s[...] = jnp.dot(h.astype(BF16), wup_ref[...], preferred_element_type=F32)

    def conv(col0):
        cols = pl.ds(col0, fc)
        c = cb_ref[:, cols]
        for t in range(CONV_WIDTH):
            c = c + a_s[pl.ds(HALO - 1 + t, tm), cols] * cw_ref[pl.ds(t, 1), cols]
        return c

    for j in range(f // fc):
        val = conv(j * fc)
        gate = conv(f + j * fc)
        act_s[:, pl.ds(j * fc, fc)] = (gate * jax.nn.sigmoid(gate) * val).astype(BF16)

    x = x + jnp.dot(act_s[...], wdn_ref[...], preferred_element_type=F32)
    gate = jax.nn.sigmoid(jnp.dot(_rms(x, gp_ref[...]).astype(BF16), wg_ref[...], preferred_element_type=F32))
    x = x + gate * jnp.dot(p_ref[...].astype(BF16), wp_ref[...], preferred_element_type=F32)
    if final:
        x = _rms(x, gfin_ref[...])
    y_ref[...] = x


def _ffn_ple(x, p, g_ffn, w_up, conv_w, conv_b, w_down, g_ple, w_gate, w_proj, g_final, layer, seq, tm, final):
    n, d = x.shape
    f = w_down.shape[1]
    pd = p.shape[2]

    def layer_spec(*shape):
        return pl.BlockSpec((None,) + shape, lambda i: (layer,) + (0,) * len(shape), pipeline_mode=pl.Buffered(1))

    fc = 2 * LANES
    tiles_per_seq = seq // tm
    hb = tm // HALO
    last_blk = n // HALO - 1
    return pl.pallas_call(
        functools.partial(_ffn_ple_kernel, tiles_per_seq=tiles_per_seq, fc=fc, final=final),
        grid=(n // tm,),
        in_specs=[pl.BlockSpec((tm, d), lambda i: (i, 0)),
                  pl.BlockSpec((HALO, d), lambda i: (jnp.maximum(i * hb - 1, 0), 0)),
                  pl.BlockSpec((HALO, d), lambda i: (jnp.minimum((i + 1) * hb, last_blk), 0)),
                  pl.BlockSpec((None, tm, pd), lambda i: (layer, i, 0)),
                  layer_spec(1, d), layer_spec(d, 2 * f), layer_spec(CONV_WIDTH, 2 * f),
                  layer_spec(1, 2 * f), layer_spec(f, d), layer_spec(1, d), layer_spec(d, d),
                  layer_spec(pd, d), _const_spec((1, d))],
        out_specs=pl.BlockSpec((tm, d), lambda i: (i, 0)),
        out_shape=jax.ShapeDtypeStruct((n, d), F32),
        scratch_shapes=[pltpu.VMEM((tm + 2 * HALO, 2 * f), F32), pltpu.VMEM((tm, f), BF16)],
        compiler_params=_params(1),
        name="ffn_ple_final" if final else "ffn_ple",
    )(x, x, x, p, g_ffn, w_up, conv_w, conv_b, w_down, g_ple, w_gate, w_proj, g_final)


def _sgu_kernel(x_ref, g_ref, wuv_ref, lng_ref, lnb_ref, ws_ref, bs_ref, wout_ref, y_ref, um_s):
    tm, d = x_ref.shape
    w = wout_ref.shape[0]
    x = x_ref[...]
    z = jnp.dot(_rms(x, g_ref[...]).astype(BF16), wuv_ref[...], preferred_element_type=F32)
    z = 0.5 * z * (1.0 + lax.erf(z * (1.0 / math.sqrt(2.0))))
    v = z[:, w:]
    vc = v - jnp.mean(v, axis=-1, keepdims=True)
    v = vc * lax.rsqrt(jnp.mean(vc * vc, axis=-1, keepdims=True) + EPS) * lng_ref[...] + lnb_ref[...]
    vb = v.astype(BF16)
    for c in range(tm // CHUNK):
        rows = slice(c * CHUNK, (c + 1) * CHUNK)
        for grp in range(w // CHUNK):
            cols = slice(grp * CHUNK, (grp + 1) * CHUNK)
            mixed = jnp.dot(ws_ref[grp], vb[rows, cols], preferred_element_type=F32) + bs_ref[grp]
            um_s[rows, cols] = (z[rows, cols] * mixed).astype(BF16)
    y_ref[...] = x + jnp.dot(um_s[...], wout_ref[...], preferred_element_type=F32)


def _sgu(x, g, w_uv, ln_g, ln_b, w_s, b_s, w_out, tm):
    n, d = x.shape
    w = w_out.shape[0]
    ng = w // CHUNK
    row = pl.BlockSpec((tm, d), lambda i: (i, 0))
    return pl.pallas_call(
        _sgu_kernel,
        grid=(n // tm,),
        in_specs=[row, _const_spec((1, d)), _const_spec((d, 2 * w)), _const_spec((1, w)), _const_spec((1, w)),
                  _const_spec((ng, CHUNK, CHUNK)), _const_spec((ng, CHUNK, 1)), _const_spec((w, d))],
        out_specs=row,
        out_shape=jax.ShapeDtypeStruct((n, d), F32),
        scratch_shapes=[pltpu.VMEM((tm, w), BF16)],
        compiler_params=_params(1),
        name="sgu",
    )(x, g, w_uv, ln_g, ln_b, w_s, b_s, w_out)


def _rope_tables(seq):
    half = HEAD_DIM // 2
    inv = 1.0 / (ROPE_THETA ** (jnp.arange(0, HEAD_DIM, 2, dtype=F32) / HEAD_DIM))
    ang = jnp.arange(seq, dtype=F32)[:, None] * inv[None, :]
    cos, sin = jnp.cos(ang), jnp.sin(ang)
    sign = jnp.where((jnp.arange(LANES) % HEAD_DIM) < half, -1.0, 1.0).astype(F32)
    reps = LANES // half
    return jnp.concatenate([cos] * reps, axis=-1), jnp.concatenate([sin] * reps, axis=-1) * sign


def _flash_tiles(seq):
    tq = _tile(seq, 2048)
    tk = _tile(seq, LANES)
    group = max(1, min(4, seq // tk // 4))
    return tq, tk, group


def _tile(n, want):
    t = min(n, want)
    assert n % t == 0, (n, t)
    return t


def _trunk(x, p, w, rope):
    b, seq, d = x.shape
    n = b * seq
    x = x.reshape(n, d)
    tm = _tile(seq, 512)
    tm_wide = _tile(seq, 1024)
    qt, k, vt = _attn_pre(x, w["norm_mix_g"][0], w["attn_w_qkv"], rope, seq, tm_wide)
    lambda_init = 0.8 - 0.6 * math.exp(-0.3 * 0)
    lam_args = (w["attn_lq1"], w["attn_lk1"], w["attn_lq2"], w["attn_lk2"], w["attn_subln_g"], lambda_init)
    if seq <= SHORT_SEQ:
        o = _flash_short(qt, k.reshape(b, seq, d), vt, *lam_args, _tile(seq, 512), 2 * LANES)
    else:
        o = _flash(qt, k.reshape(b, seq, d), vt, *lam_args, *_flash_tiles(seq))
    x = _attn_post(x, o.reshape(n, d), w["attn_w_o"], _tile(seq, 2048))
    p = p.reshape(p.shape[0], n, p.shape[-1])
    ffn = functools.partial(_ffn_ple, p=p, g_ffn=w["norm_ffn_g"], w_up=w["ffn_w_up"], conv_w=w["ffn_conv_w"],
                            conv_b=w["ffn_conv_b"], w_down=w["ffn_w_down"], g_ple=w["norm_ple_g"],
                            w_gate=w["ple_w_gate"], w_proj=w["ple_w_proj"], g_final=w["final_norm_g"], seq=seq, tm=tm)
    x = ffn(x, layer=0, final=False)
    x = _sgu(x, w["norm_mix_g"][1], w["sgu_w_uv"], w["sgu_ln_g"], w["sgu_ln_b"], w["sgu_w_s"], w["sgu_b_s"],
             w["sgu_w_out"], tm_wide)
    x = ffn(x, layer=1, final=True)
    return x.reshape(b, seq, d)


def kernel(x_prompt, x_sample, p_prompt, p_sample, norm_mix_g, attn_w_qkv, attn_lq1, attn_lk1, attn_lq2, attn_lk2, attn_subln_g, attn_w_o, sgu_w_uv, sgu_ln_g, sgu_ln_b, sgu_w_s, sgu_b_s, sgu_w_out, norm_ffn_g, ffn_w_up, ffn_conv_w, ffn_conv_b, ffn_w_down, norm_ple_g, ple_w_gate, ple_w_proj, final_norm_g):
    depth = norm_mix_g.shape[0]
    assert depth == 2 and attn_w_qkv.shape[0] == 1 and sgu_w_uv.shape[0] == 1
    d = x_prompt.shape[-1]
    row = lambda a: a.reshape(a.shape[:-1] + (1, a.shape[-1]))
    w = dict(
        norm_mix_g=row(norm_mix_g), norm_ffn_g=row(norm_ffn_g), norm_ple_g=row(norm_ple_g),
        final_norm_g=final_norm_g.reshape(1, d),
        attn_w_qkv=attn_w_qkv[0].astype(BF16), attn_w_o=attn_w_o[0].astype(BF16),
        attn_lq1=attn_lq1, attn_lk1=attn_lk1, attn_lq2=attn_lq2, attn_lk2=attn_lk2, attn_subln_g=attn_subln_g,
        sgu_w_uv=sgu_w_uv[0].astype(BF16), sgu_ln_g=sgu_ln_g, sgu_ln_b=sgu_ln_b,
        sgu_w_s=sgu_w_s[0].astype(BF16), sgu_b_s=sgu_b_s[0][:, :, None], sgu_w_out=sgu_w_out[0].astype(BF16),
        ffn_w_up=ffn_w_up.astype(BF16), ffn_conv_w=ffn_conv_w, ffn_conv_b=row(ffn_conv_b),
        ffn_w_down=ffn_w_down.astype(BF16), ple_w_gate=ple_w_gate.astype(BF16), ple_w_proj=ple_w_proj.astype(BF16),
    )
    rope = _rope_tables(max(x_prompt.shape[1], x_sample.shape[1]))
    return _trunk(x_prompt, p_prompt, w, rope), _trunk(x_sample, p_sample, w, rope)
```

```python
import functools
import math

import jax
import jax.numpy as jnp
from jax import lax
from jax.experimental import pallas as pl
from jax.experimental.pallas import tpu as pltpu

F32 = jnp.float32
BF16 = jnp.bfloat16

EPS = 1e-6
ROPE_THETA = 10000.0
HEAD_DIM = 64
CHUNK = 128
CONV_WIDTH = 3
F32_ROWS = 8
BF16_ROWS = 16
HALO = F32_ROWS
LANES = 128
MXU_COLS = 256
PIECE_COLS = MXU_COLS
SHORT_SEQ = 2048
VALUE_CHUNKS = 2
ONES_ROWS = 16
Q_SCALE = HEAD_DIM ** -0.5 * math.log2(math.e)

VMEM_LIMIT = 56 * 1024 * 1024


def _rms(x, g):
    return x * lax.rsqrt(jnp.mean(x * x, axis=-1, keepdims=True) + EPS) * g


def _const_spec(shape):
    nd = len(shape)
    return pl.BlockSpec(shape, lambda *_: (0,) * nd, pipeline_mode=pl.Buffered(1))


def _params(n_axes):
    return pltpu.CompilerParams(dimension_semantics=("arbitrary",) * n_axes, vmem_limit_bytes=VMEM_LIMIT)


def _attn_pre_kernel(x_ref, g_ref, w_ref, cos_ref, sin_ref, qt_ref, k_ref, vt_ref):
    tm, d = x_ref.shape
    hd = 2 * HEAD_DIM
    h = _rms(x_ref[...], g_ref[...]).astype(BF16)
    qkv = jnp.dot(h, w_ref[...], preferred_element_type=F32)
    lane = lax.broadcasted_iota(jnp.int32, (tm, LANES), 1)
    first_half = (lane % HEAD_DIM) < (HEAD_DIM // 2)

    def rope(t, cos, sin):
        rot = jnp.where(first_half, pltpu.roll(t, LANES - HEAD_DIM // 2, 1), pltpu.roll(t, HEAD_DIM // 2, 1))
        return t * cos + rot * sin

    cos, sin = cos_ref[...], sin_ref[...]
    for j in range(d // hd):
        cols = slice(j * hd, (j + 1) * hd)
        qt_ref[j] = (rope(qkv[:, j * hd:(j + 1) * hd], cos, sin) * Q_SCALE).T.astype(BF16)
        k_ref[:, cols] = rope(qkv[:, d + j * hd:d + (j + 1) * hd], cos, sin).astype(BF16)
        vt_ref[j, 0:hd, :] = qkv[:, 2 * d + j * hd:2 * d + (j + 1) * hd].T.astype(BF16)
        vt_ref[j, hd:hd + ONES_ROWS, :] = jnp.ones((ONES_ROWS, tm), BF16)


def _attn_pre(x, g, w_qkv, tables, seq, tm):
    n, d = x.shape
    hd = 2 * HEAD_DIM
    heads = d // hd
    tps = seq // tm
    tab_spec = pl.BlockSpec((tm, LANES), lambda i: (i % tps, 0))
    return pl.pallas_call(
        _attn_pre_kernel,
        grid=(n // tm,),
        in_specs=[pl.BlockSpec((tm, d), lambda i: (i, 0)), _const_spec((1, d)), _const_spec((d, 3 * d)),
                  tab_spec, tab_spec],
        out_specs=[pl.BlockSpec((heads, hd, tm), lambda i: (0, 0, i)),
                   pl.BlockSpec((tm, d), lambda i: (i, 0)),
                   pl.BlockSpec((heads, hd + ONES_ROWS, tm), lambda i: (0, 0, i))],
        out_shape=[jax.ShapeDtypeStruct((heads, hd, n), BF16), jax.ShapeDtypeStruct((n, d), BF16),
                   jax.ShapeDtypeStruct((heads, hd + ONES_ROWS, n), BF16)],
        compiler_params=_params(1),
        name="attn_pre",
    )(x, g, w_qkv, *tables)


def _flash_kernel(qt_ref, k_ref, vt_ref, lq1_ref, lk1_ref, lq2_ref, lk2_ref, g_ref, o_ref,
                  q2t_s, m_s, acc_s, *bufs, tq, tk, group, lambda_init):
    hd = qt_ref.shape[1]
    seq = k_ref.shape[1]
    slots = 2 * group
    s_bufs, p_bufs, x_bufs, a_bufs = (bufs[i * slots:(i + 1) * slots] for i in range(4))
    gpt = seq // (tk * group)
    n_steps = (seq // tq) * gpt
    slabs_per_piece = PIECE_COLS // LANES
    vrows = vt_ref.shape[1]

    def load_queries(tile):
        qt = qt_ref[0, :, pl.ds(pl.multiple_of(tile * tq, tq), tq)]
        row = lax.broadcasted_iota(jnp.int32, qt.shape, 0)
        zero = jnp.zeros_like(qt)
        q2t_s[:, 0:tq] = jnp.where(row < HEAD_DIM, qt, zero)
        q2t_s[:, tq:2 * tq] = jnp.where(row >= HEAD_DIM, qt, zero)

    def finish(tile):
        acc = jnp.concatenate([acc_s[j, 0:vrows] for j in range(2 * tq // LANES)], axis=1)
        o_maps = acc[0:hd] / acc[hd:hd + 1]
        lam = (jnp.exp(jnp.sum(lq1_ref[...] * lk1_ref[...], axis=-1, keepdims=True))
               - jnp.exp(jnp.sum(lq2_ref[...] * lk2_ref[...], axis=-1, keepdims=True)) + lambda_init)
        o = (o_maps[:, 0:tq] - lam * o_maps[:, tq:2 * tq]).T
        o_ref[0, pl.ds(pl.multiple_of(tile * tq, tq), tq), :] = (
            _rms(o, g_ref[...]) * (1.0 - lambda_init)).astype(o_ref.dtype)
        acc_s[...] = jnp.zeros(acc_s.shape, F32)

    def work(values=None, softmax=None, scores=None):
        if softmax is not None:
            slots_m = range(softmax[1] * group, (softmax[1] + 1) * group)
            m_old = jnp.where(softmax[0] % gpt == 0, -jnp.inf, m_s[...])
            m_new = functools.reduce(jnp.maximum, [x_bufs[slot][...] for slot in slots_m], m_old)
            a_bufs[softmax[1]][...] = jnp.exp2(m_old - m_new)
            m_s[...] = m_new
        if values is not None:
            slots_v = range(values[1] * group, (values[1] + 1) * group)
            start_v = pl.multiple_of((values[0] % gpt) * group * tk, group * tk)
        for c in range(2 * tq // PIECE_COLS):
            cols = slice(c * PIECE_COLS, (c + 1) * PIECE_COLS)
            slabs = range(c * slabs_per_piece, (c + 1) * slabs_per_piece)
            if values is not None:
                p = jnp.concatenate([jnp.concatenate([p_bufs[slot][j, 0:tk] for slot in slots_v], axis=0)
                                     for j in slabs], axis=1)
                pv = jnp.dot(vt_ref[0, :, pl.ds(start_v, group * tk)], p, preferred_element_type=F32)
                for i, j in enumerate(slabs):
                    alpha = a_bufs[values[1]][:, j * LANES:(j + 1) * LANES]
                    acc_s[j, 0:vrows] = alpha * acc_s[j, 0:vrows] + pv[:, i * LANES:(i + 1) * LANES]
            for g in range(group):
                if softmax is not None:
                    slot_m = softmax[1] * group + g
                    for j in slabs:
                        lcols = slice(j * LANES, (j + 1) * LANES)
                        p_bufs[slot_m][j, 0:tk] = jnp.exp2(s_bufs[slot_m][j, 0:tk] - m_new[:, lcols]).astype(BF16)
                if scores is not None:
                    slot_s = scores[1] * group + g
                    start_s = pl.multiple_of(((scores[0] % gpt) * group + g) * tk, tk)
                    s = jnp.dot(k_ref[0, pl.ds(start_s, tk), :], q2t_s[:, cols], preferred_element_type=F32)
                    for i, j in enumerate(slabs):
                        s_bufs[slot_s][j, 0:tk] = s[:, i * LANES:(i + 1) * LANES]
                    x_bufs[slot_s][:, cols] = jnp.max(s, axis=0, keepdims=True)

    m_s[...] = jnp.full(m_s.shape, -jnp.inf, F32)
    acc_s[...] = jnp.zeros(acc_s.shape, F32)
    load_queries(0)
    work(scores=(0, 0))
    work(softmax=(0, 0), scores=(1, 1))

    def body(u, carry):
        pl.when((u + 2) % gpt == 0)(lambda: load_queries((u + 2) // gpt))
        for half in range(2):
            pl.when(u % 2 == half)(
                lambda half=half: work(values=(u, half), softmax=(u + 1, 1 - half), scores=(u + 2, half)))
        pl.when(u % gpt == gpt - 1)(lambda: finish(u // gpt))
        return carry

    lax.fori_loop(0, n_steps - 2, body, 0)
    work(values=(n_steps - 2, n_steps % 2), softmax=(n_steps - 1, (n_steps - 1) % 2))
    work(values=(n_steps - 1, (n_steps - 1) % 2))
    finish(n_steps // gpt - 1)


def _flash(qt, k, vt, lq1, lk1, lq2, lk2, subln_g, lambda_init, tq, tk, group):
    b, seq, d = k.shape
    heads, hd, _ = qt.shape
    vrows = vt.shape[1]
    slots = 2 * group
    gpt = seq // (tk * group)
    assert seq % (tk * group) == 0 and gpt >= 2 and gpt % 2 == 0, (seq, tk, group)

    def once(shape, index_map):
        return pl.BlockSpec(shape, index_map, pipeline_mode=pl.Buffered(1))

    small = _const_spec((1, HEAD_DIM))
    return pl.pallas_call(
        functools.partial(_flash_kernel, tq=tq, tk=tk, group=group, lambda_init=lambda_init),
        grid=(b, heads),
        in_specs=[once((1, hd, seq), lambda bi, h: (h, 0, bi)),
                  once((1, seq, hd), lambda bi, h: (bi, 0, h)),
                  once((1, vrows, seq), lambda bi, h: (h, 0, bi)),
                  small, small, small, small, _const_spec((1, hd))],
        out_specs=once((1, seq, hd), lambda bi, h: (bi, 0, h)),
        out_shape=jax.ShapeDtypeStruct((b, seq, d), BF16),
        scratch_shapes=[pltpu.VMEM((hd, 2 * tq), BF16), pltpu.VMEM((1, 2 * tq), F32),
                        pltpu.VMEM((2 * tq // LANES, vrows + F32_ROWS, LANES), F32)]
        + [pltpu.VMEM((2 * tq // LANES, tk + F32_ROWS, LANES), F32)] * slots
        + [pltpu.VMEM((2 * tq // LANES, tk + BF16_ROWS, LANES), BF16)] * slots
        + [pltpu.VMEM((1, 2 * tq), F32)] * slots
        + [pltpu.VMEM((1, 2 * tq), F32)] * 2,
        compiler_params=_params(2),
        name="flash_diff_attn",
    )(qt, k, vt, lq1, lk1, lq2, lk2, subln_g)


def _flash_short_kernel(qt_ref, k_ref, vt_ref, lq1_ref, lk1_ref, lq2_ref, lk2_ref, g_ref, o_ref,
                        q2t_s, *bufs, tk, lambda_init):
    hd, tq = qt_ref.shape[1], qt_ref.shape[2]
    seq = k_ref.shape[1]
    slabs_per_piece = PIECE_COLS // LANES
    s_bufs, p_bufs, x_bufs, acc_bufs = (bufs[2 * i:2 * i + 2] for i in range(4))
    step_id = pl.program_id(0)

    @pl.when(step_id == 0)
    def _():
        for buf in s_bufs + p_bufs + x_bufs:
            buf[...] = jnp.zeros(buf.shape, buf.dtype)
        for buf in acc_bufs:
            buf[...] = jnp.ones(buf.shape, buf.dtype)

    def step(par):
        acc = acc_bufs[1 - par][...]
        o_maps = acc[0:hd] / acc[hd:hd + 1]
        lam = (jnp.exp(jnp.sum(lq1_ref[...] * lk1_ref[...], axis=-1, keepdims=True))
               - jnp.exp(jnp.sum(lq2_ref[...] * lk2_ref[...], axis=-1, keepdims=True)) + lambda_init)
        o = (o_maps[:, 0:tq] - lam * o_maps[:, tq:2 * tq]).T
        o_ref[0] = (_rms(o, g_ref[...]) * (1.0 - lambda_init)).astype(o_ref.dtype)

        qt = qt_ref[0]
        row = lax.broadcasted_iota(jnp.int32, qt.shape, 0)
        zero = jnp.zeros_like(qt)
        q2t_s[:, 0:tq] = jnp.where(row < HEAD_DIM, qt, zero)
        q2t_s[:, tq:2 * tq] = jnp.where(row >= HEAD_DIM, qt, zero)

        m_prev = x_bufs[1 - par][...]
        vk = VALUE_CHUNKS * tk
        for c in range(2 * tq // PIECE_COLS):
            cols = slice(c * PIECE_COLS, (c + 1) * PIECE_COLS)
            slabs = range(c * slabs_per_piece, (c + 1) * slabs_per_piece)
            pv, col_max = None, None
            for g in range(seq // vk):
                krows = slice(g * vk, (g + 1) * vk)
                p = jnp.concatenate([p_bufs[par][j, krows] for j in slabs], axis=1)
                part = jnp.dot(vt_ref[0, :, krows], p, preferred_element_type=F32)
                pv = part if pv is None else pv + part
                for r in range(g * VALUE_CHUNKS, (g + 1) * VALUE_CHUNKS):
                    rows = slice(r * tk, (r + 1) * tk)
                    for j in slabs:
                        lcols = slice(j * LANES, (j + 1) * LANES)
                        p_bufs[1 - par][j, rows] = jnp.exp2(s_bufs[1 - par][j, rows] - m_prev[:, lcols]).astype(BF16)
                    s = jnp.dot(k_ref[0, rows, :], q2t_s[:, cols], preferred_element_type=F32)
                    for i, j in enumerate(slabs):
                        s_bufs[par][j, rows] = s[:, i * LANES:(i + 1) * LANES]
                    chunk_max = jnp.max(s, axis=0, keepdims=True)
                    col_max = chunk_max if col_max is None else jnp.maximum(col_max, chunk_max)
            acc_bufs[par][:, cols] = pv
            x_bufs[par][:, cols] = col_max

    for par in range(2):
        pl.when(step_id % 2 == par)(functools.partial(step, par))


def _flash_short(qt, k, vt, lq1, lk1, lq2, lk2, subln_g, lambda_init, tq, tk):
    b, seq, d = k.shape
    heads, hd, _ = qt.shape
    vrows = vt.shape[1]
    qtiles = seq // tq
    n_items = b * heads * qtiles
    assert seq % (VALUE_CHUNKS * tk) == 0, (seq, tk)

    def item(i, lag):
        i = jnp.clip(i - lag, 0, n_items - 1)
        return i // (qtiles * heads), (i // qtiles) % heads, i % qtiles

    small = _const_spec((1, HEAD_DIM))
    n_slabs = 2 * tq // LANES
    return pl.pallas_call(
        functools.partial(_flash_short_kernel, tk=tk, lambda_init=lambda_init),
        grid=(n_items + 3,),
        in_specs=[pl.BlockSpec((1, hd, tq), lambda i: (item(i, 0)[1], 0, item(i, 0)[0] * qtiles + item(i, 0)[2])),
                  pl.BlockSpec((1, seq, hd), lambda i: (item(i, 0)[0], 0, item(i, 0)[1])),
                  pl.BlockSpec((1, vrows, seq), lambda i: (item(i, 2)[1], 0, item(i, 2)[0])),
                  small, small, small, small, _const_spec((1, hd))],
        out_specs=pl.BlockSpec((1, tq, hd), lambda i: (item(i, 3)[0], item(i, 3)[2], item(i, 3)[1])),
        out_shape=jax.ShapeDtypeStruct((b, seq, d), BF16),
        scratch_shapes=[pltpu.VMEM((hd, 2 * tq), BF16)]
        + [pltpu.VMEM((n_slabs, seq + F32_ROWS, LANES), F32)] * 2
        + [pltpu.VMEM((n_slabs, seq + BF16_ROWS, LANES), BF16)] * 2
        + [pltpu.VMEM((1, 2 * tq), F32)] * 2
        + [pltpu.VMEM((vrows, 2 * tq), F32)] * 2,
        compiler_params=_params(1),
        name="flash_diff_attn_short",
    )(qt, k, vt, lq1, lk1, lq2, lk2, subln_g)


def _attn_post_kernel(x_ref, o_ref, w_ref, y_ref):
    y_ref[...] = x_ref[...] + jnp.dot(o_ref[...], w_ref[...], preferred_element_type=F32)


def _attn_post(x, o, w_o, tm):
    n, d = x.shape
    row = pl.BlockSpec((tm, d), lambda i: (i, 0))
    return pl.pallas_call(
        _attn_post_kernel,
        grid=(n // tm,),
        in_specs=[row, row, _const_spec((d, d))],
        out_specs=row,
        out_shape=jax.ShapeDtypeStruct((n, d), F32),
        compiler_params=_params(1),
        name="attn_post",
    )(x, o, w_o)


def _ffn_ple_kernel(x_ref, xp_ref, xn_ref, p_ref, gf_ref, wup_ref, cw_ref, cb_ref, wdn_ref,
                    gp_ref, wg_ref, wp_ref, gfin_ref, y_ref, a_s, act_s, *, tiles_per_seq, fc, final):
    tm, d = x_ref.shape
    f = wdn_ref.shape[0]
    i = pl.program_id(0)
    keep_prev = jnp.where(i % tiles_per_seq == 0, 0.0, 1.0).astype(F32)
    keep_next = jnp.where(i % tiles_per_seq == tiles_per_seq - 1, 0.0, 1.0).astype(F32)
    x = x_ref[...]
    gf = gf_ref[...]
    h = jnp.concatenate([_rms(xp_ref[...], gf) * keep_prev, _rms(x, gf), _rms(xn_ref[...], gf) * keep_next], axis=0)
    a_s[...] = jnp.dot(h.astype(BF16), wup_ref[...], preferred_element_type=F32)

    def conv(col0):
        cols = pl.ds(col0, fc)
        c = cb_ref[:, cols]
        for t in range(CONV_WIDTH):
            c = c + a_s[pl.ds(HALO - 1 + t, tm), cols] * cw_ref[pl.ds(t, 1), cols]
        return c

    for j in range(f // fc):
        val = conv(j * fc)
        gate = conv(f + j * fc)
        act_s[:, pl.ds(j * fc, fc)] = (gate * jax.nn.sigmoid(gate) * val).astype(BF16)

    x = x + jnp.dot(act_s[...], wdn_ref[...], preferred_element_type=F32)
    gate = jax.nn.sigmoid(jnp.dot(_rms(x, gp_ref[...]).astype(BF16), wg_ref[...], preferred_element_type=F32))
    x = x + gate * jnp.dot(p_ref[...].astype(BF16), wp_ref[...], preferred_element_type=F32)
    if final:
        x = _rms(x, gfin_ref[...])
    y_ref[...] = x


def _ffn_ple(x, p, g_ffn, w_up, conv_w, conv_b, w_down, g_ple, w_gate, w_proj, g_final, layer, seq, tm, final):
    n, d = x.shape
    f = w_down.shape[1]
    pd = p.shape[2]

    def layer_spec(*shape):
        return pl.BlockSpec((None,) + shape, lambda i: (layer,) + (0,) * len(shape), pipeline_mode=pl.Buffered(1))

    fc = 2 * LANES
    tiles_per_seq = seq // tm
    hb = tm // HALO
    last_blk = n // HALO - 1
    return pl.pallas_call(
        functools.partial(_ffn_ple_kernel, tiles_per_seq=tiles_per_seq, fc=fc, final=final),
        grid=(n // tm,),
        in_specs=[pl.BlockSpec((tm, d), lambda i: (i, 0)),
                  pl.BlockSpec((HALO, d), lambda i: (jnp.maximum(i * hb - 1, 0), 0)),
                  pl.BlockSpec((HALO, d), lambda i: (jnp.minimum((i + 1) * hb, last_blk), 0)),
                  pl.BlockSpec((None, tm, pd), lambda i: (layer, i, 0)),
                  layer_spec(1, d), layer_spec(d, 2 * f), layer_spec(CONV_WIDTH, 2 * f),
                  layer_spec(1, 2 * f), layer_spec(f, d), layer_spec(1, d), layer_spec(d, d),
                  layer_spec(pd, d), _const_spec((1, d))],
        out_specs=pl.BlockSpec((tm, d), lambda i: (i, 0)),
        out_shape=jax.ShapeDtypeStruct((n, d), F32),
        scratch_shapes=[pltpu.VMEM((tm + 2 * HALO, 2 * f), F32), pltpu.VMEM((tm, f), BF16)],
        compiler_params=_params(1),
        name="ffn_ple_final" if final else "ffn_ple",
    )(x, x, x, p, g_ffn, w_up, conv_w, conv_b, w_down, g_ple, w_gate, w_proj, g_final)


def _sgu_kernel(x_ref, g_ref, wuv_ref, lng_ref, lnb_ref, ws_ref, bs_ref, wout_ref, y_ref, um_s):
    tm, d = x_ref.shape
    w = wout_ref.shape[0]
    x = x_ref[...]
    z = jnp.dot(_rms(x, g_ref[...]).astype(BF16), wuv_ref[...], preferred_element_type=F32)
    z = 0.5 * z * (1.0 + lax.erf(z * (1.0 / math.sqrt(2.0))))
    v = z[:, w:]
    vc = v - jnp.mean(v, axis=-1, keepdims=True)
    v = vc * lax.rsqrt(jnp.mean(vc * vc, axis=-1, keepdims=True) + EPS) * lng_ref[...] + lnb_ref[...]
    vb = v.astype(BF16)
    for c in range(tm // CHUNK):
        rows = slice(c * CHUNK, (c + 1) * CHUNK)
        for grp in range(w // CHUNK):
            cols = slice(grp * CHUNK, (grp + 1) * CHUNK)
            mixed = jnp.dot(ws_ref[grp], vb[rows, cols], preferred_element_type=F32) + bs_ref[grp]
            um_s[rows, cols] = (z[rows, cols] * mixed).astype(BF16)
    y_ref[...] = x + jnp.dot(um_s[...], wout_ref[...], preferred_element_type=F32)


def _sgu(x, g, w_uv, ln_g, ln_b, w_s, b_s, w_out, tm):
    n, d = x.shape
    w = w_out.shape[0]
    ng = w // CHUNK
    row = pl.BlockSpec((tm, d), lambda i: (i, 0))
    return pl.pallas_call(
        _sgu_kernel,
        grid=(n // tm,),
        in_specs=[row, _const_spec((1, d)), _const_spec((d, 2 * w)), _const_spec((1, w)), _const_spec((1, w)),
                  _const_spec((ng, CHUNK, CHUNK)), _const_spec((ng, CHUNK, 1)), _const_spec((w, d))],
        out_specs=row,
        out_shape=jax.ShapeDtypeStruct((n, d), F32),
        scratch_shapes=[pltpu.VMEM((tm, w), BF16)],
        compiler_params=_params(1),
        name="sgu",
    )(x, g, w_uv, ln_g, ln_b, w_s, b_s, w_out)


def _rope_tables(seq):
    half = HEAD_DIM // 2
    inv = 1.0 / (ROPE_THETA ** (jnp.arange(0, HEAD_DIM, 2, dtype=F32) / HEAD_DIM))
    ang = jnp.arange(seq, dtype=F32)[:, None] * inv[None, :]
    cos, sin = jnp.cos(ang), jnp.sin(ang)
    sign = jnp.where((jnp.arange(LANES) % HEAD_DIM) < half, -1.0, 1.0).astype(F32)
    reps = LANES // half
    return jnp.concatenate([cos] * reps, axis=-1), jnp.concatenate([sin] * reps, axis=-1) * sign


def _flash_tiles(seq):
    tq = _tile(seq, 1024)
    tk = _tile(seq, LANES)
    group = max(1, min(8, seq // tk // 4))
    return tq, tk, group


def _tile(n, want):
    t = min(n, want)
    assert n % t == 0, (n, t)
    return t


def _trunk(x, p, w, rope):
    b, seq, d = x.shape
    n = b * seq
    x = x.reshape(n, d)
    tm = _tile(seq, 512)
    tm_wide = _tile(seq, 1024)
    qt, k, vt = _attn_pre(x, w["norm_mix_g"][0], w["attn_w_qkv"], rope, seq, tm_wide)
    lambda_init = 0.8 - 0.6 * math.exp(-0.3 * 0)
    lam_args = (w["attn_lq1"], w["attn_lk1"], w["attn_lq2"], w["attn_lk2"], w["attn_subln_g"], lambda_init)
    if seq <= SHORT_SEQ:
        o = _flash_short(qt, k.reshape(b, seq, d), vt, *lam_args, _tile(seq, 512), 2 * LANES)
    else:
        o = _flash(qt, k.reshape(b, seq, d), vt, *lam_args, *_flash_tiles(seq))
    x = _attn_post(x, o.reshape(n, d), w["attn_w_o"], _tile(seq, 2048))
    p = p.reshape(p.shape[0], n, p.shape[-1])
    ffn = functools.partial(_ffn_ple, p=p, g_ffn=w["norm_ffn_g"], w_up=w["ffn_w_up"], conv_w=w["ffn_conv_w"],
                            conv_b=w["ffn_conv_b"], w_down=w["ffn_w_down"], g_ple=w["norm_ple_g"],
                            w_gate=w["ple_w_gate"], w_proj=w["ple_w_proj"], g_final=w["final_norm_g"], seq=seq, tm=tm)
    x = ffn(x, layer=0, final=False)
    x = _sgu(x, w["norm_mix_g"][1], w["sgu_w_uv"], w["sgu_ln_g"], w["sgu_ln_b"], w["sgu_w_s"], w["sgu_b_s"],
             w["sgu_w_out"], tm_wide)
    x = ffn(x, layer=1, final=True)
    return x.reshape(b, seq, d)


def kernel(x_prompt, x_sample, p_prompt, p_sample, norm_mix_g, attn_w_qkv, attn_lq1, attn_lk1, attn_lq2, attn_lk2, attn_subln_g, attn_w_o, sgu_w_uv, sgu_ln_g, sgu_ln_b, sgu_w_s, sgu_b_s, sgu_w_out, norm_ffn_g, ffn_w_up, ffn_conv_w, ffn_conv_b, ffn_w_down, norm_ple_g, ple_w_gate, ple_w_proj, final_norm_g):
    depth = norm_mix_g.shape[0]
    assert depth == 2 and attn_w_qkv.shape[0] == 1 and sgu_w_uv.shape[0] == 1
    d = x_prompt.shape[-1]
    row = lambda a: a.reshape(a.shape[:-1] + (1, a.shape[-1]))
    w = dict(
        norm_mix_g=row(norm_mix_g), norm_ffn_g=row(norm_ffn_g), norm_ple_g=row(norm_ple_g),
        final_norm_g=final_norm_g.reshape(1, d),
        attn_w_qkv=attn_w_qkv[0].astype(BF16), attn_w_o=attn_w_o[0].astype(BF16),
        attn_lq1=attn_lq1, attn_lk1=attn_lk1, attn_lq2=attn_lq2, attn_lk2=attn_lk2, attn_subln_g=attn_subln_g,
        sgu_w_uv=sgu_w_uv[0].astype(BF16), sgu_ln_g=sgu_ln_g, sgu_ln_b=sgu_ln_b,
        sgu_w_s=sgu_w_s[0].astype(BF16), sgu_b_s=sgu_b_s[0][:, :, None], sgu_w_out=sgu_w_out[0].astype(BF16),
        ffn_w_up=ffn_w_up.astype(BF16), ffn_conv_w=ffn_conv_w, ffn_conv_b=row(ffn_conv_b),
        ffn_w_down=ffn_w_down.astype(BF16), ple_w_gate=ple_w_gate.astype(BF16), ple_w_proj=ple_w_proj.astype(BF16),
    )
    rope = _rope_tables(max(x_prompt.shape[1], x_sample.shape[1]))
    return _trunk(x_prompt, p_prompt, w, rope), _trunk(x_sample, p_sample, w, rope)
```

```python
import functools
import math

import jax
import jax.numpy as jnp
from jax import lax
from jax.experimental import pallas as pl
from jax.experimental.pallas import tpu as pltpu

F32 = jnp.float32
BF16 = jnp.bfloat16

EPS = 1e-6
ROPE_THETA = 10000.0
HEAD_DIM = 64
CHUNK = 128
CONV_WIDTH = 3
F32_ROWS = 8
BF16_ROWS = 16
HALO = F32_ROWS
LANES = 128
MXU_COLS = 256
PIECE_COLS = MXU_COLS
SHORT_SEQ = 2048
VALUE_CHUNKS = 2
ONES_ROWS = 16
Q_SCALE = HEAD_DIM ** -0.5 * math.log2(math.e)

VMEM_LIMIT = 60 * 1024 * 1024


def _rms(x, g):
    return x * lax.rsqrt(jnp.mean(x * x, axis=-1, keepdims=True) + EPS) * g


def _const_spec(shape):
    nd = len(shape)
    return pl.BlockSpec(shape, lambda *_: (0,) * nd, pipeline_mode=pl.Buffered(1))


def _params(n_axes):
    return pltpu.CompilerParams(dimension_semantics=("arbitrary",) * n_axes, vmem_limit_bytes=VMEM_LIMIT)


def _attn_pre_kernel(x_ref, g_ref, w_ref, cos_ref, sin_ref, qt_ref, k_ref, vt_ref):
    tm, d = x_ref.shape
    hd = 2 * HEAD_DIM
    h = _rms(x_ref[...], g_ref[...]).astype(BF16)
    qkv = jnp.dot(h, w_ref[...], preferred_element_type=F32)
    lane = lax.broadcasted_iota(jnp.int32, (tm, LANES), 1)
    first_half = (lane % HEAD_DIM) < (HEAD_DIM // 2)

    def rope(t, cos, sin):
        rot = jnp.where(first_half, pltpu.roll(t, LANES - HEAD_DIM // 2, 1), pltpu.roll(t, HEAD_DIM // 2, 1))
        return t * cos + rot * sin

    cos, sin = cos_ref[...], sin_ref[...]
    for j in range(d // hd):
        cols = slice(j * hd, (j + 1) * hd)
        qt_ref[j] = (rope(qkv[:, j * hd:(j + 1) * hd], cos, sin) * Q_SCALE).T.astype(BF16)
        k_ref[:, cols] = rope(qkv[:, d + j * hd:d + (j + 1) * hd], cos, sin).astype(BF16)
        vt_ref[j, 0:hd, :] = qkv[:, 2 * d + j * hd:2 * d + (j + 1) * hd].T.astype(BF16)
        vt_ref[j, hd:hd + ONES_ROWS, :] = jnp.ones((ONES_ROWS, tm), BF16)


def _attn_pre(x, g, w_qkv, tables, seq, tm):
    n, d = x.shape
    hd = 2 * HEAD_DIM
    heads = d // hd
    tps = seq // tm
    tab_spec = pl.BlockSpec((tm, LANES), lambda i: (i % tps, 0))
    return pl.pallas_call(
        _attn_pre_kernel,
        grid=(n // tm,),
        in_specs=[pl.BlockSpec((tm, d), lambda i: (i, 0)), _const_spec((1, d)), _const_spec((d, 3 * d)),
                  tab_spec, tab_spec],
        out_specs=[pl.BlockSpec((heads, hd, tm), lambda i: (0, 0, i)),
                   pl.BlockSpec((tm, d), lambda i: (i, 0)),
                   pl.BlockSpec((heads, hd + ONES_ROWS, tm), lambda i: (0, 0, i))],
        out_shape=[jax.ShapeDtypeStruct((heads, hd, n), BF16), jax.ShapeDtypeStruct((n, d), BF16),
                   jax.ShapeDtypeStruct((heads, hd + ONES_ROWS, n), BF16)],
        compiler_params=_params(1),
        name="attn_pre",
    )(x, g, w_qkv, *tables)


def _flash_kernel(qt_ref, k_ref, vt_ref, lq1_ref, lk1_ref, lq2_ref, lk2_ref, g_ref, o_ref,
                  q2t_s, m_s, acc_s, *bufs, tq, tk, group, lambda_init):
    hd = qt_ref.shape[1]
    seq = k_ref.shape[1]
    slots = 2 * group
    s_bufs, p_bufs, x_bufs, a_bufs = (bufs[i * slots:(i + 1) * slots] for i in range(4))
    gpt = seq // (tk * group)
    n_steps = (seq // tq) * gpt
    slabs_per_piece = PIECE_COLS // LANES
    vrows = vt_ref.shape[1]

    def load_queries(tile):
        qt = qt_ref[0, :, pl.ds(pl.multiple_of(tile * tq, tq), tq)]
        row = lax.broadcasted_iota(jnp.int32, qt.shape, 0)
        zero = jnp.zeros_like(qt)
        q2t_s[:, 0:tq] = jnp.where(row < HEAD_DIM, qt, zero)
        q2t_s[:, tq:2 * tq] = jnp.where(row >= HEAD_DIM, qt, zero)

    def finish(tile):
        acc = jnp.concatenate([acc_s[j, 0:vrows] for j in range(2 * tq // LANES)], axis=1)
        o_maps = acc[0:hd] / acc[hd:hd + 1]
        lam = (jnp.exp(jnp.sum(lq1_ref[...] * lk1_ref[...], axis=-1, keepdims=True))
               - jnp.exp(jnp.sum(lq2_ref[...] * lk2_ref[...], axis=-1, keepdims=True)) + lambda_init)
        o = (o_maps[:, 0:tq] - lam * o_maps[:, tq:2 * tq]).T
        o_ref[0, pl.ds(pl.multiple_of(tile * tq, tq), tq), :] = (
            _rms(o, g_ref[...]) * (1.0 - lambda_init)).astype(o_ref.dtype)
        acc_s[...] = jnp.zeros(acc_s.shape, F32)

    def work(values=None, softmax=None, scores=None):
        if softmax is not None:
            slots_m = range(softmax[1] * group, (softmax[1] + 1) * group)
            m_old = jnp.where(softmax[0] % gpt == 0, -jnp.inf, m_s[...])
            m_new = functools.reduce(jnp.maximum, [x_bufs[slot][...] for slot in slots_m], m_old)
            a_bufs[softmax[1]][...] = jnp.exp2(m_old - m_new)
            m_s[...] = m_new
        if values is not None:
            slots_v = range(values[1] * group, (values[1] + 1) * group)
            start_v = pl.multiple_of((values[0] % gpt) * group * tk, group * tk)
        for c in range(2 * tq // PIECE_COLS):
            cols = slice(c * PIECE_COLS, (c + 1) * PIECE_COLS)
            slabs = range(c * slabs_per_piece, (c + 1) * slabs_per_piece)
            if values is not None:
                p = jnp.concatenate([jnp.concatenate([p_bufs[slot][j, 0:tk] for slot in slots_v], axis=0)
                                     for j in slabs], axis=1)
                pv = jnp.dot(vt_ref[0, :, pl.ds(start_v, group * tk)], p, preferred_element_type=F32)
                for i, j in enumerate(slabs):
                    alpha = a_bufs[values[1]][:, j * LANES:(j + 1) * LANES]
                    acc_s[j, 0:vrows] = alpha * acc_s[j, 0:vrows] + pv[:, i * LANES:(i + 1) * LANES]
            for g in range(group):
                if softmax is not None:
                    slot_m = softmax[1] * group + g
                    for j in slabs:
                        lcols = slice(j * LANES, (j + 1) * LANES)
                        p_bufs[slot_m][j, 0:tk] = jnp.exp2(s_bufs[slot_m][j, 0:tk] - m_new[:, lcols]).astype(BF16)
                if scores is not None:
                    slot_s = scores[1] * group + g
                    start_s = pl.multiple_of(((scores[0] % gpt) * group + g) * tk, tk)
                    s = jnp.dot(k_ref[0, pl.ds(start_s, tk), :], q2t_s[:, cols], preferred_element_type=F32)
                    for i, j in enumerate(slabs):
                        s_bufs[slot_s][j, 0:tk] = s[:, i * LANES:(i + 1) * LANES]
                    x_bufs[slot_s][:, cols] = jnp.max(s, axis=0, keepdims=True)

    m_s[...] = jnp.full(m_s.shape, -jnp.inf, F32)
    acc_s[...] = jnp.zeros(acc_s.shape, F32)
    load_queries(0)
    work(scores=(0, 0))
    work(softmax=(0, 0), scores=(1, 1))

    def body(u, carry):
        pl.when((u + 2) % gpt == 0)(lambda: load_queries((u + 2) // gpt))
        for half in range(2):
            pl.when(u % 2 == half)(
                lambda half=half: work(values=(u, half), softmax=(u + 1, 1 - half), scores=(u + 2, half)))
        pl.when(u % gpt == gpt - 1)(lambda: finish(u // gpt))
        return carry

    lax.fori_loop(0, n_steps - 2, body, 0)
    work(values=(n_steps - 2, n_steps % 2), softmax=(n_steps - 1, (n_steps - 1) % 2))
    work(values=(n_steps - 1, (n_steps - 1) % 2))
    finish(n_steps // gpt - 1)


def _flash(qt, k, vt, lq1, lk1, lq2, lk2, subln_g, lambda_init, tq, tk, group):
    b, seq, d = k.shape
    heads, hd, _ = qt.shape
    vrows = vt.shape[1]
    slots = 2 * group
    gpt = seq // (tk * group)
    assert seq % (tk * group) == 0 and gpt >= 2 and gpt % 2 == 0, (seq, tk, group)

    def once(shape, index_map):
        return pl.BlockSpec(shape, index_map, pipeline_mode=pl.Buffered(1))

    small = _const_spec((1, HEAD_DIM))
    return pl.pallas_call(
        functools.partial(_flash_kernel, tq=tq, tk=tk, group=group, lambda_init=lambda_init),
        grid=(b, heads),
        in_specs=[once((1, hd, seq), lambda bi, h: (h, 0, bi)),
                  pl.BlockSpec((1, seq, hd), lambda bi, h: (bi, 0, h)),
                  pl.BlockSpec((1, vrows, seq), lambda bi, h: (h, 0, bi)),
                  small, small, small, small, _const_spec((1, hd))],
        out_specs=once((1, seq, hd), lambda bi, h: (bi, 0, h)),
        out_shape=jax.ShapeDtypeStruct((b, seq, d), BF16),
        scratch_shapes=[pltpu.VMEM((hd, 2 * tq), BF16), pltpu.VMEM((1, 2 * tq), F32),
                        pltpu.VMEM((2 * tq // LANES, vrows + F32_ROWS, LANES), F32)]
        + [pltpu.VMEM((2 * tq // LANES, tk + F32_ROWS, LANES), F32)] * slots
        + [pltpu.VMEM((2 * tq // LANES, tk + BF16_ROWS, LANES), BF16)] * slots
        + [pltpu.VMEM((1, 2 * tq), F32)] * slots
        + [pltpu.VMEM((1, 2 * tq), F32)] * 2,
        compiler_params=_params(2),
        name="flash_diff_attn",
    )(qt, k, vt, lq1, lk1, lq2, lk2, subln_g)


def _flash_short_kernel(qt_ref, k_ref, vt_ref, lq1_ref, lk1_ref, lq2_ref, lk2_ref, g_ref, o_ref,
                        q2t_s, *bufs, tk, lambda_init):
    hd, tq = qt_ref.shape[1], qt_ref.shape[2]
    seq = k_ref.shape[1]
    slabs_per_piece = PIECE_COLS // LANES
    s_bufs, p_bufs, x_bufs, acc_bufs = (bufs[2 * i:2 * i + 2] for i in range(4))
    step_id = pl.program_id(0)

    @pl.when(step_id == 0)
    def _():
        for buf in s_bufs + p_bufs + x_bufs:
            buf[...] = jnp.zeros(buf.shape, buf.dtype)
        for buf in acc_bufs:
            buf[...] = jnp.ones(buf.shape, buf.dtype)

    def step(par):
        acc = acc_bufs[1 - par][...]
        o_maps = acc[0:hd] / acc[hd:hd + 1]
        lam = (jnp.exp(jnp.sum(lq1_ref[...] * lk1_ref[...], axis=-1, keepdims=True))
               - jnp.exp(jnp.sum(lq2_ref[...] * lk2_ref[...], axis=-1, keepdims=True)) + lambda_init)
        o = (o_maps[:, 0:tq] - lam * o_maps[:, tq:2 * tq]).T
        o_ref[0] = (_rms(o, g_ref[...]) * (1.0 - lambda_init)).astype(o_ref.dtype)

        qt = qt_ref[0]
        row = lax.broadcasted_iota(jnp.int32, qt.shape, 0)
        zero = jnp.zeros_like(qt)
        q2t_s[:, 0:tq] = jnp.where(row < HEAD_DIM, qt, zero)
        q2t_s[:, tq:2 * tq] = jnp.where(row >= HEAD_DIM, qt, zero)

        m_prev = x_bufs[1 - par][...]
        vk = VALUE_CHUNKS * tk
        for c in range(2 * tq // PIECE_COLS):
            cols = slice(c * PIECE_COLS, (c + 1) * PIECE_COLS)
            slabs = range(c * slabs_per_piece, (c + 1) * slabs_per_piece)
            pv, col_max = None, None
            for g in range(seq // vk):
                krows = slice(g * vk, (g + 1) * vk)
                p = jnp.concatenate([p_bufs[par][j, krows] for j in slabs], axis=1)
                part = jnp.dot(vt_ref[0, :, krows], p, preferred_element_type=F32)
                pv = part if pv is None else pv + part
                for r in range(g * VALUE_CHUNKS, (g + 1) * VALUE_CHUNKS):
                    rows = slice(r * tk, (r + 1) * tk)
                    for j in slabs:
                        lcols = slice(j * LANES, (j + 1) * LANES)
                        p_bufs[1 - par][j, rows] = jnp.exp2(s_bufs[1 - par][j, rows] - m_prev[:, lcols]).astype(BF16)
                    s = jnp.dot(k_ref[0, rows, :], q2t_s[:, cols], preferred_element_type=F32)
                    for i, j in enumerate(slabs):
                        s_bufs[par][j, rows] = s[:, i * LANES:(i + 1) * LANES]
                    chunk_max = jnp.max(s, axis=0, keepdims=True)
                    col_max = chunk_max if col_max is None else jnp.maximum(col_max, chunk_max)
            acc_bufs[par][:, cols] = pv
            x_bufs[par][:, cols] = col_max

    for par in range(2):
        pl.when(step_id % 2 == par)(functools.partial(step, par))


def _flash_short(qt, k, vt, lq1, lk1, lq2, lk2, subln_g, lambda_init, tq, tk):
    b, seq, d = k.shape
    heads, hd, _ = qt.shape
    vrows = vt.shape[1]
    qtiles = seq // tq
    n_items = b * heads * qtiles
    assert seq % (VALUE_CHUNKS * tk) == 0, (seq, tk)

    def item(i, lag):
        i = jnp.clip(i - lag, 0, n_items - 1)
        return i // (qtiles * heads), (i // qtiles) % heads, i % qtiles

    small = _const_spec((1, HEAD_DIM))
    n_slabs = 2 * tq // LANES
    return pl.pallas_call(
        functools.partial(_flash_short_kernel, tk=tk, lambda_init=lambda_init),
        grid=(n_items + 3,),
        in_specs=[pl.BlockSpec((1, hd, tq), lambda i: (item(i, 0)[1], 0, item(i, 0)[0] * qtiles + item(i, 0)[2])),
                  pl.BlockSpec((1, seq, hd), lambda i: (item(i, 0)[0], 0, item(i, 0)[1])),
                  pl.BlockSpec((1, vrows, seq), lambda i: (item(i, 2)[1], 0, item(i, 2)[0])),
                  small, small, small, small, _const_spec((1, hd))],
        out_specs=pl.BlockSpec((1, tq, hd), lambda i: (item(i, 3)[0], item(i, 3)[2], item(i, 3)[1])),
        out_shape=jax.ShapeDtypeStruct((b, seq, d), BF16),
        scratch_shapes=[pltpu.VMEM((hd, 2 * tq), BF16)]
        + [pltpu.VMEM((n_slabs, seq + F32_ROWS, LANES), F32)] * 2
        + [pltpu.VMEM((n_slabs, seq + BF16_ROWS, LANES), BF16)] * 2
        + [pltpu.VMEM((1, 2 * tq), F32)] * 2
        + [pltpu.VMEM((vrows, 2 * tq), F32)] * 2,
        compiler_params=_params(1),
        name="flash_diff_attn_short",
    )(qt, k, vt, lq1, lk1, lq2, lk2, subln_g)


def _attn_post_kernel(x_ref, o_ref, w_ref, y_ref):
    y_ref[...] = x_ref[...] + jnp.dot(o_ref[...], w_ref[...], preferred_element_type=F32)


def _attn_post(x, o, w_o, tm):
    n, d = x.shape
    row = pl.BlockSpec((tm, d), lambda i: (i, 0))
    return pl.pallas_call(
        _attn_post_kernel,
        grid=(n // tm,),
        in_specs=[row, row, _const_spec((d, d))],
        out_specs=row,
        out_shape=jax.ShapeDtypeStruct((n, d), F32),
        compiler_params=_params(1),
        name="attn_post",
    )(x, o, w_o)


def _ffn_ple_kernel(x_ref, xp_ref, xn_ref, p_ref, gf_ref, wup_ref, cw_ref, cb_ref, wdn_ref,
                    gp_ref, wg_ref, wp_ref, gfin_ref, y_ref, a_s, act_s, *, tiles_per_seq, fc, final):
    tm, d = x_ref.shape
    f = wdn_ref.shape[0]
    i = pl.program_id(0)
    keep_prev = jnp.where(i % tiles_per_seq == 0, 0.0, 1.0).astype(F32)
    keep_next = jnp.where(i % tiles_per_seq == tiles_per_seq - 1, 0.0, 1.0).astype(F32)
    x = x_ref[...]
    gf = gf_ref[...]
    h = jnp.concatenate([_rms(xp_ref[...], gf) * keep_prev, _rms(x, gf), _rms(xn_ref[...], gf) * keep_next], axis=0)
    a_s[...] = jnp.dot(h.astype(BF16), wup_ref[...], preferred_element_type=F32)

    def conv(col0):
        cols = pl.ds(col0, fc)
        c = cb_ref[:, cols]
        for t in range(CONV_WIDTH):
            c = c + a_s[pl.ds(HALO - 1 + t, tm), cols] * cw_ref[pl.ds(t, 1), cols]
        return c

    for j in range(f // fc):
        val = conv(j * fc)
        gate = conv(f + j * fc)
        act_s[:, pl.ds(j * fc, fc)] = (gate * jax.nn.sigmoid(gate) * val).astype(BF16)

    x = x + jnp.dot(act_s[...], wdn_ref[...], preferred_element_type=F32)
    gate = jax.nn.sigmoid(jnp.dot(_rms(x, gp_ref[...]).astype(BF16), wg_ref[...], preferred_element_type=F32))
    x = x + gate * jnp.dot(p_ref[...].astype(BF16), wp_ref[...], preferred_element_type=F32)
    if final:
        x = _rms(x, gfin_ref[...])
    y_ref[...] = x


def _ffn_ple(x, p, g_ffn, w_up, conv_w, conv_b, w_down, g_ple, w_gate, w_proj, g_final, layer, seq, tm, final):
    n, d = x.shape
    f = w_down.shape[1]
    pd = p.shape[2]

    def layer_spec(*shape):
        return pl.BlockSpec((None,) + shape, lambda i: (layer,) + (0,) * len(shape), pipeline_mode=pl.Buffered(1))

    fc = 2 * LANES
    tiles_per_seq = seq // tm
    hb = tm // HALO
    last_blk = n // HALO - 1
    return pl.pallas_call(
        functools.partial(_ffn_ple_kernel, tiles_per_seq=tiles_per_seq, fc=fc, final=final),
        grid=(n // tm,),
        in_specs=[pl.BlockSpec((tm, d), lambda i: (i, 0)),
                  pl.BlockSpec((HALO, d), lambda i: (jnp.maximum(i * hb - 1, 0), 0)),
                  pl.BlockSpec((HALO, d), lambda i: (jnp.minimum((i + 1) * hb, last_blk), 0)),
                  pl.BlockSpec((None, tm, pd), lambda i: (layer, i, 0)),
                  layer_spec(1, d), layer_spec(d, 2 * f), layer_spec(CONV_WIDTH, 2 * f),
                  layer_spec(1, 2 * f), layer_spec(f, d), layer_spec(1, d), layer_spec(d, d),
                  layer_spec(pd, d), _const_spec((1, d))],
        out_specs=pl.BlockSpec((tm, d), lambda i: (i, 0)),
        out_shape=jax.ShapeDtypeStruct((n, d), F32),
        scratch_shapes=[pltpu.VMEM((tm + 2 * HALO, 2 * f), F32), pltpu.VMEM((tm, f), BF16)],
        compiler_params=_params(1),
        name="ffn_ple_final" if final else "ffn_ple",
    )(x, x, x, p, g_ffn, w_up, conv_w, conv_b, w_down, g_ple, w_gate, w_proj, g_final)


def _sgu_kernel(x_ref, g_ref, wuv_ref, lng_ref, lnb_ref, ws_ref, bs_ref, wout_ref, y_ref, um_s):
    tm, d = x_ref.shape
    w = wout_ref.shape[0]
    x = x_ref[...]
    z = jnp.dot(_rms(x, g_ref[...]).astype(BF16), wuv_ref[...], preferred_element_type=F32)
    z = 0.5 * z * (1.0 + lax.erf(z * (1.0 / math.sqrt(2.0))))
    v = z[:, w:]
    vc = v - jnp.mean(v, axis=-1, keepdims=True)
    v = vc * lax.rsqrt(jnp.mean(vc * vc, axis=-1, keepdims=True) + EPS) * lng_ref[...] + lnb_ref[...]
    vb = v.astype(BF16)
    for c in range(tm // CHUNK):
        rows = slice(c * CHUNK, (c + 1) * CHUNK)
        for grp in range(w // CHUNK):
            cols = slice(grp * CHUNK, (grp + 1) * CHUNK)
            mixed = jnp.dot(ws_ref[grp], vb[rows, cols], preferred_element_type=F32) + bs_ref[grp]
            um_s[rows, cols] = (z[rows, cols] * mixed).astype(BF16)
    y_ref[...] = x + jnp.dot(um_s[...], wout_ref[...], preferred_element_type=F32)


def _sgu(x, g, w_uv, ln_g, ln_b, w_s, b_s, w_out, tm):
    n, d = x.shape
    w = w_out.shape[0]
    ng = w // CHUNK
    row = pl.BlockSpec((tm, d), lambda i: (i, 0))
    return pl.pallas_call(
        _sgu_kernel,
        grid=(n // tm,),
        in_specs=[row, _const_spec((1, d)), _const_spec((d, 2 * w)), _const_spec((1, w)), _const_spec((1, w)),
                  _const_spec((ng, CHUNK, CHUNK)), _const_spec((ng, CHUNK, 1)), _const_spec((w, d))],
        out_specs=row,
        out_shape=jax.ShapeDtypeStruct((n, d), F32),
        scratch_shapes=[pltpu.VMEM((tm, w), BF16)],
        compiler_params=_params(1),
        name="sgu",
    )(x, g, w_uv, ln_g, ln_b, w_s, b_s, w_out)


def _rope_tables(seq):
    half = HEAD_DIM // 2
    inv = 1.0 / (ROPE_THETA ** (jnp.arange(0, HEAD_DIM, 2, dtype=F32) / HEAD_DIM))
    ang = jnp.arange(seq, dtype=F32)[:, None] * inv[None, :]
    cos, sin = jnp.cos(ang), jnp.sin(ang)
    sign = jnp.where((jnp.arange(LANES) % HEAD_DIM) < half, -1.0, 1.0).astype(F32)
    reps = LANES // half
    return jnp.concatenate([cos] * reps, axis=-1), jnp.concatenate([sin] * reps, axis=-1) * sign


def _flash_tiles(seq):
    tq = _tile(seq, 2048)
    tk = _tile(seq, LANES)
    group = max(1, min(4, seq // tk // 4))
    return tq, tk, group


def _tile(n, want):
    t = min(n, want)
    assert n % t == 0, (n, t)
    return t


def _trunk(x, p, w, rope):
    b, seq, d = x.shape
    n = b * seq
    x = x.reshape(n, d)
    tm = _tile(seq, 512)
    tm_wide = _tile(seq, 1024)
    qt, k, vt = _attn_pre(x, w["norm_mix_g"][0], w["attn_w_qkv"], rope, seq, tm_wide)
    lambda_init = 0.8 - 0.6 * math.exp(-0.3 * 0)
    lam_args = (w["attn_lq1"], w["attn_lk1"], w["attn_lq2"], w["attn_lk2"], w["attn_subln_g"], lambda_init)
    if seq <= SHORT_SEQ:
        o = _flash_short(qt, k.reshape(b, seq, d), vt, *lam_args, _tile(seq, 512), 2 * LANES)
    else:
        o = _flash(qt, k.reshape(b, seq, d), vt, *lam_args, *_flash_tiles(seq))
    x = _attn_post(x, o.reshape(n, d), w["attn_w_o"], _tile(seq, 2048))
    p = p.reshape(p.shape[0], n, p.shape[-1])
    ffn = functools.partial(_ffn_ple, p=p, g_ffn=w["norm_ffn_g"], w_up=w["ffn_w_up"], conv_w=w["ffn_conv_w"],
                            conv_b=w["ffn_conv_b"], w_down=w["ffn_w_down"], g_ple=w["norm_ple_g"],
                            w_gate=w["ple_w_gate"], w_proj=w["ple_w_proj"], g_final=w["final_norm_g"], seq=seq, tm=tm)
    x = ffn(x, layer=0, final=False)
    x = _sgu(x, w["norm_mix_g"][1], w["sgu_w_uv"], w["sgu_ln_g"], w["sgu_ln_b"], w["sgu_w_s"], w["sgu_b_s"],
             w["sgu_w_out"], tm_wide)
    x = ffn(x, layer=1, final=True)
    return x.reshape(b, seq, d)


def kernel(x_prompt, x_sample, p_prompt, p_sample, norm_mix_g, attn_w_qkv, attn_lq1, attn_lk1, attn_lq2, attn_lk2, attn_subln_g, attn_w_o, sgu_w_uv, sgu_ln_g, sgu_ln_b, sgu_w_s, sgu_b_s, sgu_w_out, norm_ffn_g, ffn_w_up, ffn_conv_w, ffn_conv_b, ffn_w_down, norm_ple_g, ple_w_gate, ple_w_proj, final_norm_g):
    depth = norm_mix_g.shape[0]
    assert depth == 2 and attn_w_qkv.shape[0] == 1 and sgu_w_uv.shape[0] == 1
    d = x_prompt.shape[-1]
    row = lambda a: a.reshape(a.shape[:-1] + (1, a.shape[-1]))
    w = dict(
        norm_mix_g=row(norm_mix_g), norm_ffn_g=row(norm_ffn_g), norm_ple_g=row(norm_ple_g),
        final_norm_g=final_norm_g.reshape(1, d),
        attn_w_qkv=attn_w_qkv[0].astype(BF16), attn_w_o=attn_w_o[0].astype(BF16),
        attn_lq1=attn_lq1, attn_lk1=attn_lk1, attn_lq2=attn_lq2, attn_lk2=attn_lk2, attn_subln_g=attn_subln_g,
        sgu_w_uv=sgu_w_uv[0].astype(BF16), sgu_ln_g=sgu_ln_g, sgu_ln_b=sgu_ln_b,
        sgu_w_s=sgu_w_s[0].astype(BF16), sgu_b_s=sgu_b_s[0][:, :, None], sgu_w_out=sgu_w_out[0].astype(BF16),
        ffn_w_up=ffn_w_up.astype(BF16), ffn_conv_w=ffn_conv_w, ffn_conv_b=row(ffn_conv_b),
        ffn_w_down=ffn_w_down.astype(BF16), ple_w_gate=ple_w_gate.astype(BF16), ple_w_proj=ple_w_proj.astype(BF16),
    )
    rope = _rope_tables(max(x_prompt.shape[1], x_sample.shape[1]))
    return _trunk(x_prompt, p_prompt, w, rope), _trunk(x_sample, p_sample, w, rope)
```
